```python
import jax, jax.numpy as jnp
from jax import lax
import numpy as np

D_MODEL = 1024
BATCH = 8
SEQ = 4096
DEPTH = 2

D_MIX = D_MODEL
D_CONF = D_MIX // 2
D_SC = D_MIX - D_CONF
N_GROUPS_CONF = 8
N_GROUPS_SC = 8
CONF_KERNEL = 31
SC_KERNEL = 3
FFN_KERNEL = 3
D_FF = 2816
D_IN = 2 * D_CONF + 3 * D_SC
EPS = 1e-6

kernel_name = "hybrid_conformer_shortconv_convffn"


def rmsnorm(x, g):
    xf = x.astype(jnp.float32)
    y = xf * lax.rsqrt(jnp.mean(xf * xf, axis=-1, keepdims=True) + EPS)
    return (y * g.astype(jnp.float32)).astype(x.dtype)


def layernorm(x, g, b):
    xf = x.astype(jnp.float32)
    mu = jnp.mean(xf, axis=-1, keepdims=True)
    xc = xf - mu
    var = jnp.mean(xc * xc, axis=-1, keepdims=True)
    y = xc * lax.rsqrt(var + EPS)
    return (y * g.astype(jnp.float32) + b.astype(jnp.float32)).astype(x.dtype)


def causal_dwconv(x, w):
    k, c = w.shape
    return lax.conv_general_dilated(
        x, w[:, None, :].astype(x.dtype),
        window_strides=(1,), padding=[(k - 1, 0)],
        dimension_numbers=("NWC", "WIO", "NWC"),
        feature_group_count=c)


def _fwd_setup_inputs(seed: int = 0) -> dict:
    key = jax.random.key(seed)
    ks = jax.random.split(key, 16)
    f32 = jnp.float32
    x = jax.random.normal(ks[0], (BATCH, SEQ, D_MODEL), f32)
    mix_norm_g = 1.0 + 0.02 * jax.random.normal(ks[1], (DEPTH, D_MODEL), f32)
    w_in = jax.random.normal(ks[2], (DEPTH, D_MODEL, D_IN), f32) * D_MODEL ** -0.5
    b_in = 0.02 * jax.random.normal(ks[3], (DEPTH, D_IN), f32)
    conv_a_w = jax.random.normal(ks[4], (DEPTH, CONF_KERNEL, D_CONF), f32) * CONF_KERNEL ** -0.5
    conv_a_b = 0.02 * jax.random.normal(ks[5], (DEPTH, D_CONF), f32)
    ln_a_g = 1.0 + 0.02 * jax.random.normal(ks[6], (DEPTH, D_CONF), f32)
    ln_a_b = 0.02 * jax.random.normal(ks[7], (DEPTH, D_CONF), f32)
    conv_b_w = jax.random.normal(ks[8], (DEPTH, SC_KERNEL, D_SC), f32) * SC_KERNEL ** -0.5
    w_out = jax.random.normal(ks[9], (DEPTH, D_MIX, D_MODEL), f32) * D_MIX ** -0.5
    ffn_norm_g = 1.0 + 0.02 * jax.random.normal(ks[10], (DEPTH, D_MODEL), f32)
    w_up = jax.random.normal(ks[11], (DEPTH, D_MODEL, 2 * D_FF), f32) * D_MODEL ** -0.5
    conv_f_w = jax.random.normal(ks[12], (DEPTH, FFN_KERNEL, 2 * D_FF), f32) * FFN_KERNEL ** -0.5
    w_down = jax.random.normal(ks[13], (DEPTH, D_FF, D_MODEL), f32) * D_FF ** -0.5
    final_norm_g = 1.0 + 0.02 * jax.random.normal(ks[14], (D_MODEL,), f32)
    return {"x": x, "mix_norm_g": mix_norm_g, "w_in": w_in, "b_in": b_in,
            "conv_a_w": conv_a_w, "conv_a_b": conv_a_b, "ln_a_g": ln_a_g, "ln_a_b": ln_a_b,
            "conv_b_w": conv_b_w, "w_out": w_out, "ffn_norm_g": ffn_norm_g, "w_up": w_up,
            "conv_f_w": conv_f_w, "w_down": w_down, "final_norm_g": final_norm_g}


def token_mixer(h, w_in, b_in, conv_a_w, conv_a_b, ln_a_g, ln_a_b, conv_b_w, w_out):
    u = jnp.einsum("bsd,de->bse", h, w_in) + b_in.astype(h.dtype)
    a_val, a_gate, g_b, g_c, v_sc = jnp.split(
        u, [D_CONF, 2 * D_CONF, 2 * D_CONF + D_SC, 2 * D_CONF + 2 * D_SC], axis=-1)
    a = a_val * jax.nn.sigmoid(a_gate)
    a = causal_dwconv(a, conv_a_w) + conv_a_b.astype(a.dtype)
    a = jax.nn.silu(layernorm(a, ln_a_g, ln_a_b))
    s = g_b * causal_dwconv(g_c * v_sc, conv_b_w)
    y = jnp.concatenate([a, s], axis=-1)
    return jnp.einsum("bse,ed->bsd", y, w_out)


def conv_ffn(h, w_up, conv_f_w, w_down):
    u = jnp.einsum("bsd,df->bsf", h, w_up)
    u = causal_dwconv(u, conv_f_w)
    gate, val = jnp.split(u, 2, axis=-1)
    return jnp.einsum("bsf,fd->bsd", jax.nn.silu(gate) * val, w_down)


def _fwd_reference(x, mix_norm_g, w_in, b_in, conv_a_w, conv_a_b, ln_a_g, ln_a_b,
              conv_b_w, w_out, ffn_norm_g, w_up, conv_f_w, w_down, final_norm_g):
    for l in range(DEPTH):
        h = rmsnorm(x, mix_norm_g[l])
        x = x + token_mixer(h, w_in[l], b_in[l], conv_a_w[l], conv_a_b[l],
                            ln_a_g[l], ln_a_b[l], conv_b_w[l], w_out[l])
        h = rmsnorm(x, ffn_norm_g[l])
        x = x + conv_ffn(h, w_up[l], conv_f_w[l], w_down[l])
    return rmsnorm(x, final_norm_g)


import jax as _jax
import jax.numpy as _jnp

TWIN_FORMAT = 'train_step'
FWD_PARAMS = ['x', 'mix_norm_g', 'w_in', 'b_in', 'conv_a_w', 'conv_a_b', 'ln_a_g', 'ln_a_b', 'conv_b_w', 'w_out', 'ffn_norm_g', 'w_up', 'conv_f_w', 'w_down', 'final_norm_g']
TWIN_WEIGHTS = ['mix_norm_g', 'w_in', 'b_in', 'conv_a_w', 'conv_a_b', 'ln_a_g', 'ln_a_b', 'conv_b_w', 'w_out', 'ffn_norm_g', 'w_up', 'conv_f_w', 'w_down', 'final_norm_g']
TWIN_DIFF_INPUT = 'x'
TWIN_INPUTS = ['x', 'mix_norm_g', 'w_in', 'b_in', 'conv_a_w', 'conv_a_b', 'ln_a_g', 'ln_a_b', 'conv_b_w', 'w_out', 'ffn_norm_g', 'w_up', 'conv_f_w', 'w_down', 'final_norm_g', 'loss_target', 'm_mix_norm_g', 'm_w_in', 'm_b_in', 'm_conv_a_w', 'm_conv_a_b', 'm_ln_a_g', 'm_ln_a_b', 'm_conv_b_w', 'm_w_out', 'm_ffn_norm_g', 'm_w_up', 'm_conv_f_w', 'm_w_down', 'm_final_norm_g', 'v_mix_norm_g', 'v_w_in', 'v_b_in', 'v_conv_a_w', 'v_conv_a_b', 'v_ln_a_g', 'v_ln_a_b', 'v_conv_b_w', 'v_w_out', 'v_ffn_norm_g', 'v_w_up', 'v_conv_f_w', 'v_w_down', 'v_final_norm_g']
TWIN_OUTPUTS = ['loss', 'grad_x', 'grad_mix_norm_g', 'grad_w_in', 'grad_b_in', 'grad_conv_a_w', 'grad_conv_a_b', 'grad_ln_a_g', 'grad_ln_a_b', 'grad_conv_b_w', 'grad_w_out', 'grad_ffn_norm_g', 'grad_w_up', 'grad_conv_f_w', 'grad_w_down', 'grad_final_norm_g', 'delta_mix_norm_g', 'delta_w_in', 'delta_b_in', 'delta_conv_a_w', 'delta_conv_a_b', 'delta_ln_a_g', 'delta_ln_a_b', 'delta_conv_b_w', 'delta_w_out', 'delta_ffn_norm_g', 'delta_w_up', 'delta_conv_f_w', 'delta_w_down', 'delta_final_norm_g', 'new_m_mix_norm_g', 'new_m_w_in', 'new_m_b_in', 'new_m_conv_a_w', 'new_m_conv_a_b', 'new_m_ln_a_g', 'new_m_ln_a_b', 'new_m_conv_b_w', 'new_m_w_out', 'new_m_ffn_norm_g', 'new_m_w_up', 'new_m_conv_f_w', 'new_m_w_down', 'new_m_final_norm_g', 'new_v_mix_norm_g', 'new_v_w_in', 'new_v_b_in', 'new_v_conv_a_w', 'new_v_conv_a_b', 'new_v_ln_a_g', 'new_v_ln_a_b', 'new_v_conv_b_w', 'new_v_w_out', 'new_v_ffn_norm_g', 'new_v_w_up', 'new_v_conv_f_w', 'new_v_w_down', 'new_v_final_norm_g']
TWIN_LEAF_KINDS = {'loss': 'loss', 'grad_x': 'grad_x', 'grad_mix_norm_g': 'grad_w', 'grad_w_in': 'grad_w', 'grad_b_in': 'grad_w', 'grad_conv_a_w': 'grad_w', 'grad_conv_a_b': 'grad_w', 'grad_ln_a_g': 'grad_w', 'grad_ln_a_b': 'grad_w', 'grad_conv_b_w': 'grad_w', 'grad_w_out': 'grad_w', 'grad_ffn_norm_g': 'grad_w', 'grad_w_up': 'grad_w', 'grad_conv_f_w': 'grad_w', 'grad_w_down': 'grad_w', 'grad_final_norm_g': 'grad_w', 'delta_mix_norm_g': 'delta_w', 'delta_w_in': 'delta_w', 'delta_b_in': 'delta_w', 'delta_conv_a_w': 'delta_w', 'delta_conv_a_b': 'delta_w', 'delta_ln_a_g': 'delta_w', 'delta_ln_a_b': 'delta_w', 'delta_conv_b_w': 'delta_w', 'delta_w_out': 'delta_w', 'delta_ffn_norm_g': 'delta_w', 'delta_w_up': 'delta_w', 'delta_conv_f_w': 'delta_w', 'delta_w_down': 'delta_w', 'delta_final_norm_g': 'delta_w', 'new_m_mix_norm_g': 'new_m', 'new_m_w_in': 'new_m', 'new_m_b_in': 'new_m', 'new_m_conv_a_w': 'new_m', 'new_m_conv_a_b': 'new_m', 'new_m_ln_a_g': 'new_m', 'new_m_ln_a_b': 'new_m', 'new_m_conv_b_w': 'new_m', 'new_m_w_out': 'new_m', 'new_m_ffn_norm_g': 'new_m', 'new_m_w_up': 'new_m', 'new_m_conv_f_w': 'new_m', 'new_m_w_down': 'new_m', 'new_m_final_norm_g': 'new_m', 'new_v_mix_norm_g': 'new_v', 'new_v_w_in': 'new_v', 'new_v_b_in': 'new_v', 'new_v_conv_a_w': 'new_v', 'new_v_conv_a_b': 'new_v', 'new_v_ln_a_g': 'new_v', 'new_v_ln_a_b': 'new_v', 'new_v_conv_b_w': 'new_v', 'new_v_w_out': 'new_v', 'new_v_ffn_norm_g': 'new_v', 'new_v_w_up': 'new_v', 'new_v_conv_f_w': 'new_v', 'new_v_w_down': 'new_v', 'new_v_final_norm_g': 'new_v'}


def _forward(args):
    return _fwd_reference(*[args[k] for k in FWD_PARAMS])


def _output_shape():
    out = _jax.eval_shape(lambda: _forward(_fwd_setup_inputs(0)))
    return out.shape, out.dtype

N_MICROBATCH = 1
ADAM_LR = 0.001
ADAM_B1 = 0.9
ADAM_B2 = 0.999
ADAM_EPS = 1e-08
ADAM_WD = 0.01
ADAM_STEP = 10
PER_EXAMPLE_BATCH_AXIS = {'x': 0, 'loss_target': 0}
SHARED_INPUTS = []
_WEIGHT_DTYPES = {'mix_norm_g': _jnp.float32, 'w_in': _jnp.float32, 'b_in': _jnp.float32, 'conv_a_w': _jnp.float32, 'conv_a_b': _jnp.float32, 'ln_a_g': _jnp.float32, 'ln_a_b': _jnp.float32, 'conv_b_w': _jnp.float32, 'w_out': _jnp.float32, 'ffn_norm_g': _jnp.float32, 'w_up': _jnp.float32, 'conv_f_w': _jnp.float32, 'w_down': _jnp.float32, 'final_norm_g': _jnp.float32}
MOMENT_SCALE = {'mix_norm_g': 2.062344e-01, 'w_in': 1.290680e-01, 'b_in': 1.349227e-01, 'conv_a_w': 9.757971e-02, 'conv_a_b': 2.160581e-01, 'ln_a_g': 1.158070e-01, 'ln_a_b': 1.054620e-01, 'conv_b_w': 1.663941e-01, 'w_out': 1.278991e-01, 'ffn_norm_g': 1.156142e-01, 'w_up': 5.001528e-02, 'conv_f_w': 4.976509e-02, 'w_down': 8.175200e-02, 'final_norm_g': 3.203884e+01}


def _to_microbatches(a, axis):
    t = _jnp.moveaxis(a, axis, 0)
    t = t.reshape((N_MICROBATCH, t.shape[0] // N_MICROBATCH) + t.shape[1:])
    return _jnp.moveaxis(t, 1, axis + 1)


def setup_inputs(seed: int = 0) -> dict:
    inp = _fwd_setup_inputs(seed)
    key = _jax.random.fold_in(_jax.random.key(seed), 7919)
    shape, _ = _output_shape()
    out = dict(inp)
    out["loss_target"] = _jax.random.normal(_jax.random.fold_in(key, 0), shape, _jnp.float32)
    for i, name in enumerate(TWIN_WEIGHTS):
        w = inp[name].astype(_jnp.float32)
        if MOMENT_SCALE is None:
            s = _jnp.sqrt(_jnp.mean(_jnp.square(w)) + 1e-30)
        else:
            s = MOMENT_SCALE[name]
        km, kv = _jax.random.split(_jax.random.fold_in(key, i + 1))
        out[name] = w
        out["m_" + name] = s * _jax.random.normal(km, w.shape, _jnp.float32)
        out["v_" + name] = (s * s) * _jax.random.uniform(kv, w.shape, _jnp.float32, 0.5, 1.5)
    if N_MICROBATCH > 1:
        for name, axis in PER_EXAMPLE_BATCH_AXIS.items():
            out[name] = _to_microbatches(out[name], axis)
    return {'x': out['x'], 'mix_norm_g': out['mix_norm_g'], 'w_in': out['w_in'], 'b_in': out['b_in'], 'conv_a_w': out['conv_a_w'], 'conv_a_b': out['conv_a_b'], 'ln_a_g': out['ln_a_g'], 'ln_a_b': out['ln_a_b'], 'conv_b_w': out['conv_b_w'], 'w_out': out['w_out'], 'ffn_norm_g': out['ffn_norm_g'], 'w_up': out['w_up'], 'conv_f_w': out['conv_f_w'], 'w_down': out['w_down'], 'final_norm_g': out['final_norm_g'], 'loss_target': out['loss_target'], 'm_mix_norm_g': out['m_mix_norm_g'], 'm_w_in': out['m_w_in'], 'm_b_in': out['m_b_in'], 'm_conv_a_w': out['m_conv_a_w'], 'm_conv_a_b': out['m_conv_a_b'], 'm_ln_a_g': out['m_ln_a_g'], 'm_ln_a_b': out['m_ln_a_b'], 'm_conv_b_w': out['m_conv_b_w'], 'm_w_out': out['m_w_out'], 'm_ffn_norm_g': out['m_ffn_norm_g'], 'm_w_up': out['m_w_up'], 'm_conv_f_w': out['m_conv_f_w'], 'm_w_down': out['m_w_down'], 'm_final_norm_g': out['m_final_norm_g'], 'v_mix_norm_g': out['v_mix_norm_g'], 'v_w_in': out['v_w_in'], 'v_b_in': out['v_b_in'], 'v_conv_a_w': out['v_conv_a_w'], 'v_conv_a_b': out['v_conv_a_b'], 'v_ln_a_g': out['v_ln_a_g'], 'v_ln_a_b': out['v_ln_a_b'], 'v_conv_b_w': out['v_conv_b_w'], 'v_w_out': out['v_w_out'], 'v_ffn_norm_g': out['v_ffn_norm_g'], 'v_w_up': out['v_w_up'], 'v_conv_f_w': out['v_conv_f_w'], 'v_w_down': out['v_w_down'], 'v_final_norm_g': out['v_final_norm_g']}


def _loss(weights, diff, rest, loss_target):
    with _jax.named_scope("forward"):
        args = {**rest, TWIN_DIFF_INPUT: diff, **{k: w.astype(_WEIGHT_DTYPES[k]) for k, w in weights.items()}}
        y = _forward(args)
    with _jax.named_scope("loss_head"):
        err = _jnp.square(y.astype(_jnp.float32) - loss_target)
        return 0.5 * _jnp.sum(_jnp.mean(err, axis=-1)) if err.ndim else 0.5 * err


def _adamw(w, g, m, v):
    m = ADAM_B1 * m + (1.0 - ADAM_B1) * g
    v = ADAM_B2 * v + (1.0 - ADAM_B2) * _jnp.square(g)
    m_hat = m / (1.0 - ADAM_B1 ** ADAM_STEP)
    v_hat = v / (1.0 - ADAM_B2 ** ADAM_STEP)
    delta = -ADAM_LR * (m_hat / (_jnp.sqrt(v_hat) + ADAM_EPS) + ADAM_WD * w)
    return delta, m, v


def reference(x, mix_norm_g, w_in, b_in, conv_a_w, conv_a_b, ln_a_g, ln_a_b, conv_b_w, w_out, ffn_norm_g, w_up, conv_f_w, w_down, final_norm_g, loss_target, m_mix_norm_g, m_w_in, m_b_in, m_conv_a_w, m_conv_a_b, m_ln_a_g, m_ln_a_b, m_conv_b_w, m_w_out, m_ffn_norm_g, m_w_up, m_conv_f_w, m_w_down, m_final_norm_g, v_mix_norm_g, v_w_in, v_b_in, v_conv_a_w, v_conv_a_b, v_ln_a_g, v_ln_a_b, v_conv_b_w, v_w_out, v_ffn_norm_g, v_w_up, v_conv_f_w, v_w_down, v_final_norm_g):
    given = dict(x=x, mix_norm_g=mix_norm_g, w_in=w_in, b_in=b_in, conv_a_w=conv_a_w, conv_a_b=conv_a_b, ln_a_g=ln_a_g, ln_a_b=ln_a_b, conv_b_w=conv_b_w, w_out=w_out, ffn_norm_g=ffn_norm_g, w_up=w_up, conv_f_w=conv_f_w, w_down=w_down, final_norm_g=final_norm_g, loss_target=loss_target, m_mix_norm_g=m_mix_norm_g, m_w_in=m_w_in, m_b_in=m_b_in, m_conv_a_w=m_conv_a_w, m_conv_a_b=m_conv_a_b, m_ln_a_g=m_ln_a_g, m_ln_a_b=m_ln_a_b, m_conv_b_w=m_conv_b_w, m_w_out=m_w_out, m_ffn_norm_g=m_ffn_norm_g, m_w_up=m_w_up, m_conv_f_w=m_conv_f_w, m_w_down=m_w_down, m_final_norm_g=m_final_norm_g, v_mix_norm_g=v_mix_norm_g, v_w_in=v_w_in, v_b_in=v_b_in, v_conv_a_w=v_conv_a_w, v_conv_a_b=v_conv_a_b, v_ln_a_g=v_ln_a_g, v_ln_a_b=v_ln_a_b, v_conv_b_w=v_conv_b_w, v_w_out=v_w_out, v_ffn_norm_g=v_ffn_norm_g, v_w_up=v_w_up, v_conv_f_w=v_conv_f_w, v_w_down=v_w_down, v_final_norm_g=v_final_norm_g)
    weights = {n: given[n] for n in TWIN_WEIGHTS}
    shared = {n: given[n] for n in SHARED_INPUTS}
    per_example = {n: given[n] for n in ['x']}
    grad_fn = _jax.value_and_grad(_loss, argnums=(0, 1))

    def one_microbatch(ex, loss_target):
        ex = dict(ex)
        diff = ex.pop(TWIN_DIFF_INPUT)
        return grad_fn(weights, diff, {**shared, **ex}, loss_target)

    if N_MICROBATCH == 1:
        loss, (grad_w, grad_x) = one_microbatch(per_example, given["loss_target"])
    else:
        def body(carry, xs):
            loss_sum, grad_sum = carry
            l_k, (gw_k, gx_k) = one_microbatch(xs[0], xs[1])
            with _jax.named_scope("update"):
                return (loss_sum + l_k, _jax.tree.map(_jnp.add, grad_sum, gw_k)), gx_k

        init = (_jnp.zeros((), _jnp.float32), _jax.tree.map(_jnp.zeros_like, weights))
        (loss, grad_w), grad_x = _jax.lax.scan(body, init, (per_example, given["loss_target"]))
    with _jax.named_scope("update"):
        delta_w, new_m, new_v = {}, {}, {}
        for n in TWIN_WEIGHTS:
            delta_w[n], new_m[n], new_v[n] = _adamw(weights[n], grad_w[n], given["m_" + n], given["v_" + n])
    return (loss, grad_x, *[grad_w[n] for n in TWIN_WEIGHTS], *[delta_w[n] for n in TWIN_WEIGHTS],
            *[new_m[n] for n in TWIN_WEIGHTS], *[new_v[n] for n in TWIN_WEIGHTS])
```

```python
import functools

import jax
import jax.numpy as jnp
from jax import lax
from jax.experimental import pallas as pl
from jax.experimental.pallas import tpu as pltpu

F32 = jnp.float32
BF16 = jnp.bfloat16
MESH = pl.DeviceIdType.MESH
ANY = pl.BlockSpec(memory_space=pl.ANY)
VMEM = pl.BlockSpec(memory_space=pltpu.VMEM)

EPS = 1e-6
ADAM_LR, ADAM_B1, ADAM_B2, ADAM_EPS, ADAM_WD, ADAM_STEP = 0.001, 0.9, 0.999, 1e-08, 0.01, 10

D_CONF = 512
K_A, K_B, K_F = 31, 3, 3
HALO_A, HALO_S = 32, 8
N_CHIPS = 4
LANES = 128
CONV_ROWS = 128
TS_MIX, TS_FFN, TS_NORM = 256, 128, 512
VMEM_LIMIT = 48 * 1024 * 1024

TWIN_WEIGHTS = ['mix_norm_g', 'w_in', 'b_in', 'conv_a_w', 'conv_a_b', 'ln_a_g', 'ln_a_b', 'conv_b_w', 'w_out',
                'ffn_norm_g', 'w_up', 'conv_f_w', 'w_down', 'final_norm_g']
BIG = ['w_in', 'w_out', 'w_up', 'w_down']
SMALL = [n for n in TWIN_WEIGHTS if n not in BIG]
CHANNEL_SHARDED = ['conv_a_w', 'conv_b_w', 'conv_f_w']


def _params(n_grid=1):
    return pltpu.CompilerParams(dimension_semantics=("arbitrary",) * n_grid, vmem_limit_bytes=VMEM_LIMIT)


def _pick(n, cands):
    for c in cands:
        if n % c == 0:
            return c
    raise ValueError(f"no tile for {n}")


_DN = {"nn": (((1,), (0,)), ((), ())), "nt": (((1,), (1,)), ((), ())), "tn": (((0,), (0,)), ((), ()))}


def _mm(a, b, *, mode, name, b_sharded=False, out_sharded=False, res=None, out_dtype=F32):
    dn = _DN[mode]
    if mode == "tn":
        k, m = a.shape
        n = b.shape[1]
        tm = _pick(m, (512, 256))
        tn = n // N_CHIPS if out_sharded else _pick(n, (1408, 1280, 1024, 512))
        a_spec = pl.BlockSpec((k, tm), lambda i, j: (0, i))
        b_spec = pl.BlockSpec((k, tn), lambda i, j: (0, j))
    else:
        m, k = a.shape
        tm = _pick(m, (512, 256, 128))
        a_spec = pl.BlockSpec((tm, k), lambda i, j: (i, 0))
        if mode == "nn":
            if b_sharded:
                n, tn = b.shape[2] * N_CHIPS, b.shape[2]
                b_spec = pl.BlockSpec((None, k, tn), lambda i, j: (j, 0, 0))
            else:
                n = b.shape[1]
                tn = _pick(n, (1408, 1280, 1024, 512))
                b_spec = pl.BlockSpec((k, tn), lambda i, j: (0, j))
        else:
            if b_sharded:
                n, ks = b.shape[1], b.shape[2]
                tn = _pick(n, (512, 256))
                b_spec = pl.BlockSpec((N_CHIPS, tn, ks), lambda i, j: (0, j, 0))
            else:
                n = b.shape[0]
                tn = _pick(n, (1408, 1280, 1024, 512))
                b_spec = pl.BlockSpec((tn, k), lambda i, j: (j, 0))
    if out_sharded:
        out_shape = jax.ShapeDtypeStruct((N_CHIPS, m, tn), out_dtype)
        o_spec = pl.BlockSpec((None, tm, tn), lambda i, j: (j, i, 0))
    else:
        out_shape = jax.ShapeDtypeStruct((m, n), out_dtype)
        o_spec = pl.BlockSpec((tm, tn), lambda i, j: (i, j))
    in_specs = [a_spec, b_spec]
    args = [a, b]
    if res is not None:
        in_specs.append(pl.BlockSpec((tm, tn), lambda i, j: (i, j)))
        args.append(res)

    def body(*refs):
        a_ref, b_ref, o_ref = refs[0], refs[1], refs[-1]
        if mode == "nt" and b_sharded:
            ks_ = b_ref.shape[2]
            acc = None
            for s in range(N_CHIPS):
                part = lax.dot_general(a_ref[:, s * ks_:(s + 1) * ks_], b_ref[s], dn, preferred_element_type=F32)
                acc = part if acc is None else acc + part
        else:
            acc = lax.dot_general(a_ref[...], b_ref[...], dn, preferred_element_type=F32)
        if res is not None:
            acc = acc + refs[2][...]
        o_ref[...] = acc.astype(o_ref.dtype)

    return pl.pallas_call(body, name=name, grid=(m // tm, n // tn), in_specs=in_specs, out_specs=o_spec,
                          out_shape=out_shape, compiler_params=_params(2))(*args)


def _conv_rows(win_ref, w_ref, offsets, rows, cols, emit, bias_ref=None):
    for c0 in range(0, cols, LANES):
        cs = slice(c0, c0 + LANES)
        wk = [w_ref[k:k + 1, cs] for k in range(len(offsets))]
        for r0 in range(0, rows, CONV_ROWS):
            acc = None
            for k, off in enumerate(offsets):
                term = wk[k] * win_ref[off + r0:off + r0 + CONV_ROWS, cs]
                acc = term if acc is None else acc + term
            if bias_ref is not None:
                acc = acc + bias_ref[:, cs]
            emit(r0, c0, acc)


def _conv_wgrad(g_ref, x_ref, offsets, rows, cols, dw_ref):
    for c0 in range(0, cols, LANES):
        cs = slice(c0, c0 + LANES)
        for k, off in enumerate(offsets):
            acc = None
            for r0 in range(0, rows, CONV_ROWS):
                term = g_ref[r0:r0 + CONV_ROWS, cs] * x_ref[off + r0:off + r0 + CONV_ROWS, cs]
                acc = term if acc is None else acc + term
            dw_ref[k:k + 1, cs] += jnp.sum(acc, axis=0, keepdims=True)


def _store_into(ref):
    def emit(r0, c0, v):
        ref[r0:r0 + CONV_ROWS, c0:c0 + LANES] = v
    return emit


def _causal_offsets(halo, k):
    return [halo - (k - 1) + j for j in range(k)]


def _anticausal_offsets(k):
    return [(k - 1) - j for j in range(k)]


def _row(v):
    return v.reshape(1, -1)


def _full(shape):
    return pl.BlockSpec(shape, lambda i: (0,) * len(shape))


def _sigmoid(v):
    return jax.nn.sigmoid(v)


def _dsilu(v, sg):
    return sg * (1.0 + v * (1.0 - sg))


def _norm_fwd(x, g, *, name):
    s, d = x.shape
    ts = _pick(s, (TS_NORM, 256, 128))

    def body(x_ref, g_ref, h_ref):
        xv = x_ref[...]
        r = lax.rsqrt(jnp.mean(xv * xv, axis=-1, keepdims=True) + EPS)
        h_ref[...] = (xv * r * g_ref[...]).astype(BF16)

    return pl.pallas_call(body, name=name, grid=(s // ts,),
                          in_specs=[pl.BlockSpec((ts, d), lambda i: (i, 0)), _full((1, d))],
                          out_specs=pl.BlockSpec((ts, d), lambda i: (i, 0)),
                          out_shape=jax.ShapeDtypeStruct((s, d), BF16), compiler_params=_params())(x, _row(g))


def _mix_pw_fwd(u, b_in, wa, ba, lg, lb, wb, *, name):
    s = u.shape[0]
    c = D_CONF
    ts = _pick(s, (TS_MIX, 128))
    offs_a, offs_b = _causal_offsets(HALO_A, K_A), _causal_offsets(HALO_S, K_B)

    def body(u_ref, bin_ref, wa_ref, ba_ref, lg_ref, lb_ref, wb_ref, y_ref, p_ref, q_ref, e_ref, pwin, ewin, fbuf):
        i = pl.program_id(0)

        @pl.when(i == 0)
        def _():
            pwin[0:HALO_A, :] = jnp.zeros((HALO_A, c), F32)
            ewin[0:HALO_S, :] = jnp.zeros((HALO_S, c), F32)

        @pl.when(i > 0)
        def _():
            pwin[0:HALO_A, :] = pwin[ts:ts + HALO_A, :]
            ewin[0:HALO_S, :] = ewin[ts:ts + HALO_S, :]

        a_val = u_ref[:, 0:c] + bin_ref[:, 0:c]
        a_gate = u_ref[:, c:2 * c] + bin_ref[:, c:2 * c]
        p = a_val * _sigmoid(a_gate)
        pwin[HALO_A:HALO_A + ts, :] = p
        p_ref[...] = p
        e = (u_ref[:, 3 * c:4 * c] + bin_ref[:, 3 * c:4 * c]) * (u_ref[:, 4 * c:5 * c] + bin_ref[:, 4 * c:5 * c])
        ewin[HALO_S:HALO_S + ts, :] = e
        e_ref[...] = e
        _conv_rows(pwin, wa_ref, offs_a, ts, c, _store_into(q_ref), bias_ref=ba_ref)
        q = q_ref[...]
        mu = jnp.mean(q, axis=-1, keepdims=True)
        xc = q - mu
        var = jnp.mean(xc * xc, axis=-1, keepdims=True)
        ln = xc * lax.rsqrt(var + EPS) * lg_ref[...] + lb_ref[...]
        y_ref[:, 0:c] = (ln * _sigmoid(ln)).astype(BF16)
        _conv_rows(ewin, wb_ref, offs_b, ts, c, _store_into(fbuf))
        g_b = u_ref[:, 2 * c:3 * c] + bin_ref[:, 2 * c:3 * c]
        y_ref[:, c:2 * c] = (g_b * fbuf[...]).astype(BF16)

    tile = lambda w: pl.BlockSpec((ts, w), lambda i: (i, 0))
    return pl.pallas_call(
        body, name=name, grid=(s // ts,),
        in_specs=[tile(5 * c), _full((1, 5 * c)), _full((K_A, c)), _full((1, c)), _full((1, c)), _full((1, c)),
                  _full((K_B, c))],
        out_specs=[tile(2 * c), tile(c), tile(c), tile(c)],
        out_shape=[jax.ShapeDtypeStruct((s, 2 * c), BF16)] + [jax.ShapeDtypeStruct((s, c), F32)] * 3,
        scratch_shapes=[pltpu.VMEM((ts + HALO_A, c), F32), pltpu.VMEM((ts + HALO_S, c), F32), pltpu.VMEM((ts, c), F32)],
        compiler_params=_params())(u, _row(b_in), wa, _row(ba), _row(lg), _row(lb), wb)


def _ffn_pw_fwd(u2, wf, *, name):
    s, f2 = u2.shape
    f = f2 // 2
    ts = _pick(s, (TS_FFN,))
    offs = _causal_offsets(HALO_S, K_F)

    def body(u_ref, wf_ref, z_ref, uwin, cbuf):
        i = pl.program_id(0)

        @pl.when(i == 0)
        def _():
            uwin[0:HALO_S, :] = jnp.zeros((HALO_S, f2), F32)

        @pl.when(i > 0)
        def _():
            uwin[0:HALO_S, :] = uwin[ts:ts + HALO_S, :]

        uwin[HALO_S:HALO_S + ts, :] = u_ref[...]
        _conv_rows(uwin, wf_ref, offs, ts, f2, _store_into(cbuf))
        gate = cbuf[:, 0:f]
        z_ref[...] = (gate * _sigmoid(gate) * cbuf[:, f:f2]).astype(BF16)

    return pl.pallas_call(
        body, name=name, grid=(s // ts,),
        in_specs=[pl.BlockSpec((ts, f2), lambda i: (i, 0)), _full((K_F, f2))],
        out_specs=pl.BlockSpec((ts, f), lambda i: (i, 0)),
        out_shape=jax.ShapeDtypeStruct((s, f), BF16),
        scratch_shapes=[pltpu.VMEM((ts + HALO_S, f2), F32), pltpu.VMEM((ts, f2), F32)],
        compiler_params=_params())(u2, wf)


def _loss_bwd(x, tgt, g, *, name):
    s, d = x.shape
    ts = _pick(s, (TS_NORM, 256, 128))

    def body(x_ref, t_ref, g_ref, loss_ref, dx_ref, dxb_ref, dg_ref):
        @pl.when(pl.program_id(0) == 0)
        def _():
            loss_ref[...] = jnp.zeros_like(loss_ref)
            dg_ref[...] = jnp.zeros_like(dg_ref)

        xv = x_ref[...]
        r = lax.rsqrt(jnp.mean(xv * xv, axis=-1, keepdims=True) + EPS)
        xh = xv * r
        err = xh * g_ref[...] - t_ref[...]
        loss_ref[...] += 0.5 * jnp.sum(jnp.mean(err * err, axis=-1, keepdims=True))
        dy = err * (1.0 / d)
        dg_ref[...] += jnp.sum(dy * xh, axis=0, keepdims=True)
        dxh = dy * g_ref[...]
        dx = r * (dxh - xh * jnp.mean(dxh * xh, axis=-1, keepdims=True))
        dx_ref[...] = dx
        dxb_ref[...] = dx.astype(BF16)

    tile = pl.BlockSpec((ts, d), lambda i: (i, 0))
    return pl.pallas_call(
        body, name=name, grid=(s // ts,), in_specs=[tile, tile, _full((1, d))],
        out_specs=[_full((8, LANES)), tile, tile, _full((1, d))],
        out_shape=[jax.ShapeDtypeStruct((8, LANES), F32), jax.ShapeDtypeStruct((s, d), F32),
                   jax.ShapeDtypeStruct((s, d), BF16), jax.ShapeDtypeStruct((1, d), F32)],
        compiler_params=_params())(x, tgt, _row(g))


def _norm_bwd(x, dh, dx_in, g, *, name):
    s, d = x.shape
    ts = _pick(s, (TS_NORM, 256, 128))

    def body(x_ref, dh_ref, dxin_ref, g_ref, dx_ref, dxb_ref, dg_ref):
        @pl.when(pl.program_id(0) == 0)
        def _():
            dg_ref[...] = jnp.zeros_like(dg_ref)

        xv = x_ref[...]
        r = lax.rsqrt(jnp.mean(xv * xv, axis=-1, keepdims=True) + EPS)
        xh = xv * r
        dh_v = dh_ref[...]
        dg_ref[...] += jnp.sum(dh_v * xh, axis=0, keepdims=True)
        dxh = dh_v * g_ref[...]
        dx = dxin_ref[...] + r * (dxh - xh * jnp.mean(dxh * xh, axis=-1, keepdims=True))
        dx_ref[...] = dx
        dxb_ref[...] = dx.astype(BF16)

    tile = pl.BlockSpec((ts, d), lambda i: (i, 0))
    return pl.pallas_call(
        body, name=name, grid=(s // ts,), in_specs=[tile, tile, tile, _full((1, d))],
        out_specs=[tile, tile, _full((1, d))],
        out_shape=[jax.ShapeDtypeStruct((s, d), F32), jax.ShapeDtypeStruct((s, d), BF16),
                   jax.ShapeDtypeStruct((1, d), F32)],
        compiler_params=_params())(x, dh, dx_in, _row(g))


def _ffn_pw_bwd(u2, dz, wf, *, name):
    s, f2 = u2.shape
    f = f2 // 2
    ts = _pick(s, (TS_FFN,))
    nt = s // ts
    hb = ts // HALO_S
    offs_c, offs_t = _causal_offsets(HALO_S, K_F), _anticausal_offsets(K_F)

    def body(u_ref, uh_ref, dz_ref, wf_ref, du_ref, dwf_ref, uwin, cbuf, dcwin):
        i = pl.program_id(0)

        @pl.when(i == 0)
        def _():
            dwf_ref[...] = jnp.zeros_like(dwf_ref)
            dcwin[ts:ts + HALO_S, :] = jnp.zeros((HALO_S, f2), F32)

        @pl.when(i > 0)
        def _():
            dcwin[ts:ts + HALO_S, :] = dcwin[0:HALO_S, :]

        @pl.when(i == nt - 1)
        def _():
            uwin[0:HALO_S, :] = jnp.zeros((HALO_S, f2), F32)

        @pl.when(i < nt - 1)
        def _():
            uwin[0:HALO_S, :] = uh_ref[...]

        uwin[HALO_S:HALO_S + ts, :] = u_ref[...]
        _conv_rows(uwin, wf_ref, offs_c, ts, f2, _store_into(cbuf))
        gate = cbuf[:, 0:f]
        sg = _sigmoid(gate)
        dz_v = dz_ref[...]
        dcwin[0:ts, 0:f] = dz_v * cbuf[:, f:f2] * _dsilu(gate, sg)
        dcwin[0:ts, f:f2] = dz_v * gate * sg

        def emit(r0, c0, v):
            du_ref[r0:r0 + CONV_ROWS, c0:c0 + LANES] = v.astype(BF16)
        _conv_rows(dcwin, wf_ref, offs_t, ts, f2, emit)
        _conv_wgrad(dcwin, uwin, offs_c, ts, f2, dwf_ref)

    rev = lambda i: (nt - 1 - i, 0)
    return pl.pallas_call(
        body, name=name, grid=(nt,),
        in_specs=[pl.BlockSpec((ts, f2), rev),
                  pl.BlockSpec((HALO_S, f2), lambda i: (jnp.maximum((nt - 1 - i) * hb - 1, 0), 0)),
                  pl.BlockSpec((ts, f), rev), _full((K_F, f2))],
        out_specs=[pl.BlockSpec((ts, f2), rev), _full((K_F, f2))],
        out_shape=[jax.ShapeDtypeStruct((s, f2), BF16), jax.ShapeDtypeStruct((K_F, f2), F32)],
        scratch_shapes=[pltpu.VMEM((ts + HALO_S, f2), F32), pltpu.VMEM((ts, f2), F32),
                        pltpu.VMEM((ts + HALO_S, f2), F32)],
        compiler_params=_params())(u2, u2, dz, wf)


def _mix_pw_bwd(dy, u, p, q, e, b_in, wa, lg, lb, wb, *, name):
    s = u.shape[0]
    c = D_CONF
    ts = _pick(s, (TS_MIX, 128))
    nt = s // ts
    offs_ac, offs_at = _causal_offsets(HALO_A, K_A), _anticausal_offsets(K_A)
    offs_bc, offs_bt = _causal_offsets(HALO_S, K_B), _anticausal_offsets(K_B)

    def body(dy_ref, u_ref, p_ref, ph_ref, q_ref, e_ref, eh_ref, bin_ref, wa_ref, lg_ref, lb_ref, wb_ref,
             du_ref, dbin_ref, dwa_ref, dba_ref, dlg_ref, dlb_ref, dwb_ref, pwin, ewin, dqwin, dfwin, buf):
        i = pl.program_id(0)

        @pl.when(i == 0)
        def _():
            for r in (dbin_ref, dwa_ref, dba_ref, dlg_ref, dlb_ref, dwb_ref):
                r[...] = jnp.zeros_like(r)
            dqwin[ts:ts + HALO_A, :] = jnp.zeros((HALO_A, c), F32)
            dfwin[ts:ts + HALO_S, :] = jnp.zeros((HALO_S, c), F32)

        @pl.when(i > 0)
        def _():
            dqwin[ts:ts + HALO_A, :] = dqwin[0:HALO_A, :]
            dfwin[ts:ts + HALO_S, :] = dfwin[0:HALO_S, :]

        @pl.when(i == nt - 1)
        def _():
            pwin[0:HALO_A, :] = jnp.zeros((HALO_A, c), F32)
            ewin[0:HALO_S, :] = jnp.zeros((HALO_S, c), F32)

        @pl.when(i < nt - 1)
        def _():
            pwin[0:HALO_A, :] = ph_ref[...]
            ewin[0:HALO_S, :] = eh_ref[...]

        pwin[HALO_A:HALO_A + ts, :] = p_ref[...]
        ewin[HALO_S:HALO_S + ts, :] = e_ref[...]

        def colsum(v):
            return jnp.sum(v, axis=0, keepdims=True)

        q_v = q_ref[...]
        mu = jnp.mean(q_v, axis=-1, keepdims=True)
        xc = q_v - mu
        rstd = lax.rsqrt(jnp.mean(xc * xc, axis=-1, keepdims=True) + EPS)
        nrm = xc * rstd
        ln = nrm * lg_ref[...] + lb_ref[...]
        dln = dy_ref[:, 0:c] * _dsilu(ln, _sigmoid(ln))
        dlg_ref[...] += colsum(dln * nrm)
        dlb_ref[...] += colsum(dln)
        dn = dln * lg_ref[...]
        dq = rstd * (dn - jnp.mean(dn, axis=-1, keepdims=True) - nrm * jnp.mean(dn * nrm, axis=-1, keepdims=True))
        dba_ref[...] += colsum(dq)
        dqwin[0:ts, :] = dq
        _conv_rows(dqwin, wa_ref, offs_at, ts, c, _store_into(buf))
        _conv_wgrad(dqwin, pwin, offs_ac, ts, c, dwa_ref)
        dp = buf[...]
        a_val = u_ref[:, 0:c] + bin_ref[:, 0:c]
        sg = _sigmoid(u_ref[:, c:2 * c] + bin_ref[:, c:2 * c])
        d_aval = dp * sg
        d_agate = dp * a_val * sg * (1.0 - sg)
        du_ref[:, 0:c] = d_aval.astype(BF16)
        du_ref[:, c:2 * c] = d_agate.astype(BF16)
        dbin_ref[:, 0:c] += colsum(d_aval)
        dbin_ref[:, c:2 * c] += colsum(d_agate)

        ds = dy_ref[:, c:2 * c]
        _conv_rows(ewin, wb_ref, offs_bc, ts, c, _store_into(buf))
        d_gb = ds * buf[...]
        dfwin[0:ts, :] = ds * (u_ref[:, 2 * c:3 * c] + bin_ref[:, 2 * c:3 * c])
        _conv_rows(dfwin, wb_ref, offs_bt, ts, c, _store_into(buf))
        _conv_wgrad(dfwin, ewin, offs_bc, ts, c, dwb_ref)
        de = buf[...]
        d_gc = de * (u_ref[:, 4 * c:5 * c] + bin_ref[:, 4 * c:5 * c])
        d_v = de * (u_ref[:, 3 * c:4 * c] + bin_ref[:, 3 * c:4 * c])
        du_ref[:, 2 * c:3 * c] = d_gb.astype(BF16)
        du_ref[:, 3 * c:4 * c] = d_gc.astype(BF16)
        du_ref[:, 4 * c:5 * c] = d_v.astype(BF16)
        dbin_ref[:, 2 * c:3 * c] += colsum(d_gb)
        dbin_ref[:, 3 * c:4 * c] += colsum(d_gc)
        dbin_ref[:, 4 * c:5 * c] += colsum(d_v)

    rev = lambda i: (nt - 1 - i, 0)
    tile = lambda w: pl.BlockSpec((ts, w), rev)
    halo = lambda h: pl.BlockSpec((h, c), lambda i: (jnp.maximum((nt - 1 - i) * (ts // h) - 1, 0), 0))
    small = [(1, 5 * c), (K_A, c), (1, c), (1, c), (1, c), (K_B, c)]
    return pl.pallas_call(
        body, name=name, grid=(nt,),
        in_specs=[tile(2 * c), tile(5 * c), tile(c), halo(HALO_A), tile(c), tile(c), halo(HALO_S),
                  _full((1, 5 * c)), _full((K_A, c)), _full((1, c)), _full((1, c)), _full((K_B, c))],
        out_specs=[tile(5 * c)] + [_full(sh) for sh in small],
        out_shape=[jax.ShapeDtypeStruct((s, 5 * c), BF16)] + [jax.ShapeDtypeStruct(sh, F32) for sh in small],
        scratch_shapes=[pltpu.VMEM((ts + HALO_A, c), F32), pltpu.VMEM((ts + HALO_S, c), F32),
                        pltpu.VMEM((ts + HALO_A, c), F32), pltpu.VMEM((ts + HALO_S, c), F32),
                        pltpu.VMEM((ts, c), F32)],
        compiler_params=_params())(dy, u, p, p, q, e, e, _row(b_in), wa, _row(lg), _row(lb), wb)


def _adamw(g, w, m, v, *, name):
    r, c = g.shape
    tr = _pick(r, (256, 128, 64, 8)) if r > 256 else r

    def body(g_ref, w_ref, m_ref, v_ref, d_ref, nm_ref, nv_ref):
        gv = g_ref[...]
        nm = ADAM_B1 * m_ref[...] + (1.0 - ADAM_B1) * gv
        nv = ADAM_B2 * v_ref[...] + (1.0 - ADAM_B2) * (gv * gv)
        m_hat = nm / (1.0 - ADAM_B1 ** ADAM_STEP)
        v_hat = nv / (1.0 - ADAM_B2 ** ADAM_STEP)
        d_ref[...] = -ADAM_LR * (m_hat / (jnp.sqrt(v_hat) + ADAM_EPS) + ADAM_WD * w_ref[...])
        nm_ref[...] = nm
        nv_ref[...] = nv

    tile = pl.BlockSpec((tr, c), lambda i: (i, 0))
    return pl.pallas_call(body, name=name, grid=(r // tr,), in_specs=[tile] * 4, out_specs=[tile] * 3,
                          out_shape=[jax.ShapeDtypeStruct((r, c), F32)] * 3, compiler_params=_params())(g, w, m, v)


def _pair_sum(g, recv, core, *, name):
    _, _, r, c = g.shape

    def body(core_ref, g_ref, r_ref, o_ref):
        o_ref[...] = (g_ref[...] + r_ref[...]).astype(BF16)

    grid_spec = pltpu.PrefetchScalarGridSpec(
        num_scalar_prefetch=1, grid=(N_CHIPS,),
        in_specs=[pl.BlockSpec((None, None, r, c), lambda i, core_ref: (i, core_ref[0], 0, 0)),
                  pl.BlockSpec((None, r, c), lambda i, core_ref: (i, 0, 0))],
        out_specs=pl.BlockSpec((None, r, c), lambda i, core_ref: (i, 0, 0)))
    return pl.pallas_call(body, name=name, grid_spec=grid_spec,
                          out_shape=jax.ShapeDtypeStruct((N_CHIPS, r, c), BF16), compiler_params=_params())(core, g, recv)


def _chip_sum(parts, *, name):
    _, r, c = parts.shape
    tr = r // 2

    def body(p_ref, o_ref):
        acc = p_ref[0].astype(F32)
        for s in range(1, N_CHIPS):
            acc = acc + p_ref[s].astype(F32)
        o_ref[...] = acc

    return pl.pallas_call(body, name=name, grid=(2,), in_specs=[pl.BlockSpec((N_CHIPS, tr, c), lambda i: (0, i, 0))],
                          out_specs=pl.BlockSpec((tr, c), lambda i: (i, 0)),
                          out_shape=jax.ShapeDtypeStruct((r, c), F32), compiler_params=_params())(parts)


def _place():
    x, y, c = lax.axis_index("x"), lax.axis_index("y"), lax.axis_index("c")
    chips = [(1 - x, y), (x, 1 - y), (1 - x, 1 - y)]
    return x, y, c, 2 * x + y, chips


def _remote(src, dst, send_sems, recv_sems, k, dev):
    return pltpu.make_async_remote_copy(src_ref=src, dst_ref=dst, send_sem=send_sems.at[k], recv_sem=recv_sems.at[k],
                                        device_id=dev, device_id_type=MESH)


def _allgather_shards(shards, *, name):
    n = len(shards)
    idx = [(a, l) for a, sh in enumerate(shards) for l in range(sh.shape[0])]
    nq = len(idx)
    out_shape = [jax.ShapeDtypeStruct((N_CHIPS,) + shards[a].shape[1:], shards[a].dtype) for a, _ in idx]

    def body(*refs):
        ins, outs = refs[:n], refs[n:n + nq]
        send_sems, recv_sems, loc_sems = refs[n + nq:]
        x, y, c, j, chips = _place()
        sib = (x, y, 1 - c)
        local, sends = [], []
        for qi, (a, l) in enumerate(idx):
            cp = pltpu.make_async_copy(ins[a].at[l], outs[qi].at[j], loc_sems.at[qi])
            cp.start()
            local.append(cp)
        for qi, (a, l) in enumerate(idx):
            for k, (cx, cy) in enumerate(chips):
                cp = _remote(ins[a].at[l, c], outs[qi].at[j, c], send_sems, recv_sems, 6 * qi + k, (cx, cy, c))
                cp.start()
                sends.append(cp)
        for qi in range(nq):
            for k, (cx, cy) in enumerate(chips):
                blk = outs[qi].at[2 * cx + cy, c]
                _remote(blk, blk, send_sems, recv_sems, 6 * qi + k, (cx, cy, c)).wait_recv()
                cp = _remote(blk, blk, send_sems, recv_sems, 6 * qi + 3 + k, sib)
                cp.start()
                sends.append(cp)
        for qi in range(nq):
            for k, (cx, cy) in enumerate(chips):
                blk = outs[qi].at[2 * cx + cy, 1 - c]
                _remote(blk, blk, send_sems, recv_sems, 6 * qi + 3 + k, sib).wait_recv()
        for cp in sends:
            cp.wait_send()
        for cp in local:
            cp.wait()

    return pl.pallas_call(
        body, name=name, in_specs=[ANY] * n, out_specs=[ANY] * nq, out_shape=out_shape,
        scratch_shapes=[pltpu.SemaphoreType.DMA((6 * nq,)), pltpu.SemaphoreType.DMA((6 * nq,)),
                        pltpu.SemaphoreType.DMA((nq,))])(*shards)


def _sibling_halves(grads, *, name):
    n = len(grads)
    out_shape = [jax.ShapeDtypeStruct((N_CHIPS,) + g.shape[2:], g.dtype) for g in grads]

    def body(*refs):
        ins, outs = refs[:n], refs[n:2 * n]
        send_sems, recv_sems = refs[2 * n:]
        x, y, c, _, _ = _place()
        cps = [_remote(ins[a].at[:, 1 - c], outs[a], send_sems, recv_sems, a, (x, y, 1 - c)) for a in range(n)]
        for cp in cps:
            cp.start()
        for cp in cps:
            cp.wait()

    return pl.pallas_call(body, name=name, in_specs=[ANY] * n, out_specs=[ANY] * n, out_shape=out_shape,
                          scratch_shapes=[pltpu.SemaphoreType.DMA((n,)), pltpu.SemaphoreType.DMA((n,))])(*grads)


def _scatter_to_chips(sums, *, name):
    n = len(sums)
    out_shape = [jax.ShapeDtypeStruct(s.shape, s.dtype) for s in sums]

    def body(*refs):
        ins, outs = refs[:n], refs[n:2 * n]
        send_sems, recv_sems, loc_sems = refs[2 * n:]
        x, y, c, j, chips = _place()
        local = [pltpu.make_async_copy(ins[a].at[j], outs[a].at[j], loc_sems.at[a]) for a in range(n)]
        for cp in local:
            cp.start()
        sends = [_remote(ins[a].at[2 * cx + cy], outs[a].at[j], send_sems, recv_sems, 3 * a + k, (cx, cy, c))
                 for a in range(n) for k, (cx, cy) in enumerate(chips)]
        for cp in sends:
            cp.start()
        for a in range(n):
            for k, (cx, cy) in enumerate(chips):
                blk = outs[a].at[2 * cx + cy]
                _remote(blk, blk, send_sems, recv_sems, 3 * a + k, (cx, cy, c)).wait_recv()
        for cp in sends:
            cp.wait_send()
        for cp in local:
            cp.wait()

    return pl.pallas_call(
        body, name=name, in_specs=[ANY] * n, out_specs=[ANY] * n, out_shape=out_shape,
        scratch_shapes=[pltpu.SemaphoreType.DMA((3 * n,)), pltpu.SemaphoreType.DMA((3 * n,)),
                        pltpu.SemaphoreType.DMA((n,))])(*sums)


def _share_halves(halves, layers, *, name):
    flat = [h for hs in halves for h in hs]
    idx = [(a, l) for a, hs in enumerate(halves) for l in range(len(hs))]
    n, na = len(flat), len(halves)
    out_shape = [jax.ShapeDtypeStruct((layers, 2) + hs[0].shape, hs[0].dtype) for hs in halves]

    def body(*refs):
        ins, outs = refs[:n], refs[n:n + na]
        send_sems, recv_sems, loc_sems = refs[n + na:]
        x, y, c, _, _ = _place()
        local = [pltpu.make_async_copy(ins[qi], outs[a].at[l, c], loc_sems.at[qi]) for qi, (a, l) in enumerate(idx)]
        sends = [_remote(ins[qi], outs[a].at[l, c], send_sems, recv_sems, qi, (x, y, 1 - c)) for qi, (a, l) in enumerate(idx)]
        for cp in local + sends:
            cp.start()
        for qi, (a, l) in enumerate(idx):
            blk = outs[a].at[l, 1 - c]
            _remote(blk, blk, send_sems, recv_sems, qi, (x, y, 1 - c)).wait_recv()
        for cp in sends:
            cp.wait_send()
        for cp in local:
            cp.wait()

    return pl.pallas_call(
        body, name=name, in_specs=[ANY] * n, out_specs=[ANY] * na, out_shape=out_shape,
        scratch_shapes=[pltpu.SemaphoreType.DMA((n,)), pltpu.SemaphoreType.DMA((n,)),
                        pltpu.SemaphoreType.DMA((n,))])(*flat)


def _allreduce_small(v, *, name):
    r, c = v.shape
    n_dev = 8

    def body(v_ref, o_ref, buf, send_sems, recv_sems):
        x, y, cc = lax.axis_index("x"), lax.axis_index("y"), lax.axis_index("c")
        me = 4 * x + 2 * y + cc
        cps = []
        for d in range(1, n_dev):
            peer = (1 - x if d & 4 else x, 1 - y if d & 2 else y, 1 - cc if d & 1 else cc)
            cps.append(_remote(v_ref, buf.at[me], send_sems, recv_sems, d - 1, peer))
        for cp in cps:
            cp.start()
        buf[me] = v_ref[...]
        for cp in cps:
            cp.wait()
        acc = buf[0]
        for d in range(1, n_dev):
            acc = acc + buf[d]
        o_ref[...] = acc

    return pl.pallas_call(
        body, name=name, in_specs=[VMEM], out_specs=VMEM, out_shape=jax.ShapeDtypeStruct((r, c), F32),
        scratch_shapes=[pltpu.VMEM((n_dev, r, c), F32), pltpu.SemaphoreType.DMA((n_dev - 1,)),
                        pltpu.SemaphoreType.DMA((n_dev - 1,))])(v)


def _pack(arrays, row_multiple=8):
    flat = jnp.concatenate([a.reshape(-1) for a in arrays])
    rows = -(-flat.shape[0] // LANES)
    rows = -(-rows // row_multiple) * row_multiple
    return jnp.pad(flat, (0, rows * LANES - flat.shape[0])).reshape(rows, LANES)


def _unpack(slab, shapes):
    flat = slab.reshape(-1)
    out, pos = [], 0
    for sh in shapes:
        size = 1
        for dim in sh:
            size *= dim
        out.append(flat[pos:pos + size].reshape(sh))
        pos += size
    return out


def _halves(w):
    l, r, c = w.shape
    return w.reshape(l, 2, r // 2, c)


def kernel(x, mix_norm_g, w_in, b_in, conv_a_w, conv_a_b, ln_a_g, ln_a_b, conv_b_w, w_out, ffn_norm_g, w_up, conv_f_w, w_down, final_norm_g, loss_target, m_mix_norm_g, m_w_in, m_b_in, m_conv_a_w, m_conv_a_b, m_ln_a_g, m_ln_a_b, m_conv_b_w, m_w_out, m_ffn_norm_g, m_w_up, m_conv_f_w, m_w_down, m_final_norm_g, v_mix_norm_g, v_w_in, v_b_in, v_conv_a_w, v_conv_a_b, v_ln_a_g, v_ln_a_b, v_conv_b_w, v_w_out, v_ffn_norm_g, v_w_up, v_conv_f_w, v_w_down, v_final_norm_g):
    w = dict(mix_norm_g=mix_norm_g, w_in=w_in, b_in=b_in, conv_a_w=conv_a_w, conv_a_b=conv_a_b, ln_a_g=ln_a_g,
             ln_a_b=ln_a_b, conv_b_w=conv_b_w, w_out=w_out, ffn_norm_g=ffn_norm_g, w_up=w_up, conv_f_w=conv_f_w,
             w_down=w_down, final_norm_g=final_norm_g)
    m = dict(mix_norm_g=m_mix_norm_g, w_in=m_w_in, b_in=m_b_in, conv_a_w=m_conv_a_w, conv_a_b=m_conv_a_b,
             ln_a_g=m_ln_a_g, ln_a_b=m_ln_a_b, conv_b_w=m_conv_b_w, w_out=m_w_out, ffn_norm_g=m_ffn_norm_g,
             w_up=m_w_up, conv_f_w=m_conv_f_w, w_down=m_w_down, final_norm_g=m_final_norm_g)
    v = dict(mix_norm_g=v_mix_norm_g, w_in=v_w_in, b_in=v_b_in, conv_a_w=v_conv_a_w, conv_a_b=v_conv_a_b,
             ln_a_g=v_ln_a_g, ln_a_b=v_ln_a_b, conv_b_w=v_conv_b_w, w_out=v_w_out, ffn_norm_g=v_ffn_norm_g,
             w_up=v_w_up, conv_f_w=v_conv_f_w, w_down=v_w_down, final_norm_g=v_final_norm_g)
    depth = w_in.shape[0]
    xs = x[0]
    tgt = loss_target[0]
    chip = 2 * lax.axis_index("x") + lax.axis_index("y")
    core = lax.axis_index("c").astype(jnp.int32).reshape(1)

    conv_shapes = [w[nm].shape for nm in CHANNEL_SHARDED]
    conv_slab = _pack([w[nm] for nm in CHANNEL_SHARDED], row_multiple=16)
    gathered = _allgather_shards(
        [_halves(w[nm].astype(BF16)) for nm in BIG] + [_halves(conv_slab[None])], name="gather_weights")
    full = {nm: [gathered[depth * a + l] for l in range(depth)] for a, nm in enumerate(BIG)}
    conv_full = gathered[depth * len(BIG)].reshape(N_CHIPS, -1, LANES)
    taps = {}
    for nm, parts in zip(CHANNEL_SHARDED, zip(*[_unpack(conv_full[jj], conv_shapes) for jj in range(N_CHIPS)])):
        taps[nm] = jnp.concatenate(parts, axis=-1)

    def wmat(nm, l):
        g = full[nm][l]
        if nm in ("w_in", "w_up"):
            return g.reshape(N_CHIPS, 2 * g.shape[2], g.shape[3])
        return g.reshape(N_CHIPS * 2 * g.shape[2], g.shape[3])

    saved = []
    cur = xs
    for l in range(depth):
        h1 = _norm_fwd(cur, w["mix_norm_g"][l], name=f"norm_mix_{l}")
        u = _mm(h1, wmat("w_in", l), mode="nn", b_sharded=True, name=f"mm_in_{l}")
        y, p, q, e = _mix_pw_fwd(u, w["b_in"][l], taps["conv_a_w"][l], w["conv_a_b"][l], w["ln_a_g"][l],
                                 w["ln_a_b"][l], taps["conv_b_w"][l], name=f"mix_fwd_{l}")
        x1 = _mm(y, wmat("w_out", l), mode="nn", res=cur, name=f"mm_out_{l}")
        h2 = _norm_fwd(x1, w["ffn_norm_g"][l], name=f"norm_ffn_{l}")
        u2 = _mm(h2, wmat("w_up", l), mode="nn", b_sharded=True, name=f"mm_up_{l}")
        z = _ffn_pw_fwd(u2, taps["conv_f_w"][l], name=f"ffn_fwd_{l}")
        x2 = _mm(z, wmat("w_down", l), mode="nn", res=x1, name=f"mm_down_{l}")
        saved.append((cur, h1, u, y, p, q, e, x1, h2, u2, z))
        cur = x2

    loss_blk, dx, dxb, dgf = _loss_bwd(cur, tgt, w["final_norm_g"], name="loss_bwd")
    loss = lax.psum(loss_blk[0, 0], ("x", "y", "c"))

    big_grads = {nm: [None] * depth for nm in BIG}
    sg = {nm: [None] * depth for nm in SMALL if nm != "final_norm_g"}
    for l in reversed(range(depth)):
        x0, h1, u, y, p, q, e, x1, h2, u2, z = saved[l]
        dz = _mm(dxb, wmat("w_down", l), mode="nt", name=f"mm_ddown_{l}")
        big_grads["w_down"][l] = _mm(z, dxb, mode="tn", name=f"mm_gdown_{l}")
        du2, dwf = _ffn_pw_bwd(u2, dz, taps["conv_f_w"][l], name=f"ffn_bwd_{l}")
        dh2 = _mm(du2, wmat("w_up", l), mode="nt", b_sharded=True, name=f"mm_dup_{l}")
        big_grads["w_up"][l] = _mm(h2, du2, mode="tn", out_sharded=True, name=f"mm_gup_{l}")
        dx1, dx1b, dg2 = _norm_bwd(x1, dh2, dx, w["ffn_norm_g"][l], name=f"norm_ffn_bwd_{l}")
        dy = _mm(dx1b, wmat("w_out", l), mode="nt", name=f"mm_dout_{l}")
        big_grads["w_out"][l] = _mm(y, dx1b, mode="tn", name=f"mm_gout_{l}")
        du, dbin, dwa, dba, dlg, dlb, dwb = _mix_pw_bwd(
            dy, u, p, q, e, w["b_in"][l], taps["conv_a_w"][l], w["ln_a_g"][l], w["ln_a_b"][l],
            taps["conv_b_w"][l], name=f"mix_bwd_{l}")
        dh1 = _mm(du, wmat("w_in", l), mode="nt", b_sharded=True, name=f"mm_din_{l}")
        big_grads["w_in"][l] = _mm(h1, du, mode="tn", out_sharded=True, name=f"mm_gin_{l}")
        dx, dxb, dg1 = _norm_bwd(x0, dh1, dx1, w["mix_norm_g"][l], name=f"norm_mix_bwd_{l}")
        for nm, g in (("mix_norm_g", dg1), ("b_in", dbin), ("conv_a_w", dwa), ("conv_a_b", dba), ("ln_a_g", dlg),
                      ("ln_a_b", dlb), ("conv_b_w", dwb), ("ffn_norm_g", dg2), ("conv_f_w", dwf)):
            sg[nm][l] = g.reshape(g.shape[-1]) if g.shape[0] == 1 else g
    grad_x = dx[None]

    gl = []
    for nm in BIG:
        for l in range(depth):
            g = big_grads[nm][l]
            if g.ndim == 2:
                g = g.reshape(N_CHIPS, 2, g.shape[0] // (2 * N_CHIPS), g.shape[1])
            else:
                g = g.reshape(N_CHIPS, 2, g.shape[1] // 2, g.shape[2])
            gl.append(g)
    from_sibling = _sibling_halves(gl, name="reduce_sibling")
    chip_sums = [_pair_sum(g, r, core, name=f"pair_sum_{i}") for i, (g, r) in enumerate(zip(gl, from_sibling))]
    from_chips = _scatter_to_chips(chip_sums, name="reduce_chips")
    mine = [_chip_sum(parts, name=f"chip_sum_{i}") for i, parts in enumerate(from_chips)]
    shared = _share_halves([mine[depth * a:depth * (a + 1)] for a in range(len(BIG))], depth, name="share_halves")
    grads = {}
    for nm, g in zip(BIG, shared):
        grads[nm] = g.reshape(w[nm].shape)

    small_full_shapes = []
    small_parts = []
    for nm in SMALL:
        if nm == "final_norm_g":
            g = dgf.reshape(-1)
        else:
            g = jnp.stack(sg[nm])
        small_parts.append(g)
        small_full_shapes.append(g.shape)
    summed = _unpack(_allreduce_small(_pack(small_parts), name="reduce_small"), small_full_shapes)
    for nm, g in zip(SMALL, summed):
        if nm in CHANNEL_SHARDED:
            width = w[nm].shape[-1]
            g = lax.dynamic_slice_in_dim(g, chip * width, width, axis=2)
        grads[nm] = g

    delta, new_m, new_v = {}, {}, {}
    for nm in BIG:
        sh = w[nm].shape
        two_d = lambda a: a.reshape(sh[0] * sh[1], sh[2])
        d_, m_, v_ = _adamw(two_d(grads[nm]), two_d(w[nm]), two_d(m[nm]), two_d(v[nm]), name=f"adamw_{nm}")
        delta[nm], new_m[nm], new_v[nm] = d_.reshape(sh), m_.reshape(sh), v_.reshape(sh)
    small_shapes = [w[nm].shape for nm in SMALL]
    outs = _adamw(_pack([grads[nm] for nm in SMALL]), _pack([w[nm] for nm in SMALL]), _pack([m[nm] for nm in SMALL]),
                  _pack([v[nm] for nm in SMALL]), name="adamw_small")
    for store, slab in zip((delta, new_m, new_v), outs):
        for nm, a in zip(SMALL, _unpack(slab, small_shapes)):
            store[nm] = a

    return (loss, grad_x, *[grads[nm] for nm in TWIN_WEIGHTS], *[delta[nm] for nm in TWIN_WEIGHTS],
            *[new_m[nm] for nm in TWIN_WEIGHTS], *[new_v[nm] for nm in TWIN_WEIGHTS])
```

```python
import functools

import jax
import jax.numpy as jnp
from jax import lax
from jax.experimental import pallas as pl
from jax.experimental.pallas import tpu as pltpu

F32 = jnp.float32
BF16 = jnp.bfloat16
MESH = pl.DeviceIdType.MESH
ANY = pl.BlockSpec(memory_space=pl.ANY)
VMEM = pl.BlockSpec(memory_space=pltpu.VMEM)

EPS = 1e-6
ADAM_LR, ADAM_B1, ADAM_B2, ADAM_EPS, ADAM_WD, ADAM_STEP = 0.001, 0.9, 0.999, 1e-08, 0.01, 10

D_CONF = 512
K_A, K_B, K_F = 31, 3, 3
HALO_A, HALO_S = 32, 8
N_CHIPS = 4
LANES = 128
CONV_ROWS = 128
TS_MIX, TS_FFN, TS_NORM = 256, 128, 512
VMEM_LIMIT = 48 * 1024 * 1024

TWIN_WEIGHTS = ['mix_norm_g', 'w_in', 'b_in', 'conv_a_w', 'conv_a_b', 'ln_a_g', 'ln_a_b', 'conv_b_w', 'w_out',
                'ffn_norm_g', 'w_up', 'conv_f_w', 'w_down', 'final_norm_g']
BIG = ['w_in', 'w_out', 'w_up', 'w_down']
SMALL = [n for n in TWIN_WEIGHTS if n not in BIG]
CHANNEL_SHARDED = ['conv_a_w', 'conv_b_w', 'conv_f_w']


def _params(n_grid=1):
    return pltpu.CompilerParams(dimension_semantics=("arbitrary",) * n_grid, vmem_limit_bytes=VMEM_LIMIT)


def _pick(n, cands):
    for c in cands:
        if n % c == 0:
            return c
    raise ValueError(f"no tile for {n}")


_DN = {"nn": (((1,), (0,)), ((), ())), "nt": (((1,), (1,)), ((), ())), "tn": (((0,), (0,)), ((), ()))}


def _mm(a, b, *, mode, name, layer=None, b_sharded=False, out_sharded=False, res=None, stack=None, out_dtype=F32):
    dn = _DN[mode]
    lead = () if layer is None else (None,)
    pre = () if layer is None else (layer,)
    if mode == "tn":
        k, m = a.shape
        n = b.shape[1]
        tm = _pick(m, (512, 256))
        tn = n // N_CHIPS if out_sharded else _pick(n, (1408, 1280, 1024, 512))
        a_spec = pl.BlockSpec((k, tm), lambda i, j: (0, i))
        b_spec = pl.BlockSpec((k, tn), lambda i, j: (0, j))
    else:
        m, k = a.shape
        tm = _pick(m, (512, 256, 128))
        a_spec = pl.BlockSpec((tm, k), lambda i, j: (i, 0))
        if mode == "nn":
            if b_sharded:
                n, tn = b.shape[-1] * N_CHIPS, b.shape[-1]
                b_spec = pl.BlockSpec(lead + (None, k, tn), lambda i, j: pre + (j, 0, 0))
            else:
                n = b.shape[-1]
                tn = _pick(n, (1408, 1280, 1024, 512))
                b_spec = pl.BlockSpec(lead + (k, tn), lambda i, j: pre + (0, j))
        else:
            if b_sharded:
                n, ks = b.shape[-2], b.shape[-1]
                tn = _pick(n, (512, 256))
                b_spec = pl.BlockSpec(lead + (N_CHIPS, tn, ks), lambda i, j: pre + (0, j, 0))
            else:
                n = b.shape[-2]
                tn = _pick(n, (1408, 1280, 1024, 512))
                b_spec = pl.BlockSpec(lead + (tn, k), lambda i, j: pre + (j, 0))
    slot_lead, slot_pre, slot_shape = (), (), ()
    if stack is not None:
        slot_lead, slot_pre, slot_shape = (None,), (stack[0],), (stack[1],)
    if out_sharded:
        out_shape = jax.ShapeDtypeStruct(slot_shape + (N_CHIPS, m, tn), out_dtype)
        o_spec = pl.BlockSpec(slot_lead + (None, tm, tn), lambda i, j: slot_pre + (j, i, 0))
    else:
        out_shape = jax.ShapeDtypeStruct(slot_shape + (m, n), out_dtype)
        o_spec = pl.BlockSpec(slot_lead + (tm, tn), lambda i, j: slot_pre + (i, j))
    in_specs = [a_spec, b_spec]
    args = [a, b]
    aliases = {}
    if res is not None:
        in_specs.append(pl.BlockSpec((tm, tn), lambda i, j: (i, j)))
        args.append(res)
    if stack is not None and stack[2] is not None:
        aliases = {len(args): 0}
        in_specs.append(ANY)
        args.append(stack[2])
    n_in = len(args)

    def body(*refs):
        a_ref, b_ref, o_ref = refs[0], refs[1], refs[n_in]
        if mode == "nt" and b_sharded:
            ks_ = b_ref.shape[2]
            acc = None
            for s in range(N_CHIPS):
                part = lax.dot_general(a_ref[:, s * ks_:(s + 1) * ks_], b_ref[s], dn, preferred_element_type=F32)
                acc = part if acc is None else acc + part
        else:
            acc = lax.dot_general(a_ref[...], b_ref[...], dn, preferred_element_type=F32)
        if res is not None:
            acc = acc + refs[2][...]
        o_ref[...] = acc.astype(o_ref.dtype)

    return pl.pallas_call(body, name=name, grid=(m // tm, n // tn), in_specs=in_specs, out_specs=o_spec,
                          out_shape=out_shape, input_output_aliases=aliases, compiler_params=_params(2))(*args)


def _conv_rows(win_ref, w_ref, offsets, rows, cols, emit, bias_ref=None):
    for c0 in range(0, cols, LANES):
        cs = slice(c0, c0 + LANES)
        wk = [w_ref[k:k + 1, cs] for k in range(len(offsets))]
        for r0 in range(0, rows, CONV_ROWS):
            acc = None
            for k, off in enumerate(offsets):
                term = wk[k] * win_ref[off + r0:off + r0 + CONV_ROWS, cs]
                acc = term if acc is None else acc + term
            if bias_ref is not None:
                acc = acc + bias_ref[:, cs]
            emit(r0, c0, acc)


def _conv_wgrad(g_ref, x_ref, offsets, rows, cols, dw_ref):
    for c0 in range(0, cols, LANES):
        cs = slice(c0, c0 + LANES)
        for k, off in enumerate(offsets):
            acc = None
            for r0 in range(0, rows, CONV_ROWS):
                term = g_ref[r0:r0 + CONV_ROWS, cs] * x_ref[off + r0:off + r0 + CONV_ROWS, cs]
                acc = term if acc is None else acc + term
            dw_ref[k:k + 1, cs] += jnp.sum(acc, axis=0, keepdims=True)


def _store_into(ref):
    def emit(r0, c0, v):
        ref[r0:r0 + CONV_ROWS, c0:c0 + LANES] = v
    return emit


def _causal_offsets(halo, k):
    return [halo - (k - 1) + j for j in range(k)]


def _anticausal_offsets(k):
    return [(k - 1) - j for j in range(k)]


def _row(v):
    return v.reshape(1, -1)


def _full(shape):
    return pl.BlockSpec(shape, lambda i: (0,) * len(shape))


def _sigmoid(v):
    return jax.nn.sigmoid(v)


def _dsilu(v, sg):
    return sg * (1.0 + v * (1.0 - sg))


def _norm_fwd(x, g, *, name):
    s, d = x.shape
    ts = _pick(s, (TS_NORM, 256, 128))

    def body(x_ref, g_ref, h_ref):
        xv = x_ref[...]
        r = lax.rsqrt(jnp.mean(xv * xv, axis=-1, keepdims=True) + EPS)
        h_ref[...] = (xv * r * g_ref[...]).astype(BF16)

    return pl.pallas_call(body, name=name, grid=(s // ts,),
                          in_specs=[pl.BlockSpec((ts, d), lambda i: (i, 0)), _full((1, d))],
                          out_specs=pl.BlockSpec((ts, d), lambda i: (i, 0)),
                          out_shape=jax.ShapeDtypeStruct((s, d), BF16), compiler_params=_params())(x, _row(g))


def _mix_pw_fwd(u, b_in, wa, ba, lg, lb, wb, *, name):
    s = u.shape[0]
    c = D_CONF
    ts = _pick(s, (TS_MIX, 128))
    offs_a, offs_b = _causal_offsets(HALO_A, K_A), _causal_offsets(HALO_S, K_B)

    def body(u_ref, bin_ref, wa_ref, ba_ref, lg_ref, lb_ref, wb_ref, y_ref, p_ref, q_ref, e_ref, pwin, ewin, fbuf):
        i = pl.program_id(0)

        @pl.when(i == 0)
        def _():
            pwin[0:HALO_A, :] = jnp.zeros((HALO_A, c), F32)
            ewin[0:HALO_S, :] = jnp.zeros((HALO_S, c), F32)

        @pl.when(i > 0)
        def _():
            pwin[0:HALO_A, :] = pwin[ts:ts + HALO_A, :]
            ewin[0:HALO_S, :] = ewin[ts:ts + HALO_S, :]

        a_val = u_ref[:, 0:c] + bin_ref[:, 0:c]
        a_gate = u_ref[:, c:2 * c] + bin_ref[:, c:2 * c]
        p = a_val * _sigmoid(a_gate)
        pwin[HALO_A:HALO_A + ts, :] = p
        p_ref[...] = p
        e = (u_ref[:, 3 * c:4 * c] + bin_ref[:, 3 * c:4 * c]) * (u_ref[:, 4 * c:5 * c] + bin_ref[:, 4 * c:5 * c])
        ewin[HALO_S:HALO_S + ts, :] = e
        e_ref[...] = e
        _conv_rows(pwin, wa_ref, offs_a, ts, c, _store_into(q_ref), bias_ref=ba_ref)
        q = q_ref[...]
        mu = jnp.mean(q, axis=-1, keepdims=True)
        xc = q - mu
        var = jnp.mean(xc * xc, axis=-1, keepdims=True)
        ln = xc * lax.rsqrt(var + EPS) * lg_ref[...] + lb_ref[...]
        y_ref[:, 0:c] = (ln * _sigmoid(ln)).astype(BF16)
        _conv_rows(ewin, wb_ref, offs_b, ts, c, _store_into(fbuf))
        g_b = u_ref[:, 2 * c:3 * c] + bin_ref[:, 2 * c:3 * c]
        y_ref[:, c:2 * c] = (g_b * fbuf[...]).astype(BF16)

    tile = lambda w: pl.BlockSpec((ts, w), lambda i: (i, 0))
    return pl.pallas_call(
        body, name=name, grid=(s // ts,),
        in_specs=[tile(5 * c), _full((1, 5 * c)), _full((K_A, c)), _full((1, c)), _full((1, c)), _full((1, c)),
                  _full((K_B, c))],
        out_specs=[tile(2 * c), tile(c), tile(c), tile(c)],
        out_shape=[jax.ShapeDtypeStruct((s, 2 * c), BF16)] + [jax.ShapeDtypeStruct((s, c), F32)] * 3,
        scratch_shapes=[pltpu.VMEM((ts + HALO_A, c), F32), pltpu.VMEM((ts + HALO_S, c), F32), pltpu.VMEM((ts, c), F32)],
        compiler_params=_params())(u, _row(b_in), wa, _row(ba), _row(lg), _row(lb), wb)


def _ffn_pw_fwd(u2, wf, *, name):
    s, f2 = u2.shape
    f = f2 // 2
    ts = _pick(s, (TS_FFN,))
    offs = _causal_offsets(HALO_S, K_F)

    def body(u_ref, wf_ref, z_ref, uwin, cbuf):
        i = pl.program_id(0)

        @pl.when(i == 0)
        def _():
            uwin[0:HALO_S, :] = jnp.zeros((HALO_S, f2), F32)

        @pl.when(i > 0)
        def _():
            uwin[0:HALO_S, :] = uwin[ts:ts + HALO_S, :]

        uwin[HALO_S:HALO_S + ts, :] = u_ref[...]
        _conv_rows(uwin, wf_ref, offs, ts, f2, _store_into(cbuf))
        gate = cbuf[:, 0:f]
        z_ref[...] = (gate * _sigmoid(gate) * cbuf[:, f:f2]).astype(BF16)

    return pl.pallas_call(
        body, name=name, grid=(s // ts,),
        in_specs=[pl.BlockSpec((ts, f2), lambda i: (i, 0)), _full((K_F, f2))],
        out_specs=pl.BlockSpec((ts, f), lambda i: (i, 0)),
        out_shape=jax.ShapeDtypeStruct((s, f), BF16),
        scratch_shapes=[pltpu.VMEM((ts + HALO_S, f2), F32), pltpu.VMEM((ts, f2), F32)],
        compiler_params=_params())(u2, wf)


def _loss_bwd(x, tgt, g, *, name):
    s, d = x.shape
    ts = _pick(s, (TS_NORM, 256, 128))

    def body(x_ref, t_ref, g_ref, loss_ref, dx_ref, dxb_ref, dg_ref):
        @pl.when(pl.program_id(0) == 0)
        def _():
            loss_ref[...] = jnp.zeros_like(loss_ref)
            dg_ref[...] = jnp.zeros_like(dg_ref)

        xv = x_ref[...]
        r = lax.rsqrt(jnp.mean(xv * xv, axis=-1, keepdims=True) + EPS)
        xh = xv * r
        err = xh * g_ref[...] - t_ref[...]
        loss_ref[...] += 0.5 * jnp.sum(jnp.mean(err * err, axis=-1, keepdims=True))
        dy = err * (1.0 / d)
        dg_ref[...] += jnp.sum(dy * xh, axis=0, keepdims=True)
        dxh = dy * g_ref[...]
        dx = r * (dxh - xh * jnp.mean(dxh * xh, axis=-1, keepdims=True))
        dx_ref[...] = dx
        dxb_ref[...] = dx.astype(BF16)

    tile = pl.BlockSpec((ts, d), lambda i: (i, 0))
    return pl.pallas_call(
        body, name=name, grid=(s // ts,), in_specs=[tile, tile, _full((1, d))],
        out_specs=[_full((8, LANES)), tile, tile, _full((1, d))],
        out_shape=[jax.ShapeDtypeStruct((8, LANES), F32), jax.ShapeDtypeStruct((s, d), F32),
                   jax.ShapeDtypeStruct((s, d), BF16), jax.ShapeDtypeStruct((1, d), F32)],
        compiler_params=_params())(x, tgt, _row(g))


def _norm_bwd(x, dh, dx_in, g, *, name):
    s, d = x.shape
    ts = _pick(s, (TS_NORM, 256, 128))

    def body(x_ref, dh_ref, dxin_ref, g_ref, dx_ref, dxb_ref, dg_ref):
        @pl.when(pl.program_id(0) == 0)
        def _():
            dg_ref[...] = jnp.zeros_like(dg_ref)

        xv = x_ref[...]
        r = lax.rsqrt(jnp.mean(xv * xv, axis=-1, keepdims=True) + EPS)
        xh = xv * r
        dh_v = dh_ref[...]
        dg_ref[...] += jnp.sum(dh_v * xh, axis=0, keepdims=True)
        dxh = dh_v * g_ref[...]
        dx = dxin_ref[...] + r * (dxh - xh * jnp.mean(dxh * xh, axis=-1, keepdims=True))
        dx_ref[...] = dx
        dxb_ref[...] = dx.astype(BF16)

    tile = pl.BlockSpec((ts, d), lambda i: (i, 0))
    return pl.pallas_call(
        body, name=name, grid=(s // ts,), in_specs=[tile, tile, tile, _full((1, d))],
        out_specs=[tile, tile, _full((1, d))],
        out_shape=[jax.ShapeDtypeStruct((s, d), F32), jax.ShapeDtypeStruct((s, d), BF16),
                   jax.ShapeDtypeStruct((1, d), F32)],
        compiler_params=_params())(x, dh, dx_in, _row(g))


def _ffn_pw_bwd(u2, dz, wf, *, name):
    s, f2 = u2.shape
    f = f2 // 2
    ts = _pick(s, (TS_FFN,))
    nt = s // ts
    hb = ts // HALO_S
    offs_c, offs_t = _causal_offsets(HALO_S, K_F), _anticausal_offsets(K_F)

    def body(u_ref, uh_ref, dz_ref, wf_ref, du_ref, dwf_ref, uwin, cbuf, dcwin):
        i = pl.program_id(0)

        @pl.when(i == 0)
        def _():
            dwf_ref[...] = jnp.zeros_like(dwf_ref)
            dcwin[ts:ts + HALO_S, :] = jnp.zeros((HALO_S, f2), F32)

        @pl.when(i > 0)
        def _():
            dcwin[ts:ts + HALO_S, :] = dcwin[0:HALO_S, :]

        @pl.when(i == nt - 1)
        def _():
            uwin[0:HALO_S, :] = jnp.zeros((HALO_S, f2), F32)

        @pl.when(i < nt - 1)
        def _():
            uwin[0:HALO_S, :] = uh_ref[...]

        uwin[HALO_S:HALO_S + ts, :] = u_ref[...]
        _conv_rows(uwin, wf_ref, offs_c, ts, f2, _store_into(cbuf))
        gate = cbuf[:, 0:f]
        sg = _sigmoid(gate)
        dz_v = dz_ref[...]
        dcwin[0:ts, 0:f] = dz_v * cbuf[:, f:f2] * _dsilu(gate, sg)
        dcwin[0:ts, f:f2] = dz_v * gate * sg

        def emit(r0, c0, v):
            du_ref[r0:r0 + CONV_ROWS, c0:c0 + LANES] = v.astype(BF16)
        _conv_rows(dcwin, wf_ref, offs_t, ts, f2, emit)
        _conv_wgrad(dcwin, uwin, offs_c, ts, f2, dwf_ref)

    rev = lambda i: (nt - 1 - i, 0)
    return pl.pallas_call(
        body, name=name, grid=(nt,),
        in_specs=[pl.BlockSpec((ts, f2), rev),
                  pl.BlockSpec((HALO_S, f2), lambda i: (jnp.maximum((nt - 1 - i) * hb - 1, 0), 0)),
                  pl.BlockSpec((ts, f), rev), _full((K_F, f2))],
        out_specs=[pl.BlockSpec((ts, f2), rev), _full((K_F, f2))],
        out_shape=[jax.ShapeDtypeStruct((s, f2), BF16), jax.ShapeDtypeStruct((K_F, f2), F32)],
        scratch_shapes=[pltpu.VMEM((ts + HALO_S, f2), F32), pltpu.VMEM((ts, f2), F32),
                        pltpu.VMEM((ts + HALO_S, f2), F32)],
        compiler_params=_params())(u2, u2, dz, wf)


def _mix_pw_bwd(dy, u, p, q, e, b_in, wa, lg, lb, wb, *, name):
    s = u.shape[0]
    c = D_CONF
    ts = _pick(s, (TS_MIX, 128))
    nt = s // ts
    offs_ac, offs_at = _causal_offsets(HALO_A, K_A), _anticausal_offsets(K_A)
    offs_bc, offs_bt = _causal_offsets(HALO_S, K_B), _anticausal_offsets(K_B)

    def body(dy_ref, u_ref, p_ref, ph_ref, q_ref, e_ref, eh_ref, bin_ref, wa_ref, lg_ref, lb_ref, wb_ref,
             du_ref, dbin_ref, dwa_ref, dba_ref, dlg_ref, dlb_ref, dwb_ref, pwin, ewin, dqwin, dfwin, buf):
        i = pl.program_id(0)

        @pl.when(i == 0)
        def _():
            for r in (dbin_ref, dwa_ref, dba_ref, dlg_ref, dlb_ref, dwb_ref):
                r[...] = jnp.zeros_like(r)
            dqwin[ts:ts + HALO_A, :] = jnp.zeros((HALO_A, c), F32)
            dfwin[ts:ts + HALO_S, :] = jnp.zeros((HALO_S, c), F32)

        @pl.when(i > 0)
        def _():
            dqwin[ts:ts + HALO_A, :] = dqwin[0:HALO_A, :]
            dfwin[ts:ts + HALO_S, :] = dfwin[0:HALO_S, :]

        @pl.when(i == nt - 1)
        def _():
            pwin[0:HALO_A, :] = jnp.zeros((HALO_A, c), F32)
            ewin[0:HALO_S, :] = jnp.zeros((HALO_S, c), F32)

        @pl.when(i < nt - 1)
        def _():
            pwin[0:HALO_A, :] = ph_ref[...]
            ewin[0:HALO_S, :] = eh_ref[...]

        pwin[HALO_A:HALO_A + ts, :] = p_ref[...]
        ewin[HALO_S:HALO_S + ts, :] = e_ref[...]

        def colsum(v):
            return jnp.sum(v, axis=0, keepdims=True)

        q_v = q_ref[...]
        mu = jnp.mean(q_v, axis=-1, keepdims=True)
        xc = q_v - mu
        rstd = lax.rsqrt(jnp.mean(xc * xc, axis=-1, keepdims=True) + EPS)
        nrm = xc * rstd
        ln = nrm * lg_ref[...] + lb_ref[...]
        dln = dy_ref[:, 0:c] * _dsilu(ln, _sigmoid(ln))
        dlg_ref[...] += colsum(dln * nrm)
        dlb_ref[...] += colsum(dln)
        dn = dln * lg_ref[...]
        dq = rstd * (dn - jnp.mean(dn, axis=-1, keepdims=True) - nrm * jnp.mean(dn * nrm, axis=-1, keepdims=True))
        dba_ref[...] += colsum(dq)
        dqwin[0:ts, :] = dq
        _conv_rows(dqwin, wa_ref, offs_at, ts, c, _store_into(buf))
        _conv_wgrad(dqwin, pwin, offs_ac, ts, c, dwa_ref)
        dp = buf[...]
        a_val = u_ref[:, 0:c] + bin_ref[:, 0:c]
        sg = _sigmoid(u_ref[:, c:2 * c] + bin_ref[:, c:2 * c])
        d_aval = dp * sg
        d_agate = dp * a_val * sg * (1.0 - sg)
        du_ref[:, 0:c] = d_aval.astype(BF16)
        du_ref[:, c:2 * c] = d_agate.astype(BF16)
        dbin_ref[:, 0:c] += colsum(d_aval)
        dbin_ref[:, c:2 * c] += colsum(d_agate)

        ds = dy_ref[:, c:2 * c]
        _conv_rows(ewin, wb_ref, offs_bc, ts, c, _store_into(buf))
        d_gb = ds * buf[...]
        dfwin[0:ts, :] = ds * (u_ref[:, 2 * c:3 * c] + bin_ref[:, 2 * c:3 * c])
        _conv_rows(dfwin, wb_ref, offs_bt, ts, c, _store_into(buf))
        _conv_wgrad(dfwin, ewin, offs_bc, ts, c, dwb_ref)
        de = buf[...]
        d_gc = de * (u_ref[:, 4 * c:5 * c] + bin_ref[:, 4 * c:5 * c])
        d_v = de * (u_ref[:, 3 * c:4 * c] + bin_ref[:, 3 * c:4 * c])
        du_ref[:, 2 * c:3 * c] = d_gb.astype(BF16)
        du_ref[:, 3 * c:4 * c] = d_gc.astype(BF16)
        du_ref[:, 4 * c:5 * c] = d_v.astype(BF16)
        dbin_ref[:, 2 * c:3 * c] += colsum(d_gb)
        dbin_ref[:, 3 * c:4 * c] += colsum(d_gc)
        dbin_ref[:, 4 * c:5 * c] += colsum(d_v)

    rev = lambda i: (nt - 1 - i, 0)
    tile = lambda w: pl.BlockSpec((ts, w), rev)
    halo = lambda h: pl.BlockSpec((h, c), lambda i: (jnp.maximum((nt - 1 - i) * (ts // h) - 1, 0), 0))
    small = [(1, 5 * c), (K_A, c), (1, c), (1, c), (1, c), (K_B, c)]
    return pl.pallas_call(
        body, name=name, grid=(nt,),
        in_specs=[tile(2 * c), tile(5 * c), tile(c), halo(HALO_A), tile(c), tile(c), halo(HALO_S),
                  _full((1, 5 * c)), _full((K_A, c)), _full((1, c)), _full((1, c)), _full((K_B, c))],
        out_specs=[tile(5 * c)] + [_full(sh) for sh in small],
        out_shape=[jax.ShapeDtypeStruct((s, 5 * c), BF16)] + [jax.ShapeDtypeStruct(sh, F32) for sh in small],
        scratch_shapes=[pltpu.VMEM((ts + HALO_A, c), F32), pltpu.VMEM((ts + HALO_S, c), F32),
                        pltpu.VMEM((ts + HALO_A, c), F32), pltpu.VMEM((ts + HALO_S, c), F32),
                        pltpu.VMEM((ts, c), F32)],
        compiler_params=_params())(dy, u, p, p, q, e, e, _row(b_in), wa, _row(lg), _row(lb), wb)


def _adamw(g, w, m, v, *, name):
    r, c = g.shape
    tr = _pick(r, (256, 128, 64, 8)) if r > 256 else r

    def body(g_ref, w_ref, m_ref, v_ref, d_ref, nm_ref, nv_ref):
        gv = g_ref[...]
        nm = ADAM_B1 * m_ref[...] + (1.0 - ADAM_B1) * gv
        nv = ADAM_B2 * v_ref[...] + (1.0 - ADAM_B2) * (gv * gv)
        m_hat = nm / (1.0 - ADAM_B1 ** ADAM_STEP)
        v_hat = nv / (1.0 - ADAM_B2 ** ADAM_STEP)
        d_ref[...] = -ADAM_LR * (m_hat / (jnp.sqrt(v_hat) + ADAM_EPS) + ADAM_WD * w_ref[...])
        nm_ref[...] = nm
        nv_ref[...] = nv

    tile = pl.BlockSpec((tr, c), lambda i: (i, 0))
    return pl.pallas_call(body, name=name, grid=(r // tr,), in_specs=[tile] * 4, out_specs=[tile] * 3,
                          out_shape=[jax.ShapeDtypeStruct((r, c), F32)] * 3, compiler_params=_params())(g, w, m, v)


def _cast_place(wshard, chip, *, name):
    l, r2, c = wshard.shape
    tr = r2 // 2

    def body(chip_ref, w_ref, o_ref):
        o_ref[...] = w_ref[...].astype(BF16)

    grid_spec = pltpu.PrefetchScalarGridSpec(
        num_scalar_prefetch=1, grid=(l, 2),
        in_specs=[pl.BlockSpec((None, tr, c), lambda li, i, chip_ref: (li, i, 0))],
        out_specs=pl.BlockSpec((None, None, tr, c), lambda li, i, chip_ref: (li, chip_ref[0], i, 0)))
    return pl.pallas_call(body, name=name, grid_spec=grid_spec,
                          out_shape=jax.ShapeDtypeStruct((l, N_CHIPS, r2, c), BF16), compiler_params=_params(2))(chip, wshard)


def _pair_sum(g, recv, core, *, name):
    l, _, _, r, c = g.shape

    def body(core_ref, g_ref, r_ref, o_ref):
        o_ref[...] = (g_ref[...] + r_ref[...]).astype(BF16)

    grid_spec = pltpu.PrefetchScalarGridSpec(
        num_scalar_prefetch=1, grid=(l, N_CHIPS),
        in_specs=[pl.BlockSpec((None, None, None, r, c), lambda li, i, core_ref: (li, i, core_ref[0], 0, 0)),
                  pl.BlockSpec((None, None, r, c), lambda li, i, core_ref: (li, i, 0, 0))],
        out_specs=pl.BlockSpec((None, None, r, c), lambda li, i, core_ref: (li, i, 0, 0)))
    return pl.pallas_call(body, name=name, grid_spec=grid_spec,
                          out_shape=jax.ShapeDtypeStruct((l, N_CHIPS, r, c), BF16), compiler_params=_params(2))(core, g, recv)


def _chip_sum(parts, core, *, name):
    l, _, r, c = parts.shape
    tr = r // 2

    def body(core_ref, p_ref, o_ref):
        acc = p_ref[0].astype(F32)
        for s in range(1, N_CHIPS):
            acc = acc + p_ref[s].astype(F32)
        o_ref[...] = acc

    grid_spec = pltpu.PrefetchScalarGridSpec(
        num_scalar_prefetch=1, grid=(l, 2),
        in_specs=[pl.BlockSpec((None, N_CHIPS, tr, c), lambda li, i, core_ref: (li, 0, i, 0))],
        out_specs=pl.BlockSpec((None, None, tr, c), lambda li, i, core_ref: (li, core_ref[0], i, 0)))
    return pl.pallas_call(body, name=name, grid_spec=grid_spec,
                          out_shape=jax.ShapeDtypeStruct((l, 2, r, c), F32), compiler_params=_params(2))(core, parts)


def _place():
    x, y, c = lax.axis_index("x"), lax.axis_index("y"), lax.axis_index("c")
    chips = [(1 - x, y), (x, 1 - y), (1 - x, 1 - y)]
    return x, y, c, 2 * x + y, chips


def _remote(src, dst, send_sems, recv_sems, k, dev):
    return pltpu.make_async_remote_copy(src_ref=src, dst_ref=dst, send_sem=send_sems.at[k], recv_sem=recv_sems.at[k],
                                        device_id=dev, device_id_type=MESH)


def _allgather_inplace(bufs, *, name):
    n = len(bufs)
    idx = [(l, a) for l in range(max(b.shape[0] for b in bufs)) for a in range(n) if l < bufs[a].shape[0]]

    def body(*refs):
        ins, outs = refs[:n], refs[n:2 * n]
        send_sems, recv_sems = refs[2 * n:]
        x, y, c, j, chips = _place()
        sib = (x, y, 1 - c)
        sends = []
        for qi, (l, a) in enumerate(idx):
            for k, (cx, cy) in enumerate(chips):
                cp = _remote(ins[a].at[l, j, c], outs[a].at[l, j, c], send_sems, recv_sems, 6 * qi + k, (cx, cy, c))
                cp.start()
                sends.append(cp)
        for qi, (l, a) in enumerate(idx):
            for k, (cx, cy) in enumerate(chips):
                blk = outs[a].at[l, 2 * cx + cy, c]
                _remote(blk, blk, send_sems, recv_sems, 6 * qi + k, (cx, cy, c)).wait_recv()
                cp = _remote(blk, blk, send_sems, recv_sems, 6 * qi + 3 + k, sib)
                cp.start()
                sends.append(cp)
        for qi, (l, a) in enumerate(idx):
            for k, (cx, cy) in enumerate(chips):
                blk = outs[a].at[l, 2 * cx + cy, 1 - c]
                _remote(blk, blk, send_sems, recv_sems, 6 * qi + 3 + k, sib).wait_recv()
        for cp in sends:
            cp.wait_send()

    return pl.pallas_call(
        body, name=name, in_specs=[ANY] * n, out_specs=[ANY] * n,
        out_shape=[jax.ShapeDtypeStruct(b.shape, b.dtype) for b in bufs],
        input_output_aliases={a: a for a in range(n)},
        scratch_shapes=[pltpu.SemaphoreType.DMA((6 * len(idx),)), pltpu.SemaphoreType.DMA((6 * len(idx),))])(*bufs)


def _sibling_halves(grads, *, name):
    n = len(grads)
    out_shape = [jax.ShapeDtypeStruct(g.shape[:2] + g.shape[3:], g.dtype) for g in grads]

    def body(*refs):
        ins, outs = refs[:n], refs[n:2 * n]
        send_sems, recv_sems = refs[2 * n:]
        x, y, c, _, _ = _place()
        cps = [_remote(ins[a].at[:, :, 1 - c], outs[a], send_sems, recv_sems, a, (x, y, 1 - c)) for a in range(n)]
        for cp in cps:
            cp.start()
        for cp in cps:
            cp.wait()

    return pl.pallas_call(body, name=name, in_specs=[ANY] * n, out_specs=[ANY] * n, out_shape=out_shape,
                          scratch_shapes=[pltpu.SemaphoreType.DMA((n,)), pltpu.SemaphoreType.DMA((n,))])(*grads)


def _scatter_to_chips(sums, *, name):
    n = len(sums)
    out_shape = [jax.ShapeDtypeStruct(s.shape, s.dtype) for s in sums]

    def body(*refs):
        ins, outs = refs[:n], refs[n:2 * n]
        send_sems, recv_sems, loc_sems = refs[2 * n:]
        x, y, c, j, chips = _place()
        sends = [_remote(ins[a].at[:, 2 * cx + cy], outs[a].at[:, j], send_sems, recv_sems, 3 * a + k, (cx, cy, c))
                 for a in range(n) for k, (cx, cy) in enumerate(chips)]
        for cp in sends:
            cp.start()
        local = [pltpu.make_async_copy(ins[a].at[:, j], outs[a].at[:, j], loc_sems.at[a]) for a in range(n)]
        for cp in local:
            cp.start()
        for a in range(n):
            for k, (cx, cy) in enumerate(chips):
                blk = outs[a].at[:, 2 * cx + cy]
                _remote(blk, blk, send_sems, recv_sems, 3 * a + k, (cx, cy, c)).wait_recv()
        for cp in sends:
            cp.wait_send()
        for cp in local:
            cp.wait()

    return pl.pallas_call(
        body, name=name, in_specs=[ANY] * n, out_specs=[ANY] * n, out_shape=out_shape,
        scratch_shapes=[pltpu.SemaphoreType.DMA((3 * n,)), pltpu.SemaphoreType.DMA((3 * n,)),
                        pltpu.SemaphoreType.DMA((n,))])(*sums)


def _share_halves(bufs, *, name):
    n = len(bufs)

    def body(*refs):
        ins, outs = refs[:n], refs[n:2 * n]
        send_sems, recv_sems = refs[2 * n:]
        x, y, c, _, _ = _place()
        sends = [_remote(ins[a].at[:, c], outs[a].at[:, c], send_sems, recv_sems, a, (x, y, 1 - c)) for a in range(n)]
        for cp in sends:
            cp.start()
        for a in range(n):
            blk = outs[a].at[:, 1 - c]
            _remote(blk, blk, send_sems, recv_sems, a, (x, y, 1 - c)).wait_recv()
        for cp in sends:
            cp.wait_send()

    return pl.pallas_call(
        body, name=name, in_specs=[ANY] * n, out_specs=[ANY] * n,
        out_shape=[jax.ShapeDtypeStruct(b.shape, b.dtype) for b in bufs],
        input_output_aliases={a: a for a in range(n)},
        scratch_shapes=[pltpu.SemaphoreType.DMA((n,)), pltpu.SemaphoreType.DMA((n,))])(*bufs)


def _allreduce_small(v, *, name):
    r, c = v.shape
    n_dev = 8

    def body(v_ref, o_ref, buf, send_sems, recv_sems):
        x, y, cc = lax.axis_index("x"), lax.axis_index("y"), lax.axis_index("c")
        me = 4 * x + 2 * y + cc
        cps = []
        for d in range(1, n_dev):
            peer = (1 - x if d & 4 else x, 1 - y if d & 2 else y, 1 - cc if d & 1 else cc)
            cps.append(_remote(v_ref, buf.at[me], send_sems, recv_sems, d - 1, peer))
        for cp in cps:
            cp.start()
        buf[me] = v_ref[...]
        for cp in cps:
            cp.wait()
        acc = buf[0]
        for d in range(1, n_dev):
            acc = acc + buf[d]
        o_ref[...] = acc

    return pl.pallas_call(
        body, name=name, in_specs=[VMEM], out_specs=VMEM, out_shape=jax.ShapeDtypeStruct((r, c), F32),
        scratch_shapes=[pltpu.VMEM((n_dev, r, c), F32), pltpu.SemaphoreType.DMA((n_dev - 1,)),
                        pltpu.SemaphoreType.DMA((n_dev - 1,))])(v)


def _pack(arrays, row_multiple=8):
    flat = jnp.concatenate([a.reshape(-1) for a in arrays])
    rows = -(-flat.shape[0] // LANES)
    rows = -(-rows // row_multiple) * row_multiple
    return jnp.pad(flat, (0, rows * LANES - flat.shape[0])).reshape(rows, LANES)


def _unpack(slab, shapes):
    flat = slab.reshape(-1)
    out, pos = [], 0
    for sh in shapes:
        size = 1
        for dim in sh:
            size *= dim
        out.append(flat[pos:pos + size].reshape(sh))
        pos += size
    return out


def _halves(w):
    l, r, c = w.shape
    return w.reshape(l, 2, r // 2, c)


def kernel(x, mix_norm_g, w_in, b_in, conv_a_w, conv_a_b, ln_a_g, ln_a_b, conv_b_w, w_out, ffn_norm_g, w_up, conv_f_w, w_down, final_norm_g, loss_target, m_mix_norm_g, m_w_in, m_b_in, m_conv_a_w, m_conv_a_b, m_ln_a_g, m_ln_a_b, m_conv_b_w, m_w_out, m_ffn_norm_g, m_w_up, m_conv_f_w, m_w_down, m_final_norm_g, v_mix_norm_g, v_w_in, v_b_in, v_conv_a_w, v_conv_a_b, v_ln_a_g, v_ln_a_b, v_conv_b_w, v_w_out, v_ffn_norm_g, v_w_up, v_conv_f_w, v_w_down, v_final_norm_g):
    w = dict(mix_norm_g=mix_norm_g, w_in=w_in, b_in=b_in, conv_a_w=conv_a_w, conv_a_b=conv_a_b, ln_a_g=ln_a_g,
             ln_a_b=ln_a_b, conv_b_w=conv_b_w, w_out=w_out, ffn_norm_g=ffn_norm_g, w_up=w_up, conv_f_w=conv_f_w,
             w_down=w_down, final_norm_g=final_norm_g)
    m = dict(mix_norm_g=m_mix_norm_g, w_in=m_w_in, b_in=m_b_in, conv_a_w=m_conv_a_w, conv_a_b=m_conv_a_b,
             ln_a_g=m_ln_a_g, ln_a_b=m_ln_a_b, conv_b_w=m_conv_b_w, w_out=m_w_out, ffn_norm_g=m_ffn_norm_g,
             w_up=m_w_up, conv_f_w=m_conv_f_w, w_down=m_w_down, final_norm_g=m_final_norm_g)
    v = dict(mix_norm_g=v_mix_norm_g, w_in=v_w_in, b_in=v_b_in, conv_a_w=v_conv_a_w, conv_a_b=v_conv_a_b,
             ln_a_g=v_ln_a_g, ln_a_b=v_ln_a_b, conv_b_w=v_conv_b_w, w_out=v_w_out, ffn_norm_g=v_ffn_norm_g,
             w_up=v_w_up, conv_f_w=v_conv_f_w, w_down=v_w_down, final_norm_g=v_final_norm_g)
    depth = w_in.shape[0]
    xs = x[0]
    tgt = loss_target[0]
    chip = (2 * lax.axis_index("x") + lax.axis_index("y")).astype(jnp.int32)
    chip_arr = chip.reshape(1)
    core = lax.axis_index("c").astype(jnp.int32).reshape(1)

    def five_d(b):
        return b.reshape(b.shape[0], N_CHIPS, 2, b.shape[2] // 2, b.shape[3])

    conv_shapes = [w[nm].shape for nm in CHANNEL_SHARDED]
    conv_slab = _pack([w[nm] for nm in CHANNEL_SHARDED], row_multiple=16)
    zero = jnp.int32(0)
    slab_buf = lax.dynamic_update_slice(jnp.zeros((1, N_CHIPS) + conv_slab.shape, F32), conv_slab[None, None],
                                        (zero, chip, zero, zero))
    bufs = [_cast_place(w[nm], chip_arr, name=f"cast_{nm}") for nm in BIG] + [slab_buf]
    gathered = _allgather_inplace([five_d(b) for b in bufs], name="gather_weights")
    conv_full = gathered[len(BIG)].reshape(N_CHIPS, -1, LANES)
    taps = {}
    for nm, parts in zip(CHANNEL_SHARDED, zip(*[_unpack(conv_full[jj], conv_shapes) for jj in range(N_CHIPS)])):
        taps[nm] = jnp.concatenate(parts, axis=-1)
    wfull = {}
    for nm, g in zip(BIG, gathered):
        if nm in ("w_in", "w_up"):
            wfull[nm] = g.reshape(depth, N_CHIPS, 2 * g.shape[3], g.shape[4])
        else:
            wfull[nm] = g.reshape(depth, N_CHIPS * 2 * g.shape[3], g.shape[4])

    saved = []
    cur = xs
    for l in range(depth):
        h1 = _norm_fwd(cur, w["mix_norm_g"][l], name=f"norm_mix_{l}")
        u = _mm(h1, wfull["w_in"], layer=l, mode="nn", b_sharded=True, name=f"mm_in_{l}")
        y, p, q, e = _mix_pw_fwd(u, w["b_in"][l], taps["conv_a_w"][l], w["conv_a_b"][l], w["ln_a_g"][l],
                                 w["ln_a_b"][l], taps["conv_b_w"][l], name=f"mix_fwd_{l}")
        x1 = _mm(y, wfull["w_out"], layer=l, mode="nn", res=cur, name=f"mm_out_{l}")
        h2 = _norm_fwd(x1, w["ffn_norm_g"][l], name=f"norm_ffn_{l}")
        u2 = _mm(h2, wfull["w_up"], layer=l, mode="nn", b_sharded=True, name=f"mm_up_{l}")
        z = _ffn_pw_fwd(u2, taps["conv_f_w"][l], name=f"ffn_fwd_{l}")
        x2 = _mm(z, wfull["w_down"], layer=l, mode="nn", res=x1, name=f"mm_down_{l}")
        saved.append((cur, h1, u, y, p, q, e, x1, h2, u2, z))
        cur = x2

    loss_blk, dx, dxb, dgf = _loss_bwd(cur, tgt, w["final_norm_g"], name="loss_bwd")
    loss = lax.psum(loss_blk[0, 0], ("x", "y", "c"))

    gbuf = {nm: None for nm in BIG}
    sg = {nm: [None] * depth for nm in SMALL if nm != "final_norm_g"}
    for l in reversed(range(depth)):
        x0, h1, u, y, p, q, e, x1, h2, u2, z = saved[l]
        dz = _mm(dxb, wfull["w_down"], layer=l, mode="nt", name=f"mm_ddown_{l}")
        gbuf["w_down"] = _mm(z, dxb, mode="tn", stack=(l, depth, gbuf["w_down"]), name=f"mm_gdown_{l}")
        du2, dwf = _ffn_pw_bwd(u2, dz, taps["conv_f_w"][l], name=f"ffn_bwd_{l}")
        dh2 = _mm(du2, wfull["w_up"], layer=l, mode="nt", b_sharded=True, name=f"mm_dup_{l}")
        gbuf["w_up"] = _mm(h2, du2, mode="tn", out_sharded=True, stack=(l, depth, gbuf["w_up"]), name=f"mm_gup_{l}")
        dx1, dx1b, dg2 = _norm_bwd(x1, dh2, dx, w["ffn_norm_g"][l], name=f"norm_ffn_bwd_{l}")
        dy = _mm(dx1b, wfull["w_out"], layer=l, mode="nt", name=f"mm_dout_{l}")
        gbuf["w_out"] = _mm(y, dx1b, mode="tn", stack=(l, depth, gbuf["w_out"]), name=f"mm_gout_{l}")
        du, dbin, dwa, dba, dlg, dlb, dwb = _mix_pw_bwd(
            dy, u, p, q, e, w["b_in"][l], taps["conv_a_w"][l], w["ln_a_g"][l], w["ln_a_b"][l],
            taps["conv_b_w"][l], name=f"mix_bwd_{l}")
        dh1 = _mm(du, wfull["w_in"], layer=l, mode="nt", b_sharded=True, name=f"mm_din_{l}")
        gbuf["w_in"] = _mm(h1, du, mode="tn", out_sharded=True, stack=(l, depth, gbuf["w_in"]), name=f"mm_gin_{l}")
        dx, dxb, dg1 = _norm_bwd(x0, dh1, dx1, w["mix_norm_g"][l], name=f"norm_mix_bwd_{l}")
        for nm, g in (("mix_norm_g", dg1), ("b_in", dbin), ("conv_a_w", dwa), ("conv_a_b", dba), ("ln_a_g", dlg),
                      ("ln_a_b", dlb), ("conv_b_w", dwb), ("ffn_norm_g", dg2), ("conv_f_w", dwf)):
            sg[nm][l] = g.reshape(g.shape[-1]) if g.shape[0] == 1 else g
    grad_x = dx[None]

    gl = []
    for nm in BIG:
        g = gbuf[nm]
        if g.ndim == 3:
            g = g.reshape(depth, N_CHIPS, g.shape[1] // N_CHIPS, g.shape[2])
        gl.append(five_d(g))
    from_sibling = _sibling_halves(gl, name="reduce_sibling")
    chip_sums = [_pair_sum(g, r, core, name=f"pair_sum_{nm}") for nm, g, r in zip(BIG, gl, from_sibling)]
    from_chips = _scatter_to_chips(chip_sums, name="reduce_chips")
    mine = [_chip_sum(parts, core, name=f"chip_sum_{nm}") for nm, parts in zip(BIG, from_chips)]
    shared = _share_halves(mine, name="share_halves")
    grads = {}
    for nm, g in zip(BIG, shared):
        grads[nm] = g.reshape(w[nm].shape)

    small_full_shapes = []
    small_parts = []
    for nm in SMALL:
        if nm == "final_norm_g":
            g = dgf.reshape(-1)
        else:
            g = jnp.stack(sg[nm])
        small_parts.append(g)
        small_full_shapes.append(g.shape)
    summed = _unpack(_allreduce_small(_pack(small_parts), name="reduce_small"), small_full_shapes)
    for nm, g in zip(SMALL, summed):
        if nm in CHANNEL_SHARDED:
            width = w[nm].shape[-1]
            g = lax.dynamic_slice_in_dim(g, chip * width, width, axis=2)
        grads[nm] = g

    delta, new_m, new_v = {}, {}, {}
    for nm in BIG:
        sh = w[nm].shape
        two_d = lambda a: a.reshape(sh[0] * sh[1], sh[2])
        d_, m_, v_ = _adamw(two_d(grads[nm]), two_d(w[nm]), two_d(m[nm]), two_d(v[nm]), name=f"adamw_{nm}")
        delta[nm], new_m[nm], new_v[nm] = d_.reshape(sh), m_.reshape(sh), v_.reshape(sh)
    small_shapes = [w[nm].shape for nm in SMALL]
    outs = _adamw(_pack([grads[nm] for nm in SMALL]), _pack([w[nm] for nm in SMALL]), _pack([m[nm] for nm in SMALL]),
                  _pack([v[nm] for nm in SMALL]), name="adamw_small")
    for store, slab in zip((delta, new_m, new_v), outs):
        for nm, a in zip(SMALL, _unpack(slab, small_shapes)):
            store[nm] = a

    return (loss, grad_x, *[grads[nm] for nm in TWIN_WEIGHTS], *[delta[nm] for nm in TWIN_WEIGHTS],
            *[new_m[nm] for nm in TWIN_WEIGHTS], *[new_v[nm] for nm in TWIN_WEIGHTS])
```

```python
import functools

import jax
import jax.numpy as jnp
from jax import lax
from jax.experimental import pallas as pl
from jax.experimental.pallas import tpu as pltpu

F32 = jnp.float32
BF16 = jnp.bfloat16
MESH = pl.DeviceIdType.MESH
ANY = pl.BlockSpec(memory_space=pl.ANY)
VMEM = pl.BlockSpec(memory_space=pltpu.VMEM)

EPS = 1e-6
ADAM_LR, ADAM_B1, ADAM_B2, ADAM_EPS, ADAM_WD, ADAM_STEP = 0.001, 0.9, 0.999, 1e-08, 0.01, 10

D_CONF = 512
K_A, K_B, K_F = 31, 3, 3
HALO_A, HALO_S = 32, 8
N_CHIPS = 4
LANES = 128
CONV_ROWS = 128
TS_MIX, TS_FFN, TS_NORM = 256, 128, 512
VMEM_LIMIT = 48 * 1024 * 1024

TWIN_WEIGHTS = ['mix_norm_g', 'w_in', 'b_in', 'conv_a_w', 'conv_a_b', 'ln_a_g', 'ln_a_b', 'conv_b_w', 'w_out',
                'ffn_norm_g', 'w_up', 'conv_f_w', 'w_down', 'final_norm_g']
BIG = ['w_in', 'w_out', 'w_up', 'w_down']
SMALL = [n for n in TWIN_WEIGHTS if n not in BIG]
CHANNEL_SHARDED = ['conv_a_w', 'conv_b_w', 'conv_f_w']


def _params(n_grid=1):
    return pltpu.CompilerParams(dimension_semantics=("arbitrary",) * n_grid, vmem_limit_bytes=VMEM_LIMIT)


def _pick(n, cands):
    for c in cands:
        if n % c == 0:
            return c
    raise ValueError(f"no tile for {n}")


_DN = {"nn": (((1,), (0,)), ((), ())), "nt": (((1,), (1,)), ((), ())), "tn": (((0,), (0,)), ((), ()))}


def _mm(a, b, *, mode, name, layer=None, b_sharded=False, out_sharded=False, res=None, stack=None, out_dtype=F32):
    dn = _DN[mode]
    lead = () if layer is None else (None,)
    pre = () if layer is None else (layer,)
    wide = (1408, 1280, 1024, 512)
    if mode == "tn":
        k, m = a.shape
        n = b.shape[1]
        tm = _pick(m, (512, 256))
        tn = n // N_CHIPS if out_sharded else _pick(n, wide)
    else:
        m, k = a.shape
        tm = _pick(m, (512, 256, 128))
        if mode == "nn":
            n = b.shape[-1] * N_CHIPS if b_sharded else b.shape[-1]
            tn = b.shape[-1] if b_sharded else _pick(n, wide)
        else:
            n = b.shape[-2]
            tn = _pick(n, (1024, 512, 256)) if b_sharded else _pick(n, wide)
    n_outer = (n // tn - 1) * a.size < (m // tm - 1) * (b.size if layer is None else b.size // b.shape[0])

    def im(f):
        return (lambda g0, g1: f(g1, g0)) if n_outer else f

    if mode == "tn":
        a_spec = pl.BlockSpec((k, tm), im(lambda i, j: (0, i)))
        b_spec = pl.BlockSpec((k, tn), im(lambda i, j: (0, j)))
    else:
        a_spec = pl.BlockSpec((tm, k), im(lambda i, j: (i, 0)))
        if mode == "nn" and b_sharded:
            b_spec = pl.BlockSpec(lead + (None, k, tn), im(lambda i, j: pre + (j, 0, 0)))
        elif mode == "nn":
            b_spec = pl.BlockSpec(lead + (k, tn), im(lambda i, j: pre + (0, j)))
        elif b_sharded:
            b_spec = pl.BlockSpec(lead + (N_CHIPS, tn, b.shape[-1]), im(lambda i, j: pre + (0, j, 0)))
        else:
            b_spec = pl.BlockSpec(lead + (tn, k), im(lambda i, j: pre + (j, 0)))
    slot_lead, slot_pre, slot_shape = (), (), ()
    if stack is not None:
        slot_lead, slot_pre, slot_shape = (None,), (stack[0],), (stack[1],)
    if out_sharded:
        out_shape = jax.ShapeDtypeStruct(slot_shape + (N_CHIPS, m, tn), out_dtype)
        o_spec = pl.BlockSpec(slot_lead + (None, tm, tn), im(lambda i, j: slot_pre + (j, i, 0)))
    else:
        out_shape = jax.ShapeDtypeStruct(slot_shape + (m, n), out_dtype)
        o_spec = pl.BlockSpec(slot_lead + (tm, tn), im(lambda i, j: slot_pre + (i, j)))
    in_specs = [a_spec, b_spec]
    args = [a, b]
    aliases = {}
    if res is not None:
        in_specs.append(pl.BlockSpec((tm, tn), im(lambda i, j: (i, j))))
        args.append(res)
    if stack is not None and stack[2] is not None:
        aliases = {len(args): 0}
        in_specs.append(ANY)
        args.append(stack[2])
    n_in = len(args)

    def body(*refs):
        a_ref, b_ref, o_ref = refs[0], refs[1], refs[n_in]
        if mode == "nt" and b_sharded:
            ks_ = b_ref.shape[2]
            acc = None
            for s in range(N_CHIPS):
                part = lax.dot_general(a_ref[:, s * ks_:(s + 1) * ks_], b_ref[s], dn, preferred_element_type=F32)
                acc = part if acc is None else acc + part
        else:
            acc = lax.dot_general(a_ref[...], b_ref[...], dn, preferred_element_type=F32)
        if res is not None:
            acc = acc + refs[2][...]
        o_ref[...] = acc.astype(o_ref.dtype)

    grid = (n // tn, m // tm) if n_outer else (m // tm, n // tn)
    return pl.pallas_call(body, name=name, grid=grid, in_specs=in_specs, out_specs=o_spec,
                          out_shape=out_shape, input_output_aliases=aliases, compiler_params=_params(2))(*args)


def _conv_rows(win_ref, w_ref, offsets, rows, cols, emit, bias_ref=None):
    for c0 in range(0, cols, LANES):
        cs = slice(c0, c0 + LANES)
        wk = [w_ref[k:k + 1, cs] for k in range(len(offsets))]
        for r0 in range(0, rows, CONV_ROWS):
            acc = None
            for k, off in enumerate(offsets):
                term = wk[k] * win_ref[off + r0:off + r0 + CONV_ROWS, cs]
                acc = term if acc is None else acc + term
            if bias_ref is not None:
                acc = acc + bias_ref[:, cs]
            emit(r0, c0, acc)


def _conv_wgrad(g_ref, x_ref, offsets, rows, cols, dw_ref):
    for c0 in range(0, cols, LANES):
        cs = slice(c0, c0 + LANES)
        for k, off in enumerate(offsets):
            acc = None
            for r0 in range(0, rows, CONV_ROWS):
                term = g_ref[r0:r0 + CONV_ROWS, cs] * x_ref[off + r0:off + r0 + CONV_ROWS, cs]
                acc = term if acc is None else acc + term
            dw_ref[k:k + 1, cs] += jnp.sum(acc, axis=0, keepdims=True)


def _store_into(ref):
    def emit(r0, c0, v):
        ref[r0:r0 + CONV_ROWS, c0:c0 + LANES] = v
    return emit


def _causal_offsets(halo, k):
    return [halo - (k - 1) + j for j in range(k)]


def _anticausal_offsets(k):
    return [(k - 1) - j for j in range(k)]


def _row(v):
    return v.reshape(1, -1)


def _full(shape):
    return pl.BlockSpec(shape, lambda i: (0,) * len(shape))


def _sigmoid(v):
    return jax.nn.sigmoid(v)


def _dsilu(v, sg):
    return sg * (1.0 + v * (1.0 - sg))


def _norm_fwd(x, g, *, name):
    s, d = x.shape
    ts = _pick(s, (TS_NORM, 256, 128))

    def body(x_ref, g_ref, h_ref):
        xv = x_ref[...]
        r = lax.rsqrt(jnp.mean(xv * xv, axis=-1, keepdims=True) + EPS)
        h_ref[...] = (xv * r * g_ref[...]).astype(BF16)

    return pl.pallas_call(body, name=name, grid=(s // ts,),
                          in_specs=[pl.BlockSpec((ts, d), lambda i: (i, 0)), _full((1, d))],
                          out_specs=pl.BlockSpec((ts, d), lambda i: (i, 0)),
                          out_shape=jax.ShapeDtypeStruct((s, d), BF16), compiler_params=_params())(x, _row(g))


def _call(body, *, name, grid, in_specs, out_specs, out_shape, args, scratch_shapes=(), hook=None):
    n_in, n_out, n_scr = len(in_specs), len(out_specs), len(scratch_shapes)
    if hook is None:
        outs = pl.pallas_call(body, name=name, grid=grid, in_specs=in_specs, out_specs=out_specs, out_shape=out_shape,
                              scratch_shapes=list(scratch_shapes), compiler_params=_params(len(grid)))(*args)
        return list(outs), []
    h_in, h_out = len(hook.operands), len(hook.out_shape)
    last = grid[0] - 1

    def hosted(*refs):
        ins, refs = refs[:n_in], refs[n_in:]
        h_ins, refs = refs[:h_in], refs[h_in:]
        outs, refs = refs[:n_out], refs[n_out:]
        h_outs, refs = refs[:h_out], refs[h_out:]
        scr, sems = refs[:n_scr], refs[n_scr:]

        @pl.when(pl.program_id(0) == 0)
        def _():
            hook.start(h_ins, h_outs, sems)

        body(*ins, *outs, *scr)

        @pl.when(pl.program_id(0) == last)
        def _():
            hook.finish(h_ins, h_outs, sems)

    outs = pl.pallas_call(
        hosted, name=name, grid=grid, in_specs=list(in_specs) + [ANY] * h_in, out_specs=list(out_specs) + [ANY] * h_out,
        out_shape=list(out_shape) + list(hook.out_shape), scratch_shapes=list(scratch_shapes) + hook.sems(),
        input_output_aliases={n_in + k: n_out + v for k, v in hook.aliases.items()},
        compiler_params=_params(len(grid)))(*args, *hook.operands)
    return list(outs[:n_out]), list(outs[n_out:])


def _mix_pw_fwd(u, b_in, wa, ba, lg, lb, wb, *, name, hook=None):
    s = u.shape[0]
    c = D_CONF
    ts = _pick(s, (TS_MIX, 128))
    offs_a, offs_b = _causal_offsets(HALO_A, K_A), _causal_offsets(HALO_S, K_B)

    def body(u_ref, bin_ref, wa_ref, ba_ref, lg_ref, lb_ref, wb_ref, y_ref, p_ref, q_ref, e_ref, pwin, ewin, fbuf):
        i = pl.program_id(0)

        @pl.when(i == 0)
        def _():
            pwin[0:HALO_A, :] = jnp.zeros((HALO_A, c), F32)
            ewin[0:HALO_S, :] = jnp.zeros((HALO_S, c), F32)

        @pl.when(i > 0)
        def _():
            pwin[0:HALO_A, :] = pwin[ts:ts + HALO_A, :]
            ewin[0:HALO_S, :] = ewin[ts:ts + HALO_S, :]

        a_val = u_ref[:, 0:c] + bin_ref[:, 0:c]
        a_gate = u_ref[:, c:2 * c] + bin_ref[:, c:2 * c]
        p = a_val * _sigmoid(a_gate)
        pwin[HALO_A:HALO_A + ts, :] = p
        p_ref[...] = p
        e = (u_ref[:, 3 * c:4 * c] + bin_ref[:, 3 * c:4 * c]) * (u_ref[:, 4 * c:5 * c] + bin_ref[:, 4 * c:5 * c])
        ewin[HALO_S:HALO_S + ts, :] = e
        e_ref[...] = e
        _conv_rows(pwin, wa_ref, offs_a, ts, c, _store_into(q_ref), bias_ref=ba_ref)
        q = q_ref[...]
        mu = jnp.mean(q, axis=-1, keepdims=True)
        xc = q - mu
        var = jnp.mean(xc * xc, axis=-1, keepdims=True)
        ln = xc * lax.rsqrt(var + EPS) * lg_ref[...] + lb_ref[...]
        y_ref[:, 0:c] = (ln * _sigmoid(ln)).astype(BF16)
        _conv_rows(ewin, wb_ref, offs_b, ts, c, _store_into(fbuf))
        g_b = u_ref[:, 2 * c:3 * c] + bin_ref[:, 2 * c:3 * c]
        y_ref[:, c:2 * c] = (g_b * fbuf[...]).astype(BF16)

    tile = lambda w: pl.BlockSpec((ts, w), lambda i: (i, 0))
    return _call(
        body, name=name, grid=(s // ts,),
        in_specs=[tile(5 * c), _full((1, 5 * c)), _full((K_A, c)), _full((1, c)), _full((1, c)), _full((1, c)),
                  _full((K_B, c))],
        out_specs=[tile(2 * c), tile(c), tile(c), tile(c)],
        out_shape=[jax.ShapeDtypeStruct((s, 2 * c), BF16)] + [jax.ShapeDtypeStruct((s, c), F32)] * 3,
        scratch_shapes=[pltpu.VMEM((ts + HALO_A, c), F32), pltpu.VMEM((ts + HALO_S, c), F32), pltpu.VMEM((ts, c), F32)],
        args=(u, _row(b_in), wa, _row(ba), _row(lg), _row(lb), wb), hook=hook)


def _ffn_pw_fwd(u2, wf, *, name, hook=None):
    s, f2 = u2.shape
    f = f2 // 2
    ts = _pick(s, (TS_FFN,))
    offs = _causal_offsets(HALO_S, K_F)

    def body(u_ref, wf_ref, z_ref, uwin, cbuf):
        i = pl.program_id(0)

        @pl.when(i == 0)
        def _():
            uwin[0:HALO_S, :] = jnp.zeros((HALO_S, f2), F32)

        @pl.when(i > 0)
        def _():
            uwin[0:HALO_S, :] = uwin[ts:ts + HALO_S, :]

        uwin[HALO_S:HALO_S + ts, :] = u_ref[...]
        _conv_rows(uwin, wf_ref, offs, ts, f2, _store_into(cbuf))
        gate = cbuf[:, 0:f]
        z_ref[...] = (gate * _sigmoid(gate) * cbuf[:, f:f2]).astype(BF16)

    return _call(
        body, name=name, grid=(s // ts,),
        in_specs=[pl.BlockSpec((ts, f2), lambda i: (i, 0)), _full((K_F, f2))],
        out_specs=[pl.BlockSpec((ts, f), lambda i: (i, 0))],
        out_shape=[jax.ShapeDtypeStruct((s, f), BF16)],
        scratch_shapes=[pltpu.VMEM((ts + HALO_S, f2), F32), pltpu.VMEM((ts, f2), F32)],
        args=(u2, wf), hook=hook)


def _loss_bwd(x, tgt, g, *, name):
    s, d = x.shape
    ts = _pick(s, (TS_NORM, 256, 128))

    def body(x_ref, t_ref, g_ref, loss_ref, dx_ref, dxb_ref, dg_ref):
        @pl.when(pl.program_id(0) == 0)
        def _():
            loss_ref[...] = jnp.zeros_like(loss_ref)
            dg_ref[...] = jnp.zeros_like(dg_ref)

        xv = x_ref[...]
        r = lax.rsqrt(jnp.mean(xv * xv, axis=-1, keepdims=True) + EPS)
        xh = xv * r
        err = xh * g_ref[...] - t_ref[...]
        loss_ref[...] += 0.5 * jnp.sum(jnp.mean(err * err, axis=-1, keepdims=True))
        dy = err * (1.0 / d)
        dg_ref[...] += jnp.sum(dy * xh, axis=0, keepdims=True)
        dxh = dy * g_ref[...]
        dx = r * (dxh - xh * jnp.mean(dxh * xh, axis=-1, keepdims=True))
        dx_ref[...] = dx
        dxb_ref[...] = dx.astype(BF16)

    tile = pl.BlockSpec((ts, d), lambda i: (i, 0))
    return pl.pallas_call(
        body, name=name, grid=(s // ts,), in_specs=[tile, tile, _full((1, d))],
        out_specs=[_full((8, LANES)), tile, tile, _full((1, d))],
        out_shape=[jax.ShapeDtypeStruct((8, LANES), F32), jax.ShapeDtypeStruct((s, d), F32),
                   jax.ShapeDtypeStruct((s, d), BF16), jax.ShapeDtypeStruct((1, d), F32)],
        compiler_params=_params())(x, tgt, _row(g))


def _norm_bwd(x, dh, dx_in, g, *, name):
    s, d = x.shape
    ts = _pick(s, (TS_NORM, 256, 128))

    def body(x_ref, dh_ref, dxin_ref, g_ref, dx_ref, dxb_ref, dg_ref):
        @pl.when(pl.program_id(0) == 0)
        def _():
            dg_ref[...] = jnp.zeros_like(dg_ref)

        xv = x_ref[...]
        r = lax.rsqrt(jnp.mean(xv * xv, axis=-1, keepdims=True) + EPS)
        xh = xv * r
        dh_v = dh_ref[...]
        dg_ref[...] += jnp.sum(dh_v * xh, axis=0, keepdims=True)
        dxh = dh_v * g_ref[...]
        dx = dxin_ref[...] + r * (dxh - xh * jnp.mean(dxh * xh, axis=-1, keepdims=True))
        dx_ref[...] = dx
        dxb_ref[...] = dx.astype(BF16)

    tile = pl.BlockSpec((ts, d), lambda i: (i, 0))
    return pl.pallas_call(
        body, name=name, grid=(s // ts,), in_specs=[tile, tile, tile, _full((1, d))],
        out_specs=[tile, tile, _full((1, d))],
        out_shape=[jax.ShapeDtypeStruct((s, d), F32), jax.ShapeDtypeStruct((s, d), BF16),
                   jax.ShapeDtypeStruct((1, d), F32)],
        compiler_params=_params())(x, dh, dx_in, _row(g))


def _ffn_pw_bwd(u2, dz, wf, *, name, hook=None):
    s, f2 = u2.shape
    f = f2 // 2
    ts = _pick(s, (TS_FFN,))
    nt = s // ts
    hb = ts // HALO_S
    offs_c, offs_t = _causal_offsets(HALO_S, K_F), _anticausal_offsets(K_F)

    def body(u_ref, uh_ref, dz_ref, wf_ref, du_ref, dwf_ref, uwin, cbuf, dcwin):
        i = pl.program_id(0)

        @pl.when(i == 0)
        def _():
            dwf_ref[...] = jnp.zeros_like(dwf_ref)
            dcwin[ts:ts + HALO_S, :] = jnp.zeros((HALO_S, f2), F32)

        @pl.when(i > 0)
        def _():
            dcwin[ts:ts + HALO_S, :] = dcwin[0:HALO_S, :]

        @pl.when(i == nt - 1)
        def _():
            uwin[0:HALO_S, :] = jnp.zeros((HALO_S, f2), F32)

        @pl.when(i < nt - 1)
        def _():
            uwin[0:HALO_S, :] = uh_ref[...]

        uwin[HALO_S:HALO_S + ts, :] = u_ref[...]
        _conv_rows(uwin, wf_ref, offs_c, ts, f2, _store_into(cbuf))
        gate = cbuf[:, 0:f]
        sg = _sigmoid(gate)
        dz_v = dz_ref[...]
        dcwin[0:ts, 0:f] = dz_v * cbuf[:, f:f2] * _dsilu(gate, sg)
        dcwin[0:ts, f:f2] = dz_v * gate * sg

        def emit(r0, c0, v):
            du_ref[r0:r0 + CONV_ROWS, c0:c0 + LANES] = v.astype(BF16)
        _conv_rows(dcwin, wf_ref, offs_t, ts, f2, emit)
        _conv_wgrad(dcwin, uwin, offs_c, ts, f2, dwf_ref)

    rev = lambda i: (nt - 1 - i, 0)
    return _call(
        body, name=name, grid=(nt,),
        in_specs=[pl.BlockSpec((ts, f2), rev),
                  pl.BlockSpec((HALO_S, f2), lambda i: (jnp.maximum((nt - 1 - i) * hb - 1, 0), 0)),
                  pl.BlockSpec((ts, f), rev), _full((K_F, f2))],
        out_specs=[pl.BlockSpec((ts, f2), rev), _full((K_F, f2))],
        out_shape=[jax.ShapeDtypeStruct((s, f2), BF16), jax.ShapeDtypeStruct((K_F, f2), F32)],
        scratch_shapes=[pltpu.VMEM((ts + HALO_S, f2), F32), pltpu.VMEM((ts, f2), F32),
                        pltpu.VMEM((ts + HALO_S, f2), F32)],
        args=(u2, u2, dz, wf), hook=hook)


def _mix_pw_bwd(dy, u, p, q, e, b_in, wa, lg, lb, wb, *, name, hook=None):
    s = u.shape[0]
    c = D_CONF
    ts = _pick(s, (TS_MIX, 128))
    nt = s // ts
    offs_ac, offs_at = _causal_offsets(HALO_A, K_A), _anticausal_offsets(K_A)
    offs_bc, offs_bt = _causal_offsets(HALO_S, K_B), _anticausal_offsets(K_B)

    def body(dy_ref, u_ref, p_ref, ph_ref, q_ref, e_ref, eh_ref, bin_ref, wa_ref, lg_ref, lb_ref, wb_ref,
             du_ref, dbin_ref, dwa_ref, dba_ref, dlg_ref, dlb_ref, dwb_ref, pwin, ewin, dqwin, dfwin, buf):
        i = pl.program_id(0)

        @pl.when(i == 0)
        def _():
            for r in (dbin_ref, dwa_ref, dba_ref, dlg_ref, dlb_ref, dwb_ref):
                r[...] = jnp.zeros_like(r)
            dqwin[ts:ts + HALO_A, :] = jnp.zeros((HALO_A, c), F32)
            dfwin[ts:ts + HALO_S, :] = jnp.zeros((HALO_S, c), F32)

        @pl.when(i > 0)
        def _():
            dqwin[ts:ts + HALO_A, :] = dqwin[0:HALO_A, :]
            dfwin[ts:ts + HALO_S, :] = dfwin[0:HALO_S, :]

        @pl.when(i == nt - 1)
        def _():
            pwin[0:HALO_A, :] = jnp.zeros((HALO_A, c), F32)
            ewin[0:HALO_S, :] = jnp.zeros((HALO_S, c), F32)

        @pl.when(i < nt - 1)
        def _():
            pwin[0:HALO_A, :] = ph_ref[...]
            ewin[0:HALO_S, :] = eh_ref[...]

        pwin[HALO_A:HALO_A + ts, :] = p_ref[...]
        ewin[HALO_S:HALO_S + ts, :] = e_ref[...]

        def colsum(v):
            return jnp.sum(v, axis=0, keepdims=True)

        q_v = q_ref[...]
        mu = jnp.mean(q_v, axis=-1, keepdims=True)
        xc = q_v - mu
        rstd = lax.rsqrt(jnp.mean(xc * xc, axis=-1, keepdims=True) + EPS)
        nrm = xc * rstd
        ln = nrm * lg_ref[...] + lb_ref[...]
        dln = dy_ref[:, 0:c] * _dsilu(ln, _sigmoid(ln))
        dlg_ref[...] += colsum(dln * nrm)
        dlb_ref[...] += colsum(dln)
        dn = dln * lg_ref[...]
        dq = rstd * (dn - jnp.mean(dn, axis=-1, keepdims=True) - nrm * jnp.mean(dn * nrm, axis=-1, keepdims=True))
        dba_ref[...] += colsum(dq)
        dqwin[0:ts, :] = dq
        _conv_rows(dqwin, wa_ref, offs_at, ts, c, _store_into(buf))
        _conv_wgrad(dqwin, pwin, offs_ac, ts, c, dwa_ref)
        dp = buf[...]
        a_val = u_ref[:, 0:c] + bin_ref[:, 0:c]
        sg = _sigmoid(u_ref[:, c:2 * c] + bin_ref[:, c:2 * c])
        d_aval = dp * sg
        d_agate = dp * a_val * sg * (1.0 - sg)
        du_ref[:, 0:c] = d_aval.astype(BF16)
        du_ref[:, c:2 * c] = d_agate.astype(BF16)
        dbin_ref[:, 0:c] += colsum(d_aval)
        dbin_ref[:, c:2 * c] += colsum(d_agate)

        ds = dy_ref[:, c:2 * c]
        _conv_rows(ewin, wb_ref, offs_bc, ts, c, _store_into(buf))
        d_gb = ds * buf[...]
        dfwin[0:ts, :] = ds * (u_ref[:, 2 * c:3 * c] + bin_ref[:, 2 * c:3 * c])
        _conv_rows(dfwin, wb_ref, offs_bt, ts, c, _store_into(buf))
        _conv_wgrad(dfwin, ewin, offs_bc, ts, c, dwb_ref)
        de = buf[...]
        d_gc = de * (u_ref[:, 4 * c:5 * c] + bin_ref[:, 4 * c:5 * c])
        d_v = de * (u_ref[:, 3 * c:4 * c] + bin_ref[:, 3 * c:4 * c])
        du_ref[:, 2 * c:3 * c] = d_gb.astype(BF16)
        du_ref[:, 3 * c:4 * c] = d_gc.astype(BF16)
        du_ref[:, 4 * c:5 * c] = d_v.astype(BF16)
        dbin_ref[:, 2 * c:3 * c] += colsum(d_gb)
        dbin_ref[:, 3 * c:4 * c] += colsum(d_gc)
        dbin_ref[:, 4 * c:5 * c] += colsum(d_v)

    rev = lambda i: (nt - 1 - i, 0)
    tile = lambda w: pl.BlockSpec((ts, w), rev)
    halo = lambda h: pl.BlockSpec((h, c), lambda i: (jnp.maximum((nt - 1 - i) * (ts // h) - 1, 0), 0))
    small = [(1, 5 * c), (K_A, c), (1, c), (1, c), (1, c), (K_B, c)]
    return _call(
        body, name=name, grid=(nt,),
        in_specs=[tile(2 * c), tile(5 * c), tile(c), halo(HALO_A), tile(c), tile(c), halo(HALO_S),
                  _full((1, 5 * c)), _full((K_A, c)), _full((1, c)), _full((1, c)), _full((K_B, c))],
        out_specs=[tile(5 * c)] + [_full(sh) for sh in small],
        out_shape=[jax.ShapeDtypeStruct((s, 5 * c), BF16)] + [jax.ShapeDtypeStruct(sh, F32) for sh in small],
        scratch_shapes=[pltpu.VMEM((ts + HALO_A, c), F32), pltpu.VMEM((ts + HALO_S, c), F32),
                        pltpu.VMEM((ts + HALO_A, c), F32), pltpu.VMEM((ts + HALO_S, c), F32),
                        pltpu.VMEM((ts, c), F32)],
        args=(dy, u, p, p, q, e, e, _row(b_in), wa, _row(lg), _row(lb), wb), hook=hook)


def _adamw(g, w, m, v, *, name):
    r, c = g.shape
    tr = _pick(r, (256, 128, 64, 8)) if r > 256 else r

    def body(g_ref, w_ref, m_ref, v_ref, d_ref, nm_ref, nv_ref):
        gv = g_ref[...]
        nm = ADAM_B1 * m_ref[...] + (1.0 - ADAM_B1) * gv
        nv = ADAM_B2 * v_ref[...] + (1.0 - ADAM_B2) * (gv * gv)
        m_hat = nm / (1.0 - ADAM_B1 ** ADAM_STEP)
        v_hat = nv / (1.0 - ADAM_B2 ** ADAM_STEP)
        d_ref[...] = -ADAM_LR * (m_hat / (jnp.sqrt(v_hat) + ADAM_EPS) + ADAM_WD * w_ref[...])
        nm_ref[...] = nm
        nv_ref[...] = nv

    tile = pl.BlockSpec((tr, c), lambda i: (i, 0))
    return pl.pallas_call(body, name=name, grid=(r // tr,), in_specs=[tile] * 4, out_specs=[tile] * 3,
                          out_shape=[jax.ShapeDtypeStruct((r, c), F32)] * 3, compiler_params=_params())(g, w, m, v)


def _cast_place(wshard, chip, *, name):
    l, r2, c = wshard.shape
    tr = r2 // 2

    def body(chip_ref, w_ref, *o_refs):
        for li, o_ref in enumerate(o_refs):
            o_ref[...] = w_ref[li].astype(BF16)

    grid_spec = pltpu.PrefetchScalarGridSpec(
        num_scalar_prefetch=1, grid=(2,),
        in_specs=[pl.BlockSpec((l, tr, c), lambda i, chip_ref: (0, i, 0))],
        out_specs=[pl.BlockSpec((None, tr, c), lambda i, chip_ref: (chip_ref[0], i, 0))] * l)
    return pl.pallas_call(body, name=name, grid_spec=grid_spec,
                          out_shape=[jax.ShapeDtypeStruct((N_CHIPS, r2, c), BF16)] * l, compiler_params=_params())(chip, wshard)


def _pair_sum(g, recv, core, *, name):
    _, _, r, c = g.shape

    def body(core_ref, g_ref, r_ref, o_ref):
        o_ref[...] = (g_ref[...] + r_ref[...]).astype(BF16)

    grid_spec = pltpu.PrefetchScalarGridSpec(
        num_scalar_prefetch=1, grid=(N_CHIPS,),
        in_specs=[pl.BlockSpec((None, None, r, c), lambda i, core_ref: (i, core_ref[0], 0, 0)),
                  pl.BlockSpec((None, r, c), lambda i, core_ref: (i, 0, 0))],
        out_specs=pl.BlockSpec((None, r, c), lambda i, core_ref: (i, 0, 0)))
    return pl.pallas_call(body, name=name, grid_spec=grid_spec,
                          out_shape=jax.ShapeDtypeStruct((N_CHIPS, r, c), BF16), compiler_params=_params())(core, g, recv)


def _chip_sum(own, recv, core_chip, stack, *, name):
    _, r, c = own.shape
    tr = r // 2
    l, layers, buf = stack

    def body(cc_ref, own_ref, r1_ref, r2_ref, r3_ref, *rest):
        o_ref = rest[-1]
        o_ref[...] = ((own_ref[...].astype(F32) + r1_ref[...].astype(F32)) + r2_ref[...].astype(F32)) + r3_ref[...].astype(F32)

    def slot(s):
        return pl.BlockSpec((None, tr, c), lambda i, cc_ref: ((cc_ref[1] + s) % N_CHIPS, i, 0))

    in_specs = [slot(0), slot(1), slot(2), slot(3)]
    args = [own, recv, recv, recv]
    aliases = {}
    if buf is not None:
        aliases = {1 + len(args): 0}
        in_specs.append(ANY)
        args.append(buf)
    grid_spec = pltpu.PrefetchScalarGridSpec(
        num_scalar_prefetch=1, grid=(2,), in_specs=in_specs,
        out_specs=pl.BlockSpec((None, None, tr, c), lambda i, cc_ref: (l, cc_ref[0], i, 0)))
    return pl.pallas_call(body, name=name, grid_spec=grid_spec, input_output_aliases=aliases,
                          out_shape=jax.ShapeDtypeStruct((layers, 2, r, c), F32), compiler_params=_params())(core_chip, *args)


def _place():
    x, y, c = lax.axis_index("x"), lax.axis_index("y"), lax.axis_index("c")
    chips = [(1 - x, y), (x, 1 - y), (1 - x, 1 - y)]
    return x, y, c, 2 * x + y, chips


def _remote(src, dst, send_sems, recv_sems, k, dev):
    return pltpu.make_async_remote_copy(src_ref=src, dst_ref=dst, send_sem=send_sems.at[k], recv_sem=recv_sems.at[k],
                                        device_id=dev, device_id_type=MESH)


def _sibling_halves(grads, *, name):
    n = len(grads)
    out_shape = [jax.ShapeDtypeStruct(g.shape[:1] + g.shape[2:], g.dtype) for g in grads]

    def body(*refs):
        ins, outs = refs[:n], refs[n:2 * n]
        send_sems, recv_sems = refs[2 * n:]
        x, y, c, _, _ = _place()
        cps = [_remote(ins[a].at[:, 1 - c], outs[a], send_sems, recv_sems, a, (x, y, 1 - c)) for a in range(n)]
        for cp in cps:
            cp.start()
        for cp in cps:
            cp.wait()

    return pl.pallas_call(body, name=name, in_specs=[ANY] * n, out_specs=[ANY] * n, out_shape=out_shape,
                          scratch_shapes=[pltpu.SemaphoreType.DMA((n,)), pltpu.SemaphoreType.DMA((n,))])(*grads)


class _Gather:
    def __init__(self, bufs):
        self.operands = list(bufs)
        self.out_shape = [jax.ShapeDtypeStruct(b.shape, b.dtype) for b in bufs]
        self.aliases = {a: a for a in range(len(bufs))}

    def sems(self):
        n = 6 * len(self.operands)
        return [pltpu.SemaphoreType.DMA((n,)), pltpu.SemaphoreType.DMA((n,))]

    def _sends(self, ins, outs, sems):
        x, y, c, j, chips = _place()
        return [_remote(ins[a].at[j, c], outs[a].at[j, c], sems[0], sems[1], 6 * a + k, (cx, cy, c))
                for a in range(len(ins)) for k, (cx, cy) in enumerate(chips)]

    def start(self, ins, outs, sems):
        for cp in self._sends(ins, outs, sems):
            cp.start()

    def finish(self, ins, outs, sems):
        x, y, c, j, chips = _place()
        sib = (x, y, 1 - c)
        passed = []
        for a in range(len(ins)):
            for k, (cx, cy) in enumerate(chips):
                blk = outs[a].at[2 * cx + cy, c]
                _remote(blk, blk, sems[0], sems[1], 6 * a + k, (cx, cy, c)).wait_recv()
                cp = _remote(blk, blk, sems[0], sems[1], 6 * a + 3 + k, sib)
                cp.start()
                passed.append(cp)
        for a in range(len(ins)):
            for k, (cx, cy) in enumerate(chips):
                blk = outs[a].at[2 * cx + cy, 1 - c]
                _remote(blk, blk, sems[0], sems[1], 6 * a + 3 + k, sib).wait_recv()
        for cp in self._sends(ins, outs, sems) + passed:
            cp.wait_send()


class _Scatter:
    def __init__(self, srcs):
        self.operands = list(srcs)
        self.out_shape = [jax.ShapeDtypeStruct(s.shape, s.dtype) for s in srcs]
        self.aliases = {}

    def sems(self):
        n = 3 * len(self.operands)
        return [pltpu.SemaphoreType.DMA((n,)), pltpu.SemaphoreType.DMA((n,))]

    def _sends(self, ins, outs, sems):
        x, y, c, j, chips = _place()
        return [_remote(ins[a].at[2 * cx + cy], outs[a].at[j], sems[0], sems[1], 3 * a + k, (cx, cy, c))
                for a in range(len(ins)) for k, (cx, cy) in enumerate(chips)]

    def start(self, ins, outs, sems):
        for cp in self._sends(ins, outs, sems):
            cp.start()

    def finish(self, ins, outs, sems):
        x, y, c, j, chips = _place()
        for a in range(len(ins)):
            for k, (cx, cy) in enumerate(chips):
                blk = outs[a].at[2 * cx + cy]
                _remote(blk, blk, sems[0], sems[1], 3 * a + k, (cx, cy, c)).wait_recv()
        for cp in self._sends(ins, outs, sems):
            cp.wait_send()


def _standalone(hook, *, name):
    n_in, n_out = len(hook.operands), len(hook.out_shape)

    def body(*refs):
        ins, outs, sems = refs[:n_in], refs[n_in:n_in + n_out], refs[n_in + n_out:]
        hook.start(ins, outs, sems)
        hook.finish(ins, outs, sems)

    return list(pl.pallas_call(body, name=name, in_specs=[ANY] * n_in, out_specs=[ANY] * n_out,
                               out_shape=hook.out_shape, input_output_aliases=hook.aliases,
                               scratch_shapes=hook.sems())(*hook.operands))


def _share_halves(bufs, *, name):
    n = len(bufs)

    def body(*refs):
        ins, outs = refs[:n], refs[n:2 * n]
        send_sems, recv_sems = refs[2 * n:]
        x, y, c, _, _ = _place()
        sends = [_remote(ins[a].at[:, c], outs[a].at[:, c], send_sems, recv_sems, a, (x, y, 1 - c)) for a in range(n)]
        for cp in sends:
            cp.start()
        for a in range(n):
            blk = outs[a].at[:, 1 - c]
            _remote(blk, blk, send_sems, recv_sems, a, (x, y, 1 - c)).wait_recv()
        for cp in sends:
            cp.wait_send()

    return pl.pallas_call(
        body, name=name, in_specs=[ANY] * n, out_specs=[ANY] * n,
        out_shape=[jax.ShapeDtypeStruct(b.shape, b.dtype) for b in bufs],
        input_output_aliases={a: a for a in range(n)},
        scratch_shapes=[pltpu.SemaphoreType.DMA((n,)), pltpu.SemaphoreType.DMA((n,))])(*bufs)


def _allreduce_small(v, *, name):
    r, c = v.shape
    n_dev = 8

    def body(v_ref, o_ref, buf, send_sems, recv_sems):
        x, y, cc = lax.axis_index("x"), lax.axis_index("y"), lax.axis_index("c")
        me = 4 * x + 2 * y + cc
        cps = []
        for d in range(1, n_dev):
            peer = (1 - x if d & 4 else x, 1 - y if d & 2 else y, 1 - cc if d & 1 else cc)
            cps.append(_remote(v_ref, buf.at[me], send_sems, recv_sems, d - 1, peer))
        for cp in cps:
            cp.start()
        buf[me] = v_ref[...]
        for cp in cps:
            cp.wait()
        acc = buf[0]
        for d in range(1, n_dev):
            acc = acc + buf[d]
        o_ref[...] = acc

    return pl.pallas_call(
        body, name=name, in_specs=[VMEM], out_specs=VMEM, out_shape=jax.ShapeDtypeStruct((r, c), F32),
        scratch_shapes=[pltpu.VMEM((n_dev, r, c), F32), pltpu.SemaphoreType.DMA((n_dev - 1,)),
                        pltpu.SemaphoreType.DMA((n_dev - 1,))])(v)


def _pack(arrays, row_multiple=8):
    flat = jnp.concatenate([a.reshape(-1) for a in arrays])
    rows = -(-flat.shape[0] // LANES)
    rows = -(-rows // row_multiple) * row_multiple
    return jnp.pad(flat, (0, rows * LANES - flat.shape[0])).reshape(rows, LANES)


def _unpack(slab, shapes):
    flat = slab.reshape(-1)
    out, pos = [], 0
    for sh in shapes:
        size = 1
        for dim in sh:
            size *= dim
        out.append(flat[pos:pos + size].reshape(sh))
        pos += size
    return out


def _halves(w):
    l, r, c = w.shape
    return w.reshape(l, 2, r // 2, c)


def kernel(x, mix_norm_g, w_in, b_in, conv_a_w, conv_a_b, ln_a_g, ln_a_b, conv_b_w, w_out, ffn_norm_g, w_up, conv_f_w, w_down, final_norm_g, loss_target, m_mix_norm_g, m_w_in, m_b_in, m_conv_a_w, m_conv_a_b, m_ln_a_g, m_ln_a_b, m_conv_b_w, m_w_out, m_ffn_norm_g, m_w_up, m_conv_f_w, m_w_down, m_final_norm_g, v_mix_norm_g, v_w_in, v_b_in, v_conv_a_w, v_conv_a_b, v_ln_a_g, v_ln_a_b, v_conv_b_w, v_w_out, v_ffn_norm_g, v_w_up, v_conv_f_w, v_w_down, v_final_norm_g):
    w = dict(mix_norm_g=mix_norm_g, w_in=w_in, b_in=b_in, conv_a_w=conv_a_w, conv_a_b=conv_a_b, ln_a_g=ln_a_g,
             ln_a_b=ln_a_b, conv_b_w=conv_b_w, w_out=w_out, ffn_norm_g=ffn_norm_g, w_up=w_up, conv_f_w=conv_f_w,
             w_down=w_down, final_norm_g=final_norm_g)
    m = dict(mix_norm_g=m_mix_norm_g, w_in=m_w_in, b_in=m_b_in, conv_a_w=m_conv_a_w, conv_a_b=m_conv_a_b,
             ln_a_g=m_ln_a_g, ln_a_b=m_ln_a_b, conv_b_w=m_conv_b_w, w_out=m_w_out, ffn_norm_g=m_ffn_norm_g,
             w_up=m_w_up, conv_f_w=m_conv_f_w, w_down=m_w_down, final_norm_g=m_final_norm_g)
    v = dict(mix_norm_g=v_mix_norm_g, w_in=v_w_in, b_in=v_b_in, conv_a_w=v_conv_a_w, conv_a_b=v_conv_a_b,
             ln_a_g=v_ln_a_g, ln_a_b=v_ln_a_b, conv_b_w=v_conv_b_w, w_out=v_w_out, ffn_norm_g=v_ffn_norm_g,
             w_up=v_w_up, conv_f_w=v_conv_f_w, w_down=v_w_down, final_norm_g=v_final_norm_g)
    depth = w_in.shape[0]
    xs = x[0]
    tgt = loss_target[0]
    chip = (2 * lax.axis_index("x") + lax.axis_index("y")).astype(jnp.int32)
    chip_arr = chip.reshape(1)
    core = lax.axis_index("c").astype(jnp.int32).reshape(1)
    core_chip = jnp.concatenate([core, chip_arr])

    def halves(b):
        return b.reshape(N_CHIPS, 2, b.shape[1] // 2, b.shape[2])

    conv_shapes = [w[nm].shape for nm in CHANNEL_SHARDED]
    conv_slab = _pack([w[nm] for nm in CHANNEL_SHARDED], row_multiple=16)
    zero = jnp.int32(0)
    slab_buf = lax.dynamic_update_slice(jnp.zeros((N_CHIPS,) + conv_slab.shape, F32), conv_slab[None], (chip, zero, zero))
    wbuf = {}
    for nm in BIG:
        for l, b in enumerate(_cast_place(w[nm], chip_arr, name=f"cast_{nm}")):
            wbuf[nm, l] = halves(b)

    def gather_into(keys, done):
        for key, g in zip(keys, done):
            wbuf[key] = g

    def wmat(nm, l):
        g = wbuf[nm, l]
        if nm in ("w_in", "w_up"):
            return g.reshape(N_CHIPS, 2 * g.shape[2], g.shape[3])
        return g.reshape(N_CHIPS * 2 * g.shape[2], g.shape[3])

    first = [("w_in", 0), ("w_out", 0)]
    done = _standalone(_Gather([wbuf[k] for k in first] + [halves(slab_buf)]), name="gather_first")
    gather_into(first, done)
    conv_full = done[-1].reshape(N_CHIPS, -1, LANES)
    taps = {}
    for nm, parts in zip(CHANNEL_SHARDED, zip(*[_unpack(conv_full[jj], conv_shapes) for jj in range(N_CHIPS)])):
        taps[nm] = jnp.concatenate(parts, axis=-1)

    saved = []
    cur = xs
    for l in range(depth):
        h1 = _norm_fwd(cur, w["mix_norm_g"][l], name=f"norm_mix_{l}")
        u = _mm(h1, wmat("w_in", l), mode="nn", b_sharded=True, name=f"mm_in_{l}")
        keys = [("w_up", l), ("w_down", l)] if l == 0 else [("w_up", l)]
        (y, p, q, e), done = _mix_pw_fwd(u, w["b_in"][l], taps["conv_a_w"][l], w["conv_a_b"][l], w["ln_a_g"][l],
                                         w["ln_a_b"][l], taps["conv_b_w"][l], name=f"mix_fwd_{l}",
                                         hook=_Gather([wbuf[k] for k in keys]))
        gather_into(keys, done)
        x1 = _mm(y, wmat("w_out", l), mode="nn", res=cur, name=f"mm_out_{l}")
        h2 = _norm_fwd(x1, w["ffn_norm_g"][l], name=f"norm_ffn_{l}")
        u2 = _mm(h2, wmat("w_up", l), mode="nn", b_sharded=True, name=f"mm_up_{l}")
        keys = [("w_in", l + 1), ("w_out", l + 1), ("w_down", l + 1)] if l + 1 < depth else []
        (z,), done = _ffn_pw_fwd(u2, taps["conv_f_w"][l], name=f"ffn_fwd_{l}",
                                 hook=_Gather([wbuf[k] for k in keys]) if keys else None)
        gather_into(keys, done)
        x2 = _mm(z, wmat("w_down", l), mode="nn", res=x1, name=f"mm_down_{l}")
        saved.append((cur, h1, u, y, p, q, e, x1, h2, u2, z))
        cur = x2

    loss_blk, dx, dxb, dgf = _loss_bwd(cur, tgt, w["final_norm_g"], name="loss_bwd")
    loss = lax.psum(loss_blk[0, 0], ("x", "y", "c"))

    gfull, chip_sums, from_chips = {}, {}, {}

    def four_d(g):
        if g.ndim == 2:
            g = g.reshape(N_CHIPS, g.shape[0] // N_CHIPS, g.shape[1])
        return halves(g)

    def scatter_hook(keys, tag):
        gl = [four_d(gfull[k]) for k in keys]
        from_sibling = _sibling_halves(gl, name=f"reduce_sibling_{tag}")
        for k, g, r in zip(keys, gl, from_sibling):
            chip_sums[k] = _pair_sum(g, r, core, name=f"pair_sum_{k[0]}_{k[1]}")
        return _Scatter([chip_sums[k] for k in keys])

    sg = {nm: [None] * depth for nm in SMALL if nm != "final_norm_g"}
    pending = []
    for l in reversed(range(depth)):
        x0, h1, u, y, p, q, e, x1, h2, u2, z = saved[l]
        dz = _mm(dxb, wmat("w_down", l), mode="nt", name=f"mm_ddown_{l}")
        gfull["w_down", l] = _mm(z, dxb, mode="tn", name=f"mm_gdown_{l}")
        keys = pending + [("w_down", l)]
        (du2, dwf), done = _ffn_pw_bwd(u2, dz, taps["conv_f_w"][l], name=f"ffn_bwd_{l}", hook=scatter_hook(keys, f"a{l}"))
        from_chips.update(zip(keys, done))
        dh2 = _mm(du2, wmat("w_up", l), mode="nt", b_sharded=True, name=f"mm_dup_{l}")
        gfull["w_up", l] = _mm(h2, du2, mode="tn", out_sharded=True, name=f"mm_gup_{l}")
        dx1, dx1b, dg2 = _norm_bwd(x1, dh2, dx, w["ffn_norm_g"][l], name=f"norm_ffn_bwd_{l}")
        dy = _mm(dx1b, wmat("w_out", l), mode="nt", name=f"mm_dout_{l}")
        gfull["w_out", l] = _mm(y, dx1b, mode="tn", name=f"mm_gout_{l}")
        keys = [("w_up", l), ("w_out", l)]
        (du, dbin, dwa, dba, dlg, dlb, dwb), done = _mix_pw_bwd(
            dy, u, p, q, e, w["b_in"][l], taps["conv_a_w"][l], w["ln_a_g"][l], w["ln_a_b"][l],
            taps["conv_b_w"][l], name=f"mix_bwd_{l}", hook=scatter_hook(keys, f"b{l}"))
        from_chips.update(zip(keys, done))
        dh1 = _mm(du, wmat("w_in", l), mode="nt", b_sharded=True, name=f"mm_din_{l}")
        gfull["w_in", l] = _mm(h1, du, mode="tn", out_sharded=True, name=f"mm_gin_{l}")
        pending = [("w_in", l)]
        dx, dxb, dg1 = _norm_bwd(x0, dh1, dx1, w["mix_norm_g"][l], name=f"norm_mix_bwd_{l}")
        for nm, g in (("mix_norm_g", dg1), ("b_in", dbin), ("conv_a_w", dwa), ("conv_a_b", dba), ("ln_a_g", dlg),
                      ("ln_a_b", dlb), ("conv_b_w", dwb), ("ffn_norm_g", dg2), ("conv_f_w", dwf)):
            sg[nm][l] = g.reshape(g.shape[-1]) if g.shape[0] == 1 else g
    grad_x = dx[None]
    from_chips.update(zip(pending, _standalone(scatter_hook(pending, "last"), name="reduce_chips_last")))

    mine = []
    for nm in BIG:
        buf = None
        for l in range(depth):
            buf = _chip_sum(chip_sums[nm, l], from_chips[nm, l], core_chip, (l, depth, buf), name=f"chip_sum_{nm}_{l}")
        mine.append(buf)
    shared = _share_halves(mine, name="share_halves")
    grads = {}
    for nm, g in zip(BIG, shared):
        grads[nm] = g.reshape(w[nm].shape)

    small_full_shapes = []
    small_parts = []
    for nm in SMALL:
        if nm == "final_norm_g":
            g = dgf.reshape(-1)
        else:
            g = jnp.stack(sg[nm])
        small_parts.append(g)
        small_full_shapes.append(g.shape)
    summed = _unpack(_allreduce_small(_pack(small_parts), name="reduce_small"), small_full_shapes)
    for nm, g in zip(SMALL, summed):
        if nm in CHANNEL_SHARDED:
            width = w[nm].shape[-1]
            g = lax.dynamic_slice_in_dim(g, chip * width, width, axis=2)
        grads[nm] = g

    delta, new_m, new_v = {}, {}, {}
    for nm in BIG:
        sh = w[nm].shape
        two_d = lambda a: a.reshape(sh[0] * sh[1], sh[2])
        d_, m_, v_ = _adamw(two_d(grads[nm]), two_d(w[nm]), two_d(m[nm]), two_d(v[nm]), name=f"adamw_{nm}")
        delta[nm], new_m[nm], new_v[nm] = d_.reshape(sh), m_.reshape(sh), v_.reshape(sh)
    small_shapes = [w[nm].shape for nm in SMALL]
    outs = _adamw(_pack([grads[nm] for nm in SMALL]), _pack([w[nm] for nm in SMALL]), _pack([m[nm] for nm in SMALL]),
                  _pack([v[nm] for nm in SMALL]), name="adamw_small")
    for store, slab in zip((delta, new_m, new_v), outs):
        for nm, a in zip(SMALL, _unpack(slab, small_shapes)):
            store[nm] = a

    return (loss, grad_x, *[grads[nm] for nm in TWIN_WEIGHTS], *[delta[nm] for nm in TWIN_WEIGHTS],
            *[new_m[nm] for nm in TWIN_WEIGHTS], *[new_v[nm] for nm in TWIN_WEIGHTS])
```

```python
import functools

import jax
import jax.numpy as jnp
from jax import lax
from jax.experimental import pallas as pl
from jax.experimental.pallas import tpu as pltpu

F32 = jnp.float32
BF16 = jnp.bfloat16
MESH = pl.DeviceIdType.MESH
ANY = pl.BlockSpec(memory_space=pl.ANY)
VMEM = pl.BlockSpec(memory_space=pltpu.VMEM)

EPS = 1e-6
ADAM_LR, ADAM_B1, ADAM_B2, ADAM_EPS, ADAM_WD, ADAM_STEP = 0.001, 0.9, 0.999, 1e-08, 0.01, 10

D_CONF = 512
K_A, K_B, K_F = 31, 3, 3
HALO_A, HALO_S = 32, 8
N_CHIPS = 4
LANES = 128
CONV_ROWS = 128
TS_MIX, TS_FFN, TS_NORM = 256, 128, 512
VMEM_LIMIT = 48 * 1024 * 1024
VMEM_LIMIT_WIDE = 56 * 1024 * 1024

TWIN_WEIGHTS = ['mix_norm_g', 'w_in', 'b_in', 'conv_a_w', 'conv_a_b', 'ln_a_g', 'ln_a_b', 'conv_b_w', 'w_out',
                'ffn_norm_g', 'w_up', 'conv_f_w', 'w_down', 'final_norm_g']
BIG = ['w_in', 'w_out', 'w_up', 'w_down']
SMALL = [n for n in TWIN_WEIGHTS if n not in BIG]
CHANNEL_SHARDED = ['conv_a_w', 'conv_b_w', 'conv_f_w']


def _params(n_grid=1):
    return pltpu.CompilerParams(dimension_semantics=("arbitrary",) * n_grid, vmem_limit_bytes=VMEM_LIMIT)


def _pick(n, cands):
    for c in cands:
        if n % c == 0:
            return c
    raise ValueError(f"no tile for {n}")


_DN = {"nn": (((1,), (0,)), ((), ())), "nt": (((1,), (1,)), ((), ())), "tn": (((0,), (0,)), ((), ()))}


def _mm(a, b, *, mode, name, layer=None, b_sharded=False, out_sharded=False, res=None, stack=None, out_dtype=F32):
    dn = _DN[mode]
    lead = () if layer is None else (None,)
    pre = () if layer is None else (layer,)
    wide = (1408, 1280, 1024, 512)
    if mode == "tn":
        k, m = a.shape
        n = b.shape[1]
        tm = _pick(m, (512, 256))
        tn = n // N_CHIPS if out_sharded else _pick(n, wide)
    else:
        m, k = a.shape
        tm = _pick(m, (1024, 512, 256, 128)) if k <= 1024 else _pick(m, (512, 256, 128))
        if mode == "nn":
            n = b.shape[-1] * N_CHIPS if b_sharded else b.shape[-1]
            tn = b.shape[-1] if b_sharded else _pick(n, wide)
        else:
            n = b.shape[-2]
            tn = _pick(n, (1024, 512, 256)) if b_sharded else _pick(n, wide)
    n_outer = (n // tn - 1) * a.size < (m // tm - 1) * (b.size if layer is None else b.size // b.shape[0])

    def im(f):
        return (lambda g0, g1: f(g1, g0)) if n_outer else f

    if mode == "tn":
        a_spec = pl.BlockSpec((k, tm), im(lambda i, j: (0, i)))
        b_spec = pl.BlockSpec((k, tn), im(lambda i, j: (0, j)))
    else:
        a_spec = pl.BlockSpec((tm, k), im(lambda i, j: (i, 0)))
        if mode == "nn" and b_sharded:
            b_spec = pl.BlockSpec(lead + (None, k, tn), im(lambda i, j: pre + (j, 0, 0)))
        elif mode == "nn":
            b_spec = pl.BlockSpec(lead + (k, tn), im(lambda i, j: pre + (0, j)))
        elif b_sharded:
            b_spec = pl.BlockSpec(lead + (N_CHIPS, tn, b.shape[-1]), im(lambda i, j: pre + (0, j, 0)))
        else:
            b_spec = pl.BlockSpec(lead + (tn, k), im(lambda i, j: pre + (j, 0)))
    slot_lead, slot_pre, slot_shape = (), (), ()
    if stack is not None:
        slot_lead, slot_pre, slot_shape = (None,), (stack[0],), (stack[1],)
    if out_sharded:
        out_shape = jax.ShapeDtypeStruct(slot_shape + (N_CHIPS, m, tn), out_dtype)
        o_spec = pl.BlockSpec(slot_lead + (None, tm, tn), im(lambda i, j: slot_pre + (j, i, 0)))
    else:
        out_shape = jax.ShapeDtypeStruct(slot_shape + (m, n), out_dtype)
        o_spec = pl.BlockSpec(slot_lead + (tm, tn), im(lambda i, j: slot_pre + (i, j)))
    in_specs = [a_spec, b_spec]
    args = [a, b]
    aliases = {}
    if res is not None:
        in_specs.append(pl.BlockSpec((tm, tn), im(lambda i, j: (i, j))))
        args.append(res)
    if stack is not None and stack[2] is not None:
        aliases = {len(args): 0}
        in_specs.append(ANY)
        args.append(stack[2])
    n_in = len(args)

    def body(*refs):
        a_ref, b_ref, o_ref = refs[0], refs[1], refs[n_in]
        if mode == "nt" and b_sharded:
            ks_ = b_ref.shape[2]
            acc = None
            for s in range(N_CHIPS):
                part = lax.dot_general(a_ref[:, s * ks_:(s + 1) * ks_], b_ref[s], dn, preferred_element_type=F32)
                acc = part if acc is None else acc + part
        else:
            acc = lax.dot_general(a_ref[...], b_ref[...], dn, preferred_element_type=F32)
        if res is not None:
            acc = acc + refs[2][...]
        o_ref[...] = acc.astype(o_ref.dtype)

    grid = (n // tn, m // tm) if n_outer else (m // tm, n // tn)
    return pl.pallas_call(body, name=name, grid=grid, in_specs=in_specs, out_specs=o_spec,
                          out_shape=out_shape, input_output_aliases=aliases, compiler_params=_params(2))(*args)


def _mm_res_norm(a, b, res, g, *, name):
    m, k = a.shape
    n = b.shape[1]
    tm = _pick(m, (1024, 512)) if k <= 1024 else _pick(m, (512, 256))

    def body(a_ref, b_ref, r_ref, g_ref, x_ref, h_ref):
        xv = r_ref[...] + lax.dot_general(a_ref[...], b_ref[...], _DN["nn"], preferred_element_type=F32)
        x_ref[...] = xv
        r = lax.rsqrt(jnp.mean(xv * xv, axis=-1, keepdims=True) + EPS)
        h_ref[...] = (xv * r * g_ref[...]).astype(BF16)

    rows = lambda w: pl.BlockSpec((tm, w), lambda i: (i, 0))
    return pl.pallas_call(
        body, name=name, grid=(m // tm,), in_specs=[rows(k), _full((k, n)), rows(n), _full((1, n))],
        out_specs=[rows(n), rows(n)],
        out_shape=[jax.ShapeDtypeStruct((m, n), F32), jax.ShapeDtypeStruct((m, n), BF16)],
        compiler_params=_params())(a, b, res, _row(g))


def _mm_norm_bwd(a, b4, x, dx_in, g, *, name):
    m, k = a.shape
    n, ks = b4.shape[1], b4.shape[2]
    tm = _pick(m, (512, 256))

    def body(a_ref, b_ref, x_ref, dxin_ref, g_ref, dx_ref, dxb_ref, dg_ref):
        @pl.when(pl.program_id(0) == 0)
        def _():
            dg_ref[...] = jnp.zeros_like(dg_ref)

        dh = None
        for s in range(N_CHIPS):
            part = lax.dot_general(a_ref[:, s * ks:(s + 1) * ks], b_ref[s], _DN["nt"], preferred_element_type=F32)
            dh = part if dh is None else dh + part
        xv = x_ref[...]
        r = lax.rsqrt(jnp.mean(xv * xv, axis=-1, keepdims=True) + EPS)
        xh = xv * r
        dg_ref[...] += jnp.sum(dh * xh, axis=0, keepdims=True)
        dxh = dh * g_ref[...]
        dx = dxin_ref[...] + r * (dxh - xh * jnp.mean(dxh * xh, axis=-1, keepdims=True))
        dx_ref[...] = dx
        dxb_ref[...] = dx.astype(BF16)

    rows = lambda w: pl.BlockSpec((tm, w), lambda i: (i, 0))
    weights = pl.BlockSpec((N_CHIPS, n, ks), lambda i: (0, 0, 0), pipeline_mode=pl.Buffered(1))
    return pl.pallas_call(
        body, name=name, grid=(m // tm,), in_specs=[rows(k), weights, rows(n), rows(n), _full((1, n))],
        out_specs=[rows(n), rows(n), _full((1, n))],
        out_shape=[jax.ShapeDtypeStruct((m, n), F32), jax.ShapeDtypeStruct((m, n), BF16),
                   jax.ShapeDtypeStruct((1, n), F32)],
        compiler_params=pltpu.CompilerParams(dimension_semantics=("arbitrary",), vmem_limit_bytes=VMEM_LIMIT_WIDE))(
            a, b4, x, dx_in, _row(g))


def _conv_rows(win_ref, w_ref, offsets, rows, cols, emit, bias_ref=None):
    for c0 in range(0, cols, LANES):
        cs = slice(c0, c0 + LANES)
        wk = [w_ref[k:k + 1, cs] for k in range(len(offsets))]
        for r0 in range(0, rows, CONV_ROWS):
            acc = None
            for k, off in enumerate(offsets):
                term = wk[k] * win_ref[off + r0:off + r0 + CONV_ROWS, cs]
                acc = term if acc is None else acc + term
            if bias_ref is not None:
                acc = acc + bias_ref[:, cs]
            emit(r0, c0, acc)


def _conv_wgrad(g_ref, x_ref, offsets, rows, cols, dw_ref):
    for c0 in range(0, cols, LANES):
        cs = slice(c0, c0 + LANES)
        for k, off in enumerate(offsets):
            acc = None
            for r0 in range(0, rows, CONV_ROWS):
                term = g_ref[r0:r0 + CONV_ROWS, cs] * x_ref[off + r0:off + r0 + CONV_ROWS, cs]
                acc = term if acc is None else acc + term
            dw_ref[k:k + 1, cs] += jnp.sum(acc, axis=0, keepdims=True)


def _store_into(ref):
    def emit(r0, c0, v):
        ref[r0:r0 + CONV_ROWS, c0:c0 + LANES] = v
    return emit


def _causal_offsets(halo, k):
    return [halo - (k - 1) + j for j in range(k)]


def _anticausal_offsets(k):
    return [(k - 1) - j for j in range(k)]


def _row(v):
    return v.reshape(1, -1)


def _full(shape):
    return pl.BlockSpec(shape, lambda i: (0,) * len(shape))


def _sigmoid(v):
    return jax.nn.sigmoid(v)


def _dsilu(v, sg):
    return sg * (1.0 + v * (1.0 - sg))


def _norm_fwd(x, g, *, name):
    s, d = x.shape
    ts = _pick(s, (TS_NORM, 256, 128))

    def body(x_ref, g_ref, h_ref):
        xv = x_ref[...]
        r = lax.rsqrt(jnp.mean(xv * xv, axis=-1, keepdims=True) + EPS)
        h_ref[...] = (xv * r * g_ref[...]).astype(BF16)

    return pl.pallas_call(body, name=name, grid=(s // ts,),
                          in_specs=[pl.BlockSpec((ts, d), lambda i: (i, 0)), _full((1, d))],
                          out_specs=pl.BlockSpec((ts, d), lambda i: (i, 0)),
                          out_shape=jax.ShapeDtypeStruct((s, d), BF16), compiler_params=_params())(x, _row(g))


def _call(body, *, name, grid, in_specs, out_specs, out_shape, args, scratch_shapes=(), hook=None):
    n_in, n_out, n_scr = len(in_specs), len(out_specs), len(scratch_shapes)
    if hook is None:
        outs = pl.pallas_call(body, name=name, grid=grid, in_specs=in_specs, out_specs=out_specs, out_shape=out_shape,
                              scratch_shapes=list(scratch_shapes), compiler_params=_params(len(grid)))(*args)
        return list(outs), []
    h_in, h_out = len(hook.operands), len(hook.out_shape)
    last = grid[0] - 1

    def hosted(*refs):
        ins, refs = refs[:n_in], refs[n_in:]
        h_ins, refs = refs[:h_in], refs[h_in:]
        outs, refs = refs[:n_out], refs[n_out:]
        h_outs, refs = refs[:h_out], refs[h_out:]
        scr, sems = refs[:n_scr], refs[n_scr:]

        @pl.when(pl.program_id(0) == 0)
        def _():
            hook.start(h_ins, h_outs, sems)

        body(*ins, *outs, *scr)

        @pl.when(pl.program_id(0) == last)
        def _():
            hook.finish(h_ins, h_outs, sems)

    outs = pl.pallas_call(
        hosted, name=name, grid=grid, in_specs=list(in_specs) + [ANY] * h_in, out_specs=list(out_specs) + [ANY] * h_out,
        out_shape=list(out_shape) + list(hook.out_shape), scratch_shapes=list(scratch_shapes) + hook.sems(),
        input_output_aliases={n_in + k: n_out + v for k, v in hook.aliases.items()},
        compiler_params=_params(len(grid)))(*args, *hook.operands)
    return list(outs[:n_out]), list(outs[n_out:])


def _mix_pw_fwd(u, b_in, wa, ba, lg, lb, wb, *, name, hook=None):
    s = u.shape[0]
    c = D_CONF
    ts = _pick(s, (TS_MIX, 128))
    offs_a, offs_b = _causal_offsets(HALO_A, K_A), _causal_offsets(HALO_S, K_B)

    def body(u_ref, bin_ref, wa_ref, ba_ref, lg_ref, lb_ref, wb_ref, y_ref, p_ref, q_ref, e_ref, pwin, ewin, fbuf):
        i = pl.program_id(0)

        @pl.when(i == 0)
        def _():
            pwin[0:HALO_A, :] = jnp.zeros((HALO_A, c), F32)
            ewin[0:HALO_S, :] = jnp.zeros((HALO_S, c), F32)

        @pl.when(i > 0)
        def _():
            pwin[0:HALO_A, :] = pwin[ts:ts + HALO_A, :]
            ewin[0:HALO_S, :] = ewin[ts:ts + HALO_S, :]

        a_val = u_ref[:, 0:c] + bin_ref[:, 0:c]
        a_gate = u_ref[:, c:2 * c] + bin_ref[:, c:2 * c]
        p = a_val * _sigmoid(a_gate)
        pwin[HALO_A:HALO_A + ts, :] = p
        p_ref[...] = p
        e = (u_ref[:, 3 * c:4 * c] + bin_ref[:, 3 * c:4 * c]) * (u_ref[:, 4 * c:5 * c] + bin_ref[:, 4 * c:5 * c])
        ewin[HALO_S:HALO_S + ts, :] = e
        e_ref[...] = e
        _conv_rows(pwin, wa_ref, offs_a, ts, c, _store_into(q_ref), bias_ref=ba_ref)
        q = q_ref[...]
        mu = jnp.mean(q, axis=-1, keepdims=True)
        xc = q - mu
        var = jnp.mean(xc * xc, axis=-1, keepdims=True)
        ln = xc * lax.rsqrt(var + EPS) * lg_ref[...] + lb_ref[...]
        y_ref[:, 0:c] = (ln * _sigmoid(ln)).astype(BF16)
        _conv_rows(ewin, wb_ref, offs_b, ts, c, _store_into(fbuf))
        g_b = u_ref[:, 2 * c:3 * c] + bin_ref[:, 2 * c:3 * c]
        y_ref[:, c:2 * c] = (g_b * fbuf[...]).astype(BF16)

    tile = lambda w: pl.BlockSpec((ts, w), lambda i: (i, 0))
    return _call(
        body, name=name, grid=(s // ts,),
        in_specs=[tile(5 * c), _full((1, 5 * c)), _full((K_A, c)), _full((1, c)), _full((1, c)), _full((1, c)),
                  _full((K_B, c))],
        out_specs=[tile(2 * c), tile(c), tile(c), tile(c)],
        out_shape=[jax.ShapeDtypeStruct((s, 2 * c), BF16)] + [jax.ShapeDtypeStruct((s, c), F32)] * 3,
        scratch_shapes=[pltpu.VMEM((ts + HALO_A, c), F32), pltpu.VMEM((ts + HALO_S, c), F32), pltpu.VMEM((ts, c), F32)],
        args=(u, _row(b_in), wa, _row(ba), _row(lg), _row(lb), wb), hook=hook)


def _ffn_pw_fwd(u2, wf, *, name, hook=None):
    s, f2 = u2.shape
    f = f2 // 2
    ts = _pick(s, (TS_FFN,))
    offs = _causal_offsets(HALO_S, K_F)

    def body(u_ref, wf_ref, z_ref, uwin, cbuf):
        i = pl.program_id(0)

        @pl.when(i == 0)
        def _():
            uwin[0:HALO_S, :] = jnp.zeros((HALO_S, f2), F32)

        @pl.when(i > 0)
        def _():
            uwin[0:HALO_S, :] = uwin[ts:ts + HALO_S, :]

        uwin[HALO_S:HALO_S + ts, :] = u_ref[...]
        _conv_rows(uwin, wf_ref, offs, ts, f2, _store_into(cbuf))
        gate = cbuf[:, 0:f]
        z_ref[...] = (gate * _sigmoid(gate) * cbuf[:, f:f2]).astype(BF16)

    return _call(
        body, name=name, grid=(s // ts,),
        in_specs=[pl.BlockSpec((ts, f2), lambda i: (i, 0)), _full((K_F, f2))],
        out_specs=[pl.BlockSpec((ts, f), lambda i: (i, 0))],
        out_shape=[jax.ShapeDtypeStruct((s, f), BF16)],
        scratch_shapes=[pltpu.VMEM((ts + HALO_S, f2), F32), pltpu.VMEM((ts, f2), F32)],
        args=(u2, wf), hook=hook)


def _loss_bwd(x, tgt, g, *, name):
    s, d = x.shape
    ts = _pick(s, (TS_NORM, 256, 128))

    def body(x_ref, t_ref, g_ref, loss_ref, dx_ref, dxb_ref, dg_ref):
        @pl.when(pl.program_id(0) == 0)
        def _():
            loss_ref[...] = jnp.zeros_like(loss_ref)
            dg_ref[...] = jnp.zeros_like(dg_ref)

        xv = x_ref[...]
        r = lax.rsqrt(jnp.mean(xv * xv, axis=-1, keepdims=True) + EPS)
        xh = xv * r
        err = xh * g_ref[...] - t_ref[...]
        loss_ref[...] += 0.5 * jnp.sum(jnp.mean(err * err, axis=-1, keepdims=True))
        dy = err * (1.0 / d)
        dg_ref[...] += jnp.sum(dy * xh, axis=0, keepdims=True)
        dxh = dy * g_ref[...]
        dx = r * (dxh - xh * jnp.mean(dxh * xh, axis=-1, keepdims=True))
        dx_ref[...] = dx
        dxb_ref[...] = dx.astype(BF16)

    tile = pl.BlockSpec((ts, d), lambda i: (i, 0))
    return pl.pallas_call(
        body, name=name, grid=(s // ts,), in_specs=[tile, tile, _full((1, d))],
        out_specs=[_full((8, LANES)), tile, tile, _full((1, d))],
        out_shape=[jax.ShapeDtypeStruct((8, LANES), F32), jax.ShapeDtypeStruct((s, d), F32),
                   jax.ShapeDtypeStruct((s, d), BF16), jax.ShapeDtypeStruct((1, d), F32)],
        compiler_params=_params())(x, tgt, _row(g))


def _norm_bwd(x, dh, dx_in, g, *, name):
    s, d = x.shape
    ts = _pick(s, (TS_NORM, 256, 128))

    def body(x_ref, dh_ref, dxin_ref, g_ref, dx_ref, dxb_ref, dg_ref):
        @pl.when(pl.program_id(0) == 0)
        def _():
            dg_ref[...] = jnp.zeros_like(dg_ref)

        xv = x_ref[...]
        r = lax.rsqrt(jnp.mean(xv * xv, axis=-1, keepdims=True) + EPS)
        xh = xv * r
        dh_v = dh_ref[...]
        dg_ref[...] += jnp.sum(dh_v * xh, axis=0, keepdims=True)
        dxh = dh_v * g_ref[...]
        dx = dxin_ref[...] + r * (dxh - xh * jnp.mean(dxh * xh, axis=-1, keepdims=True))
        dx_ref[...] = dx
        dxb_ref[...] = dx.astype(BF16)

    tile = pl.BlockSpec((ts, d), lambda i: (i, 0))
    return pl.pallas_call(
        body, name=name, grid=(s // ts,), in_specs=[tile, tile, tile, _full((1, d))],
        out_specs=[tile, tile, _full((1, d))],
        out_shape=[jax.ShapeDtypeStruct((s, d), F32), jax.ShapeDtypeStruct((s, d), BF16),
                   jax.ShapeDtypeStruct((1, d), F32)],
        compiler_params=_params())(x, dh, dx_in, _row(g))


def _ffn_pw_bwd(u2, dz, wf, *, name, hook=None):
    s, f2 = u2.shape
    f = f2 // 2
    ts = _pick(s, (TS_FFN,))
    nt = s // ts
    hb = ts // HALO_S
    offs_c, offs_t = _causal_offsets(HALO_S, K_F), _anticausal_offsets(K_F)

    def body(u_ref, uh_ref, dz_ref, wf_ref, du_ref, dwf_ref, uwin, cbuf, dcwin):
        i = pl.program_id(0)

        @pl.when(i == 0)
        def _():
            dwf_ref[...] = jnp.zeros_like(dwf_ref)
            dcwin[ts:ts + HALO_S, :] = jnp.zeros((HALO_S, f2), F32)

        @pl.when(i > 0)
        def _():
            dcwin[ts:ts + HALO_S, :] = dcwin[0:HALO_S, :]

        @pl.when(i == nt - 1)
        def _():
            uwin[0:HALO_S, :] = jnp.zeros((HALO_S, f2), F32)

        @pl.when(i < nt - 1)
        def _():
            uwin[0:HALO_S, :] = uh_ref[...]

        uwin[HALO_S:HALO_S + ts, :] = u_ref[...]
        _conv_rows(uwin, wf_ref, offs_c, ts, f2, _store_into(cbuf))
        gate = cbuf[:, 0:f]
        sg = _sigmoid(gate)
        dz_v = dz_ref[...]
        dcwin[0:ts, 0:f] = dz_v * cbuf[:, f:f2] * _dsilu(gate, sg)
        dcwin[0:ts, f:f2] = dz_v * gate * sg

        def emit(r0, c0, v):
            du_ref[r0:r0 + CONV_ROWS, c0:c0 + LANES] = v.astype(BF16)
        _conv_rows(dcwin, wf_ref, offs_t, ts, f2, emit)
        _conv_wgrad(dcwin, uwin, offs_c, ts, f2, dwf_ref)

    rev = lambda i: (nt - 1 - i, 0)
    return _call(
        body, name=name, grid=(nt,),
        in_specs=[pl.BlockSpec((ts, f2), rev),
                  pl.BlockSpec((HALO_S, f2), lambda i: (jnp.maximum((nt - 1 - i) * hb - 1, 0), 0)),
                  pl.BlockSpec((ts, f), rev), _full((K_F, f2))],
        out_specs=[pl.BlockSpec((ts, f2), rev), _full((K_F, f2))],
        out_shape=[jax.ShapeDtypeStruct((s, f2), BF16), jax.ShapeDtypeStruct((K_F, f2), F32)],
        scratch_shapes=[pltpu.VMEM((ts + HALO_S, f2), F32), pltpu.VMEM((ts, f2), F32),
                        pltpu.VMEM((ts + HALO_S, f2), F32)],
        args=(u2, u2, dz, wf), hook=hook)


def _mix_pw_bwd(dy, u, p, q, e, b_in, wa, lg, lb, wb, *, name, hook=None):
    s = u.shape[0]
    c = D_CONF
    ts = _pick(s, (TS_MIX, 128))
    nt = s // ts
    offs_ac, offs_at = _causal_offsets(HALO_A, K_A), _anticausal_offsets(K_A)
    offs_bc, offs_bt = _causal_offsets(HALO_S, K_B), _anticausal_offsets(K_B)

    def body(dy_ref, u_ref, p_ref, ph_ref, q_ref, e_ref, eh_ref, bin_ref, wa_ref, lg_ref, lb_ref, wb_ref,
             du_ref, dbin_ref, dwa_ref, dba_ref, dlg_ref, dlb_ref, dwb_ref, pwin, ewin, dqwin, dfwin, buf):
        i = pl.program_id(0)

        @pl.when(i == 0)
        def _():
            for r in (dbin_ref, dwa_ref, dba_ref, dlg_ref, dlb_ref, dwb_ref):
                r[...] = jnp.zeros_like(r)
            dqwin[ts:ts + HALO_A, :] = jnp.zeros((HALO_A, c), F32)
            dfwin[ts:ts + HALO_S, :] = jnp.zeros((HALO_S, c), F32)

        @pl.when(i > 0)
        def _():
            dqwin[ts:ts + HALO_A, :] = dqwin[0:HALO_A, :]
            dfwin[ts:ts + HALO_S, :] = dfwin[0:HALO_S, :]

        @pl.when(i == nt - 1)
        def _():
            pwin[0:HALO_A, :] = jnp.zeros((HALO_A, c), F32)
            ewin[0:HALO_S, :] = jnp.zeros((HALO_S, c), F32)

        @pl.when(i < nt - 1)
        def _():
            pwin[0:HALO_A, :] = ph_ref[...]
            ewin[0:HALO_S, :] = eh_ref[...]

        pwin[HALO_A:HALO_A + ts, :] = p_ref[...]
        ewin[HALO_S:HALO_S + ts, :] = e_ref[...]

        def colsum(v):
            return jnp.sum(v, axis=0, keepdims=True)

        q_v = q_ref[...]
        mu = jnp.mean(q_v, axis=-1, keepdims=True)
        xc = q_v - mu
        rstd = lax.rsqrt(jnp.mean(xc * xc, axis=-1, keepdims=True) + EPS)
        nrm = xc * rstd
        ln = nrm * lg_ref[...] + lb_ref[...]
        dln = dy_ref[:, 0:c] * _dsilu(ln, _sigmoid(ln))
        dlg_ref[...] += colsum(dln * nrm)
        dlb_ref[...] += colsum(dln)
        dn = dln * lg_ref[...]
        dq = rstd * (dn - jnp.mean(dn, axis=-1, keepdims=True) - nrm * jnp.mean(dn * nrm, axis=-1, keepdims=True))
        dba_ref[...] += colsum(dq)
        dqwin[0:ts, :] = dq
        _conv_rows(dqwin, wa_ref, offs_at, ts, c, _store_into(buf))
        _conv_wgrad(dqwin, pwin, offs_ac, ts, c, dwa_ref)
        dp = buf[...]
        a_val = u_ref[:, 0:c] + bin_ref[:, 0:c]
        sg = _sigmoid(u_ref[:, c:2 * c] + bin_ref[:, c:2 * c])
        d_aval = dp * sg
        d_agate = dp * a_val * sg * (1.0 - sg)
        du_ref[:, 0:c] = d_aval.astype(BF16)
        du_ref[:, c:2 * c] = d_agate.astype(BF16)
        dbin_ref[:, 0:c] += colsum(d_aval)
        dbin_ref[:, c:2 * c] += colsum(d_agate)

        ds = dy_ref[:, c:2 * c]
        _conv_rows(ewin, wb_ref, offs_bc, ts, c, _store_into(buf))
        d_gb = ds * buf[...]
        dfwin[0:ts, :] = ds * (u_ref[:, 2 * c:3 * c] + bin_ref[:, 2 * c:3 * c])
        _conv_rows(dfwin, wb_ref, offs_bt, ts, c, _store_into(buf))
        _conv_wgrad(dfwin, ewin, offs_bc, ts, c, dwb_ref)
        de = buf[...]
        d_gc = de * (u_ref[:, 4 * c:5 * c] + bin_ref[:, 4 * c:5 * c])
        d_v = de * (u_ref[:, 3 * c:4 * c] + bin_ref[:, 3 * c:4 * c])
        du_ref[:, 2 * c:3 * c] = d_gb.astype(BF16)
        du_ref[:, 3 * c:4 * c] = d_gc.astype(BF16)
        du_ref[:, 4 * c:5 * c] = d_v.astype(BF16)
        dbin_ref[:, 2 * c:3 * c] += colsum(d_gb)
        dbin_ref[:, 3 * c:4 * c] += colsum(d_gc)
        dbin_ref[:, 4 * c:5 * c] += colsum(d_v)

    rev = lambda i: (nt - 1 - i, 0)
    tile = lambda w: pl.BlockSpec((ts, w), rev)
    halo = lambda h: pl.BlockSpec((h, c), lambda i: (jnp.maximum((nt - 1 - i) * (ts // h) - 1, 0), 0))
    small = [(1, 5 * c), (K_A, c), (1, c), (1, c), (1, c), (K_B, c)]
    return _call(
        body, name=name, grid=(nt,),
        in_specs=[tile(2 * c), tile(5 * c), tile(c), halo(HALO_A), tile(c), tile(c), halo(HALO_S),
                  _full((1, 5 * c)), _full((K_A, c)), _full((1, c)), _full((1, c)), _full((K_B, c))],
        out_specs=[tile(5 * c)] + [_full(sh) for sh in small],
        out_shape=[jax.ShapeDtypeStruct((s, 5 * c), BF16)] + [jax.ShapeDtypeStruct(sh, F32) for sh in small],
        scratch_shapes=[pltpu.VMEM((ts + HALO_A, c), F32), pltpu.VMEM((ts + HALO_S, c), F32),
                        pltpu.VMEM((ts + HALO_A, c), F32), pltpu.VMEM((ts + HALO_S, c), F32),
                        pltpu.VMEM((ts, c), F32)],
        args=(dy, u, p, p, q, e, e, _row(b_in), wa, _row(lg), _row(lb), wb), hook=hook)


def _adamw(g, w, m, v, *, name):
    r, c = g.shape
    tr = _pick(r, (256, 128, 64, 8)) if r > 256 else r

    def body(g_ref, w_ref, m_ref, v_ref, d_ref, nm_ref, nv_ref):
        gv = g_ref[...]
        nm = ADAM_B1 * m_ref[...] + (1.0 - ADAM_B1) * gv
        nv = ADAM_B2 * v_ref[...] + (1.0 - ADAM_B2) * (gv * gv)
        m_hat = nm / (1.0 - ADAM_B1 ** ADAM_STEP)
        v_hat = nv / (1.0 - ADAM_B2 ** ADAM_STEP)
        d_ref[...] = -ADAM_LR * (m_hat / (jnp.sqrt(v_hat) + ADAM_EPS) + ADAM_WD * w_ref[...])
        nm_ref[...] = nm
        nv_ref[...] = nv

    tile = pl.BlockSpec((tr, c), lambda i: (i, 0))
    return pl.pallas_call(body, name=name, grid=(r // tr,), in_specs=[tile] * 4, out_specs=[tile] * 3,
                          out_shape=[jax.ShapeDtypeStruct((r, c), F32)] * 3, compiler_params=_params())(g, w, m, v)


def _cast_place(wshard, chip, *, name):
    l, r2, c = wshard.shape
    tr = r2 // 2

    def body(chip_ref, w_ref, *o_refs):
        for li, o_ref in enumerate(o_refs):
            o_ref[...] = w_ref[li].astype(BF16)

    grid_spec = pltpu.PrefetchScalarGridSpec(
        num_scalar_prefetch=1, grid=(2,),
        in_specs=[pl.BlockSpec((l, tr, c), lambda i, chip_ref: (0, i, 0))],
        out_specs=[pl.BlockSpec((None, tr, c), lambda i, chip_ref: (chip_ref[0], i, 0))] * l)
    return pl.pallas_call(body, name=name, grid_spec=grid_spec,
                          out_shape=[jax.ShapeDtypeStruct((N_CHIPS, r2, c), BF16)] * l, compiler_params=_params())(chip, wshard)


def _pair_sum(g, recv, core, *, name):
    _, _, r, c = g.shape

    def body(core_ref, g_ref, r_ref, o_ref):
        o_ref[...] = (g_ref[...] + r_ref[...]).astype(BF16)

    grid_spec = pltpu.PrefetchScalarGridSpec(
        num_scalar_prefetch=1, grid=(N_CHIPS,),
        in_specs=[pl.BlockSpec((None, None, r, c), lambda i, core_ref: (i, core_ref[0], 0, 0)),
                  pl.BlockSpec((None, r, c), lambda i, core_ref: (i, 0, 0))],
        out_specs=pl.BlockSpec((None, r, c), lambda i, core_ref: (i, 0, 0)))
    return pl.pallas_call(body, name=name, grid_spec=grid_spec,
                          out_shape=jax.ShapeDtypeStruct((N_CHIPS, r, c), BF16), compiler_params=_params())(core, g, recv)


def _chip_sum(own, recv, core_chip, stack, *, name):
    _, r, c = own.shape
    tr = r // 2
    l, layers, buf = stack

    def body(cc_ref, own_ref, r1_ref, r2_ref, r3_ref, *rest):
        o_ref = rest[-1]
        o_ref[...] = ((own_ref[...].astype(F32) + r1_ref[...].astype(F32)) + r2_ref[...].astype(F32)) + r3_ref[...].astype(F32)

    def slot(s):
        return pl.BlockSpec((None, tr, c), lambda i, cc_ref: ((cc_ref[1] + s) % N_CHIPS, i, 0))

    in_specs = [slot(0), slot(1), slot(2), slot(3)]
    args = [own, recv, recv, recv]
    aliases = {}
    if buf is not None:
        aliases = {1 + len(args): 0}
        in_specs.append(ANY)
        args.append(buf)
    grid_spec = pltpu.PrefetchScalarGridSpec(
        num_scalar_prefetch=1, grid=(2,), in_specs=in_specs,
        out_specs=pl.BlockSpec((None, None, tr, c), lambda i, cc_ref: (l, cc_ref[0], i, 0)))
    return pl.pallas_call(body, name=name, grid_spec=grid_spec, input_output_aliases=aliases,
                          out_shape=jax.ShapeDtypeStruct((layers, 2, r, c), F32), compiler_params=_params())(core_chip, *args)


def _place():
    x, y, c = lax.axis_index("x"), lax.axis_index("y"), lax.axis_index("c")
    chips = [(1 - x, y), (x, 1 - y), (1 - x, 1 - y)]
    return x, y, c, 2 * x + y, chips


def _remote(src, dst, send_sems, recv_sems, k, dev):
    return pltpu.make_async_remote_copy(src_ref=src, dst_ref=dst, send_sem=send_sems.at[k], recv_sem=recv_sems.at[k],
                                        device_id=dev, device_id_type=MESH)


def _sibling_halves(grads, *, name):
    n = len(grads)
    out_shape = [jax.ShapeDtypeStruct(g.shape[:1] + g.shape[2:], g.dtype) for g in grads]

    def body(*refs):
        ins, outs = refs[:n], refs[n:2 * n]
        send_sems, recv_sems = refs[2 * n:]
        x, y, c, _, _ = _place()
        cps = [_remote(ins[a].at[:, 1 - c], outs[a], send_sems, recv_sems, a, (x, y, 1 - c)) for a in range(n)]
        for cp in cps:
            cp.start()
        for cp in cps:
            cp.wait()

    return pl.pallas_call(body, name=name, in_specs=[ANY] * n, out_specs=[ANY] * n, out_shape=out_shape,
                          scratch_shapes=[pltpu.SemaphoreType.DMA((n,)), pltpu.SemaphoreType.DMA((n,))])(*grads)


class _Gather:
    def __init__(self, bufs):
        self.operands = list(bufs)
        self.out_shape = [jax.ShapeDtypeStruct(b.shape, b.dtype) for b in bufs]
        self.aliases = {a: a for a in range(len(bufs))}

    def sems(self):
        n = 6 * len(self.operands)
        return [pltpu.SemaphoreType.DMA((n,)), pltpu.SemaphoreType.DMA((n,))]

    def _sends(self, ins, outs, sems):
        x, y, c, j, chips = _place()
        return [_remote(ins[a].at[j, c], outs[a].at[j, c], sems[0], sems[1], 6 * a + k, (cx, cy, c))
                for a in range(len(ins)) for k, (cx, cy) in enumerate(chips)]

    def start(self, ins, outs, sems):
        for cp in self._sends(ins, outs, sems):
            cp.start()

    def finish(self, ins, outs, sems):
        x, y, c, j, chips = _place()
        sib = (x, y, 1 - c)
        passed = []
        for a in range(len(ins)):
            for k, (cx, cy) in enumerate(chips):
                blk = outs[a].at[2 * cx + cy, c]
                _remote(blk, blk, sems[0], sems[1], 6 * a + k, (cx, cy, c)).wait_recv()
                cp = _remote(blk, blk, sems[0], sems[1], 6 * a + 3 + k, sib)
                cp.start()
                passed.append(cp)
        for a in range(len(ins)):
            for k, (cx, cy) in enumerate(chips):
                blk = outs[a].at[2 * cx + cy, 1 - c]
                _remote(blk, blk, sems[0], sems[1], 6 * a + 3 + k, sib).wait_recv()
        for cp in self._sends(ins, outs, sems) + passed:
            cp.wait_send()


class _Scatter:
    def __init__(self, srcs):
        self.operands = list(srcs)
        self.out_shape = [jax.ShapeDtypeStruct(s.shape, s.dtype) for s in srcs]
        self.aliases = {}

    def sems(self):
        n = 3 * len(self.operands)
        return [pltpu.SemaphoreType.DMA((n,)), pltpu.SemaphoreType.DMA((n,))]

    def _sends(self, ins, outs, sems):
        x, y, c, j, chips = _place()
        return [_remote(ins[a].at[2 * cx + cy], outs[a].at[j], sems[0], sems[1], 3 * a + k, (cx, cy, c))
                for a in range(len(ins)) for k, (cx, cy) in enumerate(chips)]

    def start(self, ins, outs, sems):
        for cp in self._sends(ins, outs, sems):
            cp.start()

    def finish(self, ins, outs, sems):
        x, y, c, j, chips = _place()
        for a in range(len(ins)):
            for k, (cx, cy) in enumerate(chips):
                blk = outs[a].at[2 * cx + cy]
                _remote(blk, blk, sems[0], sems[1], 3 * a + k, (cx, cy, c)).wait_recv()
        for cp in self._sends(ins, outs, sems):
            cp.wait_send()


def _standalone(hook, *, name):
    n_in, n_out = len(hook.operands), len(hook.out_shape)

    def body(*refs):
        ins, outs, sems = refs[:n_in], refs[n_in:n_in + n_out], refs[n_in + n_out:]
        hook.start(ins, outs, sems)
        hook.finish(ins, outs, sems)

    return list(pl.pallas_call(body, name=name, in_specs=[ANY] * n_in, out_specs=[ANY] * n_out,
                               out_shape=hook.out_shape, input_output_aliases=hook.aliases,
                               scratch_shapes=hook.sems())(*hook.operands))


def _share_halves(bufs, *, name):
    n = len(bufs)

    def body(*refs):
        ins, outs = refs[:n], refs[n:2 * n]
        send_sems, recv_sems = refs[2 * n:]
        x, y, c, _, _ = _place()
        sends = [_remote(ins[a].at[:, c], outs[a].at[:, c], send_sems, recv_sems, a, (x, y, 1 - c)) for a in range(n)]
        for cp in sends:
            cp.start()
        for a in range(n):
            blk = outs[a].at[:, 1 - c]
            _remote(blk, blk, send_sems, recv_sems, a, (x, y, 1 - c)).wait_recv()
        for cp in sends:
            cp.wait_send()

    return pl.pallas_call(
        body, name=name, in_specs=[ANY] * n, out_specs=[ANY] * n,
        out_shape=[jax.ShapeDtypeStruct(b.shape, b.dtype) for b in bufs],
        input_output_aliases={a: a for a in range(n)},
        scratch_shapes=[pltpu.SemaphoreType.DMA((n,)), pltpu.SemaphoreType.DMA((n,))])(*bufs)


def _allreduce_small(v, *, name):
    r, c = v.shape
    n_dev = 8

    def body(v_ref, o_ref, buf, send_sems, recv_sems):
        x, y, cc = lax.axis_index("x"), lax.axis_index("y"), lax.axis_index("c")
        me = 4 * x + 2 * y + cc
        cps = []
        for d in range(1, n_dev):
            peer = (1 - x if d & 4 else x, 1 - y if d & 2 else y, 1 - cc if d & 1 else cc)
            cps.append(_remote(v_ref, buf.at[me], send_sems, recv_sems, d - 1, peer))
        for cp in cps:
            cp.start()
        buf[me] = v_ref[...]
        for cp in cps:
            cp.wait()
        acc = buf[0]
        for d in range(1, n_dev):
            acc = acc + buf[d]
        o_ref[...] = acc

    return pl.pallas_call(
        body, name=name, in_specs=[VMEM], out_specs=VMEM, out_shape=jax.ShapeDtypeStruct((r, c), F32),
        scratch_shapes=[pltpu.VMEM((n_dev, r, c), F32), pltpu.SemaphoreType.DMA((n_dev - 1,)),
                        pltpu.SemaphoreType.DMA((n_dev - 1,))])(v)


def _pack(arrays, row_multiple=8):
    flat = jnp.concatenate([a.reshape(-1) for a in arrays])
    rows = -(-flat.shape[0] // LANES)
    rows = -(-rows // row_multiple) * row_multiple
    return jnp.pad(flat, (0, rows * LANES - flat.shape[0])).reshape(rows, LANES)


def _unpack(slab, shapes):
    flat = slab.reshape(-1)
    out, pos = [], 0
    for sh in shapes:
        size = 1
        for dim in sh:
            size *= dim
        out.append(flat[pos:pos + size].reshape(sh))
        pos += size
    return out


def _halves(w):
    l, r, c = w.shape
    return w.reshape(l, 2, r // 2, c)


def kernel(x, mix_norm_g, w_in, b_in, conv_a_w, conv_a_b, ln_a_g, ln_a_b, conv_b_w, w_out, ffn_norm_g, w_up, conv_f_w, w_down, final_norm_g, loss_target, m_mix_norm_g, m_w_in, m_b_in, m_conv_a_w, m_conv_a_b, m_ln_a_g, m_ln_a_b, m_conv_b_w, m_w_out, m_ffn_norm_g, m_w_up, m_conv_f_w, m_w_down, m_final_norm_g, v_mix_norm_g, v_w_in, v_b_in, v_conv_a_w, v_conv_a_b, v_ln_a_g, v_ln_a_b, v_conv_b_w, v_w_out, v_ffn_norm_g, v_w_up, v_conv_f_w, v_w_down, v_final_norm_g):
    w = dict(mix_norm_g=mix_norm_g, w_in=w_in, b_in=b_in, conv_a_w=conv_a_w, conv_a_b=conv_a_b, ln_a_g=ln_a_g,
             ln_a_b=ln_a_b, conv_b_w=conv_b_w, w_out=w_out, ffn_norm_g=ffn_norm_g, w_up=w_up, conv_f_w=conv_f_w,
             w_down=w_down, final_norm_g=final_norm_g)
    m = dict(mix_norm_g=m_mix_norm_g, w_in=m_w_in, b_in=m_b_in, conv_a_w=m_conv_a_w, conv_a_b=m_conv_a_b,
             ln_a_g=m_ln_a_g, ln_a_b=m_ln_a_b, conv_b_w=m_conv_b_w, w_out=m_w_out, ffn_norm_g=m_ffn_norm_g,
             w_up=m_w_up, conv_f_w=m_conv_f_w, w_down=m_w_down, final_norm_g=m_final_norm_g)
    v = dict(mix_norm_g=v_mix_norm_g, w_in=v_w_in, b_in=v_b_in, conv_a_w=v_conv_a_w, conv_a_b=v_conv_a_b,
             ln_a_g=v_ln_a_g, ln_a_b=v_ln_a_b, conv_b_w=v_conv_b_w, w_out=v_w_out, ffn_norm_g=v_ffn_norm_g,
             w_up=v_w_up, conv_f_w=v_conv_f_w, w_down=v_w_down, final_norm_g=v_final_norm_g)
    depth = w_in.shape[0]
    xs = x[0]
    tgt = loss_target[0]
    chip = (2 * lax.axis_index("x") + lax.axis_index("y")).astype(jnp.int32)
    chip_arr = chip.reshape(1)
    core = lax.axis_index("c").astype(jnp.int32).reshape(1)
    core_chip = jnp.concatenate([core, chip_arr])

    def halves(b):
        return b.reshape(N_CHIPS, 2, b.shape[1] // 2, b.shape[2])

    conv_shapes = [w[nm].shape for nm in CHANNEL_SHARDED]
    conv_slab = _pack([w[nm] for nm in CHANNEL_SHARDED], row_multiple=16)
    zero = jnp.int32(0)
    slab_buf = lax.dynamic_update_slice(jnp.zeros((N_CHIPS,) + conv_slab.shape, F32), conv_slab[None], (chip, zero, zero))
    wbuf = {}
    for nm in BIG:
        for l, b in enumerate(_cast_place(w[nm], chip_arr, name=f"cast_{nm}")):
            wbuf[nm, l] = halves(b)

    def gather_into(keys, done):
        for key, g in zip(keys, done):
            wbuf[key] = g

    def wmat(nm, l):
        g = wbuf[nm, l]
        if nm in ("w_in", "w_up"):
            return g.reshape(N_CHIPS, 2 * g.shape[2], g.shape[3])
        return g.reshape(N_CHIPS * 2 * g.shape[2], g.shape[3])

    first = [("w_in", 0), ("w_out", 0)]
    done = _standalone(_Gather([wbuf[k] for k in first] + [halves(slab_buf)]), name="gather_first")
    gather_into(first, done)
    conv_full = done[-1].reshape(N_CHIPS, -1, LANES)
    taps = {}
    for nm, parts in zip(CHANNEL_SHARDED, zip(*[_unpack(conv_full[jj], conv_shapes) for jj in range(N_CHIPS)])):
        taps[nm] = jnp.concatenate(parts, axis=-1)

    saved = []
    cur = xs
    h1 = _norm_fwd(cur, w["mix_norm_g"][0], name="norm_mix_0")
    for l in range(depth):
        u = _mm(h1, wmat("w_in", l), mode="nn", b_sharded=True, name=f"mm_in_{l}")
        keys = [("w_up", l), ("w_down", l)] if l == 0 else [("w_up", l)]
        (y, p, q, e), done = _mix_pw_fwd(u, w["b_in"][l], taps["conv_a_w"][l], w["conv_a_b"][l], w["ln_a_g"][l],
                                         w["ln_a_b"][l], taps["conv_b_w"][l], name=f"mix_fwd_{l}",
                                         hook=_Gather([wbuf[k] for k in keys]))
        gather_into(keys, done)
        x1, h2 = _mm_res_norm(y, wmat("w_out", l), cur, w["ffn_norm_g"][l], name=f"mm_out_{l}")
        u2 = _mm(h2, wmat("w_up", l), mode="nn", b_sharded=True, name=f"mm_up_{l}")
        keys = [("w_in", l + 1), ("w_out", l + 1), ("w_down", l + 1)] if l + 1 < depth else []
        (z,), done = _ffn_pw_fwd(u2, taps["conv_f_w"][l], name=f"ffn_fwd_{l}",
                                 hook=_Gather([wbuf[k] for k in keys]) if keys else None)
        gather_into(keys, done)
        saved.append((cur, h1, u, y, p, q, e, x1, h2, u2, z))
        if l + 1 < depth:
            cur, h1 = _mm_res_norm(z, wmat("w_down", l), x1, w["mix_norm_g"][l + 1], name=f"mm_down_{l}")
        else:
            cur = _mm(z, wmat("w_down", l), mode="nn", res=x1, name=f"mm_down_{l}")

    loss_blk, dx, dxb, dgf = _loss_bwd(cur, tgt, w["final_norm_g"], name="loss_bwd")
    loss = lax.psum(loss_blk[0, 0], ("x", "y", "c"))

    gfull, chip_sums, from_chips = {}, {}, {}

    def four_d(g):
        if g.ndim == 2:
            g = g.reshape(N_CHIPS, g.shape[0] // N_CHIPS, g.shape[1])
        return halves(g)

    def scatter_hook(keys, tag):
        gl = [four_d(gfull[k]) for k in keys]
        from_sibling = _sibling_halves(gl, name=f"reduce_sibling_{tag}")
        for k, g, r in zip(keys, gl, from_sibling):
            chip_sums[k] = _pair_sum(g, r, core, name=f"pair_sum_{k[0]}_{k[1]}")
        return _Scatter([chip_sums[k] for k in keys])

    sg = {nm: [None] * depth for nm in SMALL if nm != "final_norm_g"}
    pending = []
    for l in reversed(range(depth)):
        x0, h1, u, y, p, q, e, x1, h2, u2, z = saved[l]
        dz = _mm(dxb, wmat("w_down", l), mode="nt", name=f"mm_ddown_{l}")
        gfull["w_down", l] = _mm(z, dxb, mode="tn", name=f"mm_gdown_{l}")
        keys = pending + [("w_down", l)]
        (du2, dwf), done = _ffn_pw_bwd(u2, dz, taps["conv_f_w"][l], name=f"ffn_bwd_{l}", hook=scatter_hook(keys, f"a{l}"))
        from_chips.update(zip(keys, done))
        dx1, dx1b, dg2 = _mm_norm_bwd(du2, wmat("w_up", l), x1, dx, w["ffn_norm_g"][l], name=f"mm_dup_{l}")
        gfull["w_up", l] = _mm(h2, du2, mode="tn", out_sharded=True, name=f"mm_gup_{l}")
        dy = _mm(dx1b, wmat("w_out", l), mode="nt", name=f"mm_dout_{l}")
        gfull["w_out", l] = _mm(y, dx1b, mode="tn", name=f"mm_gout_{l}")
        keys = [("w_up", l), ("w_out", l)]
        (du, dbin, dwa, dba, dlg, dlb, dwb), done = _mix_pw_bwd(
            dy, u, p, q, e, w["b_in"][l], taps["conv_a_w"][l], w["ln_a_g"][l], w["ln_a_b"][l],
            taps["conv_b_w"][l], name=f"mix_bwd_{l}", hook=scatter_hook(keys, f"b{l}"))
        from_chips.update(zip(keys, done))
        dx, dxb, dg1 = _mm_norm_bwd(du, wmat("w_in", l), x0, dx1, w["mix_norm_g"][l], name=f"mm_din_{l}")
        gfull["w_in", l] = _mm(h1, du, mode="tn", out_sharded=True, name=f"mm_gin_{l}")
        pending = [("w_in", l)]
        for nm, g in (("mix_norm_g", dg1), ("b_in", dbin), ("conv_a_w", dwa), ("conv_a_b", dba), ("ln_a_g", dlg),
                      ("ln_a_b", dlb), ("conv_b_w", dwb), ("ffn_norm_g", dg2), ("conv_f_w", dwf)):
            sg[nm][l] = g.reshape(g.shape[-1]) if g.shape[0] == 1 else g
    grad_x = dx[None]
    from_chips.update(zip(pending, _standalone(scatter_hook(pending, "last"), name="reduce_chips_last")))

    mine = []
    for nm in BIG:
        buf = None
        for l in range(depth):
            buf = _chip_sum(chip_sums[nm, l], from_chips[nm, l], core_chip, (l, depth, buf), name=f"chip_sum_{nm}_{l}")
        mine.append(buf)
    shared = _share_halves(mine, name="share_halves")
    grads = {}
    for nm, g in zip(BIG, shared):
        grads[nm] = g.reshape(w[nm].shape)

    small_full_shapes = []
    small_parts = []
    for nm in SMALL:
        if nm == "final_norm_g":
            g = dgf.reshape(-1)
        else:
            g = jnp.stack(sg[nm])
        small_parts.append(g)
        small_full_shapes.append(g.shape)
    summed = _unpack(_allreduce_small(_pack(small_parts), name="reduce_small"), small_full_shapes)
    for nm, g in zip(SMALL, summed):
        if nm in CHANNEL_SHARDED:
            width = w[nm].shape[-1]
            g = lax.dynamic_slice_in_dim(g, chip * width, width, axis=2)
        grads[nm] = g

    delta, new_m, new_v = {}, {}, {}
    for nm in BIG:
        sh = w[nm].shape
        two_d = lambda a: a.reshape(sh[0] * sh[1], sh[2])
        d_, m_, v_ = _adamw(two_d(grads[nm]), two_d(w[nm]), two_d(m[nm]), two_d(v[nm]), name=f"adamw_{nm}")
        delta[nm], new_m[nm], new_v[nm] = d_.reshape(sh), m_.reshape(sh), v_.reshape(sh)
    small_shapes = [w[nm].shape for nm in SMALL]
    outs = _adamw(_pack([grads[nm] for nm in SMALL]), _pack([w[nm] for nm in SMALL]), _pack([m[nm] for nm in SMALL]),
                  _pack([v[nm] for nm in SMALL]), name="adamw_small")
    for store, slab in zip((delta, new_m, new_v), outs):
        for nm, a in zip(SMALL, _unpack(slab, small_shapes)):
            store[nm] = a

    return (loss, grad_x, *[grads[nm] for nm in TWIN_WEIGHTS], *[delta[nm] for nm in TWIN_WEIGHTS],
            *[new_m[nm] for nm in TWIN_WEIGHTS], *[new_v[nm] for nm in TWIN_WEIGHTS])
```

```python
import functools

import jax
import jax.numpy as jnp
from jax import lax
from jax.experimental import pallas as pl
from jax.experimental.pallas import tpu as pltpu

F32 = jnp.float32
BF16 = jnp.bfloat16
MESH = pl.DeviceIdType.MESH
ANY = pl.BlockSpec(memory_space=pl.ANY)
VMEM = pl.BlockSpec(memory_space=pltpu.VMEM)

EPS = 1e-6
ADAM_LR, ADAM_B1, ADAM_B2, ADAM_EPS, ADAM_WD, ADAM_STEP = 0.001, 0.9, 0.999, 1e-08, 0.01, 10

D_CONF = 512
K_A, K_B, K_F = 31, 3, 3
HALO_A, HALO_S = 32, 8
N_CHIPS = 4
LANES = 128
SUBLANES = 8
CONV_ROWS = 128
TS_MIX, TS_FFN, TS_NORM = 256, 128, 512
VMEM_LIMIT = 48 * 1024 * 1024
VMEM_LIMIT_WIDE = 56 * 1024 * 1024

TWIN_WEIGHTS = ['mix_norm_g', 'w_in', 'b_in', 'conv_a_w', 'conv_a_b', 'ln_a_g', 'ln_a_b', 'conv_b_w', 'w_out',
                'ffn_norm_g', 'w_up', 'conv_f_w', 'w_down', 'final_norm_g']
BIG = ['w_in', 'w_out', 'w_up', 'w_down']
SMALL = [n for n in TWIN_WEIGHTS if n not in BIG]
CHANNEL_SHARDED = ['conv_a_w', 'conv_b_w', 'conv_f_w']


def _params(n_grid=1):
    return pltpu.CompilerParams(dimension_semantics=("arbitrary",) * n_grid, vmem_limit_bytes=VMEM_LIMIT)


def _pick(n, cands):
    for c in cands:
        if n % c == 0:
            return c
    raise ValueError(f"no tile for {n}")


_DN = {"nn": (((1,), (0,)), ((), ())), "nt": (((1,), (1,)), ((), ())), "tn": (((0,), (0,)), ((), ()))}


def _mm(a, b, *, mode, name, layer=None, b_sharded=False, out_sharded=False, res=None, stack=None, out_dtype=F32,
        also_bf16=False):
    dn = _DN[mode]
    lead = () if layer is None else (None,)
    pre = () if layer is None else (layer,)
    wide = (1408, 1280, 1024, 512)
    if mode == "tn":
        k, m = a.shape
        n = b.shape[1]
        tm = _pick(m, (512, 256))
        tn = n // N_CHIPS if out_sharded else _pick(n, wide)
    else:
        m, k = a.shape
        tm = _pick(m, (1024, 512, 256, 128)) if k <= 1024 else _pick(m, (512, 256, 128))
        if mode == "nn":
            n = b.shape[-1] * N_CHIPS if b_sharded else b.shape[-1]
            tn = b.shape[-1] if b_sharded else _pick(n, wide)
        else:
            n = b.shape[-2]
            tn = _pick(n, (1024, 512, 256)) if b_sharded else _pick(n, wide)
    n_outer = (n // tn - 1) * a.size < (m // tm - 1) * (b.size if layer is None else b.size // b.shape[0])

    def im(f):
        return (lambda g0, g1: f(g1, g0)) if n_outer else f

    if mode == "tn":
        a_spec = pl.BlockSpec((k, tm), im(lambda i, j: (0, i)))
        b_spec = pl.BlockSpec((k, tn), im(lambda i, j: (0, j)))
    else:
        a_spec = pl.BlockSpec((tm, k), im(lambda i, j: (i, 0)))
        if mode == "nn" and b_sharded:
            b_spec = pl.BlockSpec(lead + (None, k, tn), im(lambda i, j: pre + (j, 0, 0)))
        elif mode == "nn":
            b_spec = pl.BlockSpec(lead + (k, tn), im(lambda i, j: pre + (0, j)))
        elif b_sharded:
            b_spec = pl.BlockSpec(lead + (N_CHIPS, tn, b.shape[-1]), im(lambda i, j: pre + (0, j, 0)))
        else:
            b_spec = pl.BlockSpec(lead + (tn, k), im(lambda i, j: pre + (j, 0)))
    slot_lead, slot_pre, slot_shape = (), (), ()
    if stack is not None:
        slot_lead, slot_pre, slot_shape = (None,), (stack[0],), (stack[1],)
    if out_sharded:
        out_shape = jax.ShapeDtypeStruct(slot_shape + (N_CHIPS, m, tn), out_dtype)
        o_spec = pl.BlockSpec(slot_lead + (None, tm, tn), im(lambda i, j: slot_pre + (j, i, 0)))
    else:
        out_shape = jax.ShapeDtypeStruct(slot_shape + (m, n), out_dtype)
        o_spec = pl.BlockSpec(slot_lead + (tm, tn), im(lambda i, j: slot_pre + (i, j)))
    in_specs = [a_spec, b_spec]
    args = [a, b]
    aliases = {}
    if res is not None:
        in_specs.append(pl.BlockSpec((tm, tn), im(lambda i, j: (i, j))))
        args.append(res)
    if stack is not None and stack[2] is not None:
        aliases = {len(args): 0}
        in_specs.append(ANY)
        args.append(stack[2])
    n_in = len(args)

    def body(*refs):
        a_ref, b_ref, o_ref = refs[0], refs[1], refs[n_in]
        if mode == "nt" and b_sharded:
            ks_ = b_ref.shape[2]
            acc = None
            for s in range(N_CHIPS):
                part = lax.dot_general(a_ref[:, s * ks_:(s + 1) * ks_], b_ref[s], dn, preferred_element_type=F32)
                acc = part if acc is None else acc + part
        else:
            acc = lax.dot_general(a_ref[...], b_ref[...], dn, preferred_element_type=F32)
        if res is not None:
            acc = acc + refs[2][...]
        o_ref[...] = acc.astype(o_ref.dtype)
        if also_bf16:
            refs[n_in + 1][...] = acc.astype(BF16)

    grid = (n // tn, m // tm) if n_outer else (m // tm, n // tn)
    if also_bf16:
        o_spec = [o_spec, o_spec]
        out_shape = [out_shape, jax.ShapeDtypeStruct(out_shape.shape, BF16)]
    return pl.pallas_call(body, name=name, grid=grid, in_specs=in_specs, out_specs=o_spec,
                          out_shape=out_shape, input_output_aliases=aliases, compiler_params=_params(2))(*args)


def _mm_res_norm(a, b, res, g, *, name):
    m, k = a.shape
    n = b.shape[1]
    tm = _pick(m, (1024, 512)) if k <= 1024 else _pick(m, (512, 256))

    def body(a_ref, b_ref, r_ref, g_ref, x_ref, h_ref):
        xv = r_ref[...] + lax.dot_general(a_ref[...], b_ref[...], _DN["nn"], preferred_element_type=F32)
        x_ref[...] = xv
        r = lax.rsqrt(jnp.mean(xv * xv, axis=-1, keepdims=True) + EPS)
        h_ref[...] = (xv * r * g_ref[...]).astype(BF16)

    rows = lambda w: pl.BlockSpec((tm, w), lambda i: (i, 0))
    return pl.pallas_call(
        body, name=name, grid=(m // tm,), in_specs=[rows(k), _full((k, n)), rows(n), _full((1, n))],
        out_specs=[rows(n), rows(n)],
        out_shape=[jax.ShapeDtypeStruct((m, n), F32), jax.ShapeDtypeStruct((m, n), BF16)],
        compiler_params=_params())(a, b, res, _row(g))


def _mm_norm_bwd(a, b4, x, dx_in, g, *, name):
    m, k = a.shape
    n, ks = b4.shape[1], b4.shape[2]
    tm = _pick(m, (512, 256))

    def body(a_ref, b_ref, x_ref, dxin_ref, g_ref, dx_ref, dxb_ref, dg_ref):
        @pl.when(pl.program_id(0) == 0)
        def _():
            dg_ref[...] = jnp.zeros_like(dg_ref)

        dh = None
        for s in range(N_CHIPS):
            part = lax.dot_general(a_ref[:, s * ks:(s + 1) * ks], b_ref[s], _DN["nt"], preferred_element_type=F32)
            dh = part if dh is None else dh + part
        xv = x_ref[...]
        r = lax.rsqrt(jnp.mean(xv * xv, axis=-1, keepdims=True) + EPS)
        xh = xv * r
        dg_ref[...] += jnp.sum(dh * xh, axis=0, keepdims=True)
        dxh = dh * g_ref[...]
        dx = dxin_ref[...] + r * (dxh - xh * jnp.mean(dxh * xh, axis=-1, keepdims=True))
        dx_ref[...] = dx
        dxb_ref[...] = dx.astype(BF16)

    rows = lambda w: pl.BlockSpec((tm, w), lambda i: (i, 0))
    weights = pl.BlockSpec((N_CHIPS, n, ks), lambda i: (0, 0, 0), pipeline_mode=pl.Buffered(1))
    return pl.pallas_call(
        body, name=name, grid=(m // tm,), in_specs=[rows(k), weights, rows(n), rows(n), _full((1, n))],
        out_specs=[rows(n), rows(n), _full((1, n))],
        out_shape=[jax.ShapeDtypeStruct((m, n), F32), jax.ShapeDtypeStruct((m, n), BF16),
                   jax.ShapeDtypeStruct((1, n), F32)],
        compiler_params=pltpu.CompilerParams(dimension_semantics=("arbitrary",), vmem_limit_bytes=VMEM_LIMIT_WIDE))(
            a, b4, x, dx_in, _row(g))


def _phases(offsets):
    return sorted({off % SUBLANES for off in offsets} - {0})


def _phase_rows(offsets, rows):
    return rows + max(off - off % SUBLANES for off in offsets if off % SUBLANES)


def _build_phase_copies(win_ref, copies, n_rows, cols):
    for r, ref in copies.items():
        for c0 in range(0, cols, LANES):
            for r0 in range(0, n_rows, CONV_ROWS):
                n = min(CONV_ROWS, n_rows - r0)
                ref[r0:r0 + n, c0:c0 + LANES] = win_ref[r + r0:r + r0 + n, c0:c0 + LANES]


def _tap(win_ref, copies, off):
    r = off % SUBLANES
    return (win_ref, off) if r == 0 or copies is None else (copies[r], off - r)


def _conv_rows(win_ref, w_ref, offsets, rows, cols, emit, bias_ref=None, copies=None):
    taps = [_tap(win_ref, copies, off) for off in offsets]
    for c0 in range(0, cols, LANES):
        cs = slice(c0, c0 + LANES)
        wk = [w_ref[k:k + 1, cs] for k in range(len(offsets))]
        for r0 in range(0, rows, CONV_ROWS):
            acc = None
            for k, (src, off) in enumerate(taps):
                term = wk[k] * src[off + r0:off + r0 + CONV_ROWS, cs]
                acc = term if acc is None else acc + term
            if bias_ref is not None:
                acc = acc + bias_ref[:, cs]
            emit(r0, c0, acc)


def _conv_wgrad(g_ref, x_ref, offsets, rows, cols, dw_ref, copies=None):
    taps = [_tap(x_ref, copies, off) for off in offsets]
    for c0 in range(0, cols, LANES):
        cs = slice(c0, c0 + LANES)
        for k, (src, off) in enumerate(taps):
            acc = None
            for r0 in range(0, rows, CONV_ROWS):
                term = g_ref[r0:r0 + CONV_ROWS, cs] * src[off + r0:off + r0 + CONV_ROWS, cs]
                acc = term if acc is None else acc + term
            dw_ref[k:k + 1, cs] += jnp.sum(acc, axis=0, keepdims=True)


def _store_into(ref):
    def emit(r0, c0, v):
        ref[r0:r0 + CONV_ROWS, c0:c0 + LANES] = v
    return emit


def _causal_offsets(halo, k):
    return [halo - (k - 1) + j for j in range(k)]


def _anticausal_offsets(k):
    return [(k - 1) - j for j in range(k)]


def _row(v):
    return v.reshape(1, -1)


def _full(shape):
    return pl.BlockSpec(shape, lambda i: (0,) * len(shape))


def _sigmoid(v):
    return jax.nn.sigmoid(v)


def _dsilu(v, sg):
    return sg * (1.0 + v * (1.0 - sg))


def _norm_fwd(x, g, *, name):
    s, d = x.shape
    ts = _pick(s, (TS_NORM, 256, 128))

    def body(x_ref, g_ref, h_ref):
        xv = x_ref[...]
        r = lax.rsqrt(jnp.mean(xv * xv, axis=-1, keepdims=True) + EPS)
        h_ref[...] = (xv * r * g_ref[...]).astype(BF16)

    return pl.pallas_call(body, name=name, grid=(s // ts,),
                          in_specs=[pl.BlockSpec((ts, d), lambda i: (i, 0)), _full((1, d))],
                          out_specs=pl.BlockSpec((ts, d), lambda i: (i, 0)),
                          out_shape=jax.ShapeDtypeStruct((s, d), BF16), compiler_params=_params())(x, _row(g))


def _call(body, *, name, grid, in_specs, out_specs, out_shape, args, scratch_shapes=(), hook=None):
    n_in, n_out, n_scr = len(in_specs), len(out_specs), len(scratch_shapes)
    if hook is None:
        outs = pl.pallas_call(body, name=name, grid=grid, in_specs=in_specs, out_specs=out_specs, out_shape=out_shape,
                              scratch_shapes=list(scratch_shapes), compiler_params=_params(len(grid)))(*args)
        return list(outs), []
    h_in, h_out = len(hook.operands), len(hook.out_shape)
    last = grid[0] - 1

    def hosted(*refs):
        ins, refs = refs[:n_in], refs[n_in:]
        h_ins, refs = refs[:h_in], refs[h_in:]
        outs, refs = refs[:n_out], refs[n_out:]
        h_outs, refs = refs[:h_out], refs[h_out:]
        scr, sems = refs[:n_scr], refs[n_scr:]

        @pl.when(pl.program_id(0) == 0)
        def _():
            hook.start(h_ins, h_outs, sems)

        body(*ins, *outs, *scr)

        @pl.when(pl.program_id(0) == last)
        def _():
            hook.finish(h_ins, h_outs, sems)

    outs = pl.pallas_call(
        hosted, name=name, grid=grid, in_specs=list(in_specs) + [ANY] * h_in, out_specs=list(out_specs) + [ANY] * h_out,
        out_shape=list(out_shape) + list(hook.out_shape), scratch_shapes=list(scratch_shapes) + hook.sems(),
        input_output_aliases={n_in + k: n_out + v for k, v in hook.aliases.items()},
        compiler_params=_params(len(grid)))(*args, *hook.operands)
    return list(outs[:n_out]), list(outs[n_out:])


def _mix_pw_fwd(u, b_in, wa, ba, lg, lb, wb, *, name, hook=None):
    s = u.shape[0]
    c = D_CONF
    ts = _pick(s, (TS_MIX, 128))
    offs_a, offs_b = _causal_offsets(HALO_A, K_A), _causal_offsets(HALO_S, K_B)
    ph_a, ph_rows = _phases(offs_a), _phase_rows(offs_a, ts)

    def body(u_ref, bin_ref, wa_ref, ba_ref, lg_ref, lb_ref, wb_ref, y_ref, p_ref, q_ref, e_ref, pwin, ewin, fbuf, pcop):
        i = pl.program_id(0)
        p_copies = {r: pcop.at[n] for n, r in enumerate(ph_a)}

        @pl.when(i == 0)
        def _():
            pwin[0:HALO_A, :] = jnp.zeros((HALO_A, c), F32)
            ewin[0:HALO_S, :] = jnp.zeros((HALO_S, c), F32)

        @pl.when(i > 0)
        def _():
            pwin[0:HALO_A, :] = pwin[ts:ts + HALO_A, :]
            ewin[0:HALO_S, :] = ewin[ts:ts + HALO_S, :]

        a_val = u_ref[:, 0:c] + bin_ref[:, 0:c]
        a_gate = u_ref[:, c:2 * c] + bin_ref[:, c:2 * c]
        p = a_val * _sigmoid(a_gate)
        pwin[HALO_A:HALO_A + ts, :] = p
        p_ref[...] = p
        e = (u_ref[:, 3 * c:4 * c] + bin_ref[:, 3 * c:4 * c]) * (u_ref[:, 4 * c:5 * c] + bin_ref[:, 4 * c:5 * c])
        ewin[HALO_S:HALO_S + ts, :] = e
        e_ref[...] = e
        _build_phase_copies(pwin, p_copies, ph_rows, c)
        _conv_rows(pwin, wa_ref, offs_a, ts, c, _store_into(q_ref), bias_ref=ba_ref, copies=p_copies)
        q = q_ref[...]
        mu = jnp.mean(q, axis=-1, keepdims=True)
        xc = q - mu
        var = jnp.mean(xc * xc, axis=-1, keepdims=True)
        ln = xc * lax.rsqrt(var + EPS) * lg_ref[...] + lb_ref[...]
        y_ref[:, 0:c] = (ln * _sigmoid(ln)).astype(BF16)
        _conv_rows(ewin, wb_ref, offs_b, ts, c, _store_into(fbuf))
        g_b = u_ref[:, 2 * c:3 * c] + bin_ref[:, 2 * c:3 * c]
        y_ref[:, c:2 * c] = (g_b * fbuf[...]).astype(BF16)

    tile = lambda w: pl.BlockSpec((ts, w), lambda i: (i, 0))
    return _call(
        body, name=name, grid=(s // ts,),
        in_specs=[tile(5 * c), _full((1, 5 * c)), _full((K_A, c)), _full((1, c)), _full((1, c)), _full((1, c)),
                  _full((K_B, c))],
        out_specs=[tile(2 * c), tile(c), tile(c), tile(c)],
        out_shape=[jax.ShapeDtypeStruct((s, 2 * c), BF16)] + [jax.ShapeDtypeStruct((s, c), F32)] * 3,
        scratch_shapes=[pltpu.VMEM((ts + HALO_A, c), F32), pltpu.VMEM((ts + HALO_S, c), F32), pltpu.VMEM((ts, c), F32),
                        pltpu.VMEM((len(ph_a), ph_rows, c), F32)],
        args=(u, _row(b_in), wa, _row(ba), _row(lg), _row(lb), wb), hook=hook)


def _ffn_pw_fwd(u2, wf, *, name, hook=None):
    s, f2 = u2.shape
    f = f2 // 2
    ts = _pick(s, (TS_FFN,))
    offs = _causal_offsets(HALO_S, K_F)

    def body(u_ref, wf_ref, z_ref, uwin, cbuf):
        i = pl.program_id(0)

        @pl.when(i == 0)
        def _():
            uwin[0:HALO_S, :] = jnp.zeros((HALO_S, f2), F32)

        @pl.when(i > 0)
        def _():
            uwin[0:HALO_S, :] = uwin[ts:ts + HALO_S, :]

        uwin[HALO_S:HALO_S + ts, :] = u_ref[...]
        _conv_rows(uwin, wf_ref, offs, ts, f2, _store_into(cbuf))
        gate = cbuf[:, 0:f]
        z_ref[...] = (gate * _sigmoid(gate) * cbuf[:, f:f2]).astype(BF16)

    return _call(
        body, name=name, grid=(s // ts,),
        in_specs=[pl.BlockSpec((ts, f2), lambda i: (i, 0)), _full((K_F, f2))],
        out_specs=[pl.BlockSpec((ts, f), lambda i: (i, 0))],
        out_shape=[jax.ShapeDtypeStruct((s, f), BF16)],
        scratch_shapes=[pltpu.VMEM((ts + HALO_S, f2), F32), pltpu.VMEM((ts, f2), F32)],
        args=(u2, wf), hook=hook)


def _loss_bwd(x, tgt, g, *, name):
    s, d = x.shape
    ts = _pick(s, (TS_NORM, 256, 128))

    def body(x_ref, t_ref, g_ref, loss_ref, dx_ref, dxb_ref, dg_ref):
        @pl.when(pl.program_id(0) == 0)
        def _():
            loss_ref[...] = jnp.zeros_like(loss_ref)
            dg_ref[...] = jnp.zeros_like(dg_ref)

        xv = x_ref[...]
        r = lax.rsqrt(jnp.mean(xv * xv, axis=-1, keepdims=True) + EPS)
        xh = xv * r
        err = xh * g_ref[...] - t_ref[...]
        loss_ref[...] += 0.5 * jnp.sum(jnp.mean(err * err, axis=-1, keepdims=True))
        dy = err * (1.0 / d)
        dg_ref[...] += jnp.sum(dy * xh, axis=0, keepdims=True)
        dxh = dy * g_ref[...]
        dx = r * (dxh - xh * jnp.mean(dxh * xh, axis=-1, keepdims=True))
        dx_ref[...] = dx
        dxb_ref[...] = dx.astype(BF16)

    tile = pl.BlockSpec((ts, d), lambda i: (i, 0))
    return pl.pallas_call(
        body, name=name, grid=(s // ts,), in_specs=[tile, tile, _full((1, d))],
        out_specs=[_full((8, LANES)), tile, tile, _full((1, d))],
        out_shape=[jax.ShapeDtypeStruct((8, LANES), F32), jax.ShapeDtypeStruct((s, d), F32),
                   jax.ShapeDtypeStruct((s, d), BF16), jax.ShapeDtypeStruct((1, d), F32)],
        compiler_params=_params())(x, tgt, _row(g))


def _norm_bwd(x, dh, dx_in, g, *, name):
    s, d = x.shape
    ts = _pick(s, (TS_NORM, 256, 128))

    def body(x_ref, dh_ref, dxin_ref, g_ref, dx_ref, dxb_ref, dg_ref):
        @pl.when(pl.program_id(0) == 0)
        def _():
            dg_ref[...] = jnp.zeros_like(dg_ref)

        xv = x_ref[...]
        r = lax.rsqrt(jnp.mean(xv * xv, axis=-1, keepdims=True) + EPS)
        xh = xv * r
        dh_v = dh_ref[...]
        dg_ref[...] += jnp.sum(dh_v * xh, axis=0, keepdims=True)
        dxh = dh_v * g_ref[...]
        dx = dxin_ref[...] + r * (dxh - xh * jnp.mean(dxh * xh, axis=-1, keepdims=True))
        dx_ref[...] = dx
        dxb_ref[...] = dx.astype(BF16)

    tile = pl.BlockSpec((ts, d), lambda i: (i, 0))
    return pl.pallas_call(
        body, name=name, grid=(s // ts,), in_specs=[tile, tile, tile, _full((1, d))],
        out_specs=[tile, tile, _full((1, d))],
        out_shape=[jax.ShapeDtypeStruct((s, d), F32), jax.ShapeDtypeStruct((s, d), BF16),
                   jax.ShapeDtypeStruct((1, d), F32)],
        compiler_params=_params())(x, dh, dx_in, _row(g))


def _ffn_pw_bwd(u2, dz, wf, *, name, hook=None):
    s, f2 = u2.shape
    f = f2 // 2
    ts = _pick(s, (TS_FFN,))
    nt = s // ts
    hb = ts // HALO_S
    offs_c, offs_t = _causal_offsets(HALO_S, K_F), _anticausal_offsets(K_F)
    ph_u, rows_u = _phases(offs_c), _phase_rows(offs_c, ts)

    def body(u_ref, uh_ref, dz_ref, wf_ref, du_ref, dwf_ref, uwin, cbuf, dcwin, ucop):
        i = pl.program_id(0)
        u_copies = {r: ucop.at[n] for n, r in enumerate(ph_u)}

        @pl.when(i == 0)
        def _():
            dwf_ref[...] = jnp.zeros_like(dwf_ref)
            dcwin[ts:ts + HALO_S, :] = jnp.zeros((HALO_S, f2), F32)

        @pl.when(i > 0)
        def _():
            dcwin[ts:ts + HALO_S, :] = dcwin[0:HALO_S, :]

        @pl.when(i == nt - 1)
        def _():
            uwin[0:HALO_S, :] = jnp.zeros((HALO_S, f2), F32)

        @pl.when(i < nt - 1)
        def _():
            uwin[0:HALO_S, :] = uh_ref[...]

        uwin[HALO_S:HALO_S + ts, :] = u_ref[...]
        _build_phase_copies(uwin, u_copies, rows_u, f2)
        _conv_rows(uwin, wf_ref, offs_c, ts, f2, _store_into(cbuf), copies=u_copies)
        gate = cbuf[:, 0:f]
        sg = _sigmoid(gate)
        dz_v = dz_ref[...]
        dcwin[0:ts, 0:f] = dz_v * cbuf[:, f:f2] * _dsilu(gate, sg)
        dcwin[0:ts, f:f2] = dz_v * gate * sg

        def emit(r0, c0, v):
            du_ref[r0:r0 + CONV_ROWS, c0:c0 + LANES] = v.astype(BF16)
        _conv_rows(dcwin, wf_ref, offs_t, ts, f2, emit)
        _conv_wgrad(dcwin, uwin, offs_c, ts, f2, dwf_ref, copies=u_copies)

    rev = lambda i: (nt - 1 - i, 0)
    return _call(
        body, name=name, grid=(nt,),
        in_specs=[pl.BlockSpec((ts, f2), rev),
                  pl.BlockSpec((HALO_S, f2), lambda i: (jnp.maximum((nt - 1 - i) * hb - 1, 0), 0)),
                  pl.BlockSpec((ts, f), rev), _full((K_F, f2))],
        out_specs=[pl.BlockSpec((ts, f2), rev), _full((K_F, f2))],
        out_shape=[jax.ShapeDtypeStruct((s, f2), BF16), jax.ShapeDtypeStruct((K_F, f2), F32)],
        scratch_shapes=[pltpu.VMEM((ts + HALO_S, f2), F32), pltpu.VMEM((ts, f2), F32),
                        pltpu.VMEM((ts + HALO_S, f2), F32), pltpu.VMEM((len(ph_u), rows_u, f2), F32)],
        args=(u2, u2, dz, wf), hook=hook)


def _mix_pw_bwd(dy, u, p, q, e, b_in, wa, lg, lb, wb, *, name, hook=None):
    s = u.shape[0]
    c = D_CONF
    ts = _pick(s, (TS_MIX, 128))
    nt = s // ts
    offs_ac, offs_at = _causal_offsets(HALO_A, K_A), _anticausal_offsets(K_A)
    offs_bc, offs_bt = _causal_offsets(HALO_S, K_B), _anticausal_offsets(K_B)
    ph_p, rows_p = _phases(offs_ac), _phase_rows(offs_ac, ts)
    ph_q, rows_q = _phases(offs_at), _phase_rows(offs_at, ts)

    def body(dy_ref, u_ref, p_ref, ph_ref, q_ref, e_ref, eh_ref, bin_ref, wa_ref, lg_ref, lb_ref, wb_ref,
             du_ref, dbin_ref, dwa_ref, dba_ref, dlg_ref, dlb_ref, dwb_ref, pwin, ewin, dqwin, dfwin, buf, pcop, qcop):
        i = pl.program_id(0)
        p_copies = {r: pcop.at[n] for n, r in enumerate(ph_p)}
        q_copies = {r: qcop.at[n] for n, r in enumerate(ph_q)}

        @pl.when(i == 0)
        def _():
            for r in (dbin_ref, dwa_ref, dba_ref, dlg_ref, dlb_ref, dwb_ref):
                r[...] = jnp.zeros_like(r)
            dqwin[ts:ts + HALO_A, :] = jnp.zeros((HALO_A, c), F32)
            dfwin[ts:ts + HALO_S, :] = jnp.zeros((HALO_S, c), F32)

        @pl.when(i > 0)
        def _():
            dqwin[ts:ts + HALO_A, :] = dqwin[0:HALO_A, :]
            dfwin[ts:ts + HALO_S, :] = dfwin[0:HALO_S, :]

        @pl.when(i == nt - 1)
        def _():
            pwin[0:HALO_A, :] = jnp.zeros((HALO_A, c), F32)
            ewin[0:HALO_S, :] = jnp.zeros((HALO_S, c), F32)

        @pl.when(i < nt - 1)
        def _():
            pwin[0:HALO_A, :] = ph_ref[...]
            ewin[0:HALO_S, :] = eh_ref[...]

        pwin[HALO_A:HALO_A + ts, :] = p_ref[...]
        ewin[HALO_S:HALO_S + ts, :] = e_ref[...]

        def colsum(v):
            return jnp.sum(v, axis=0, keepdims=True)

        q_v = q_ref[...]
        mu = jnp.mean(q_v, axis=-1, keepdims=True)
        xc = q_v - mu
        rstd = lax.rsqrt(jnp.mean(xc * xc, axis=-1, keepdims=True) + EPS)
        nrm = xc * rstd
        ln = nrm * lg_ref[...] + lb_ref[...]
        dln = dy_ref[:, 0:c] * _dsilu(ln, _sigmoid(ln))
        dlg_ref[...] += colsum(dln * nrm)
        dlb_ref[...] += colsum(dln)
        dn = dln * lg_ref[...]
        dq = rstd * (dn - jnp.mean(dn, axis=-1, keepdims=True) - nrm * jnp.mean(dn * nrm, axis=-1, keepdims=True))
        dba_ref[...] += colsum(dq)
        dqwin[0:ts, :] = dq
        _build_phase_copies(dqwin, q_copies, rows_q, c)
        _build_phase_copies(pwin, p_copies, rows_p, c)
        _conv_rows(dqwin, wa_ref, offs_at, ts, c, _store_into(buf), copies=q_copies)
        _conv_wgrad(dqwin, pwin, offs_ac, ts, c, dwa_ref, copies=p_copies)
        dp = buf[...]
        a_val = u_ref[:, 0:c] + bin_ref[:, 0:c]
        sg = _sigmoid(u_ref[:, c:2 * c] + bin_ref[:, c:2 * c])
        d_aval = dp * sg
        d_agate = dp * a_val * sg * (1.0 - sg)
        du_ref[:, 0:c] = d_aval.astype(BF16)
        du_ref[:, c:2 * c] = d_agate.astype(BF16)
        dbin_ref[:, 0:c] += colsum(d_aval)
        dbin_ref[:, c:2 * c] += colsum(d_agate)

        ds = dy_ref[:, c:2 * c]
        _conv_rows(ewin, wb_ref, offs_bc, ts, c, _store_into(buf))
        d_gb = ds * buf[...]
        dfwin[0:ts, :] = ds * (u_ref[:, 2 * c:3 * c] + bin_ref[:, 2 * c:3 * c])
        _conv_rows(dfwin, wb_ref, offs_bt, ts, c, _store_into(buf))
        _conv_wgrad(dfwin, ewin, offs_bc, ts, c, dwb_ref)
        de = buf[...]
        d_gc = de * (u_ref[:, 4 * c:5 * c] + bin_ref[:, 4 * c:5 * c])
        d_v = de * (u_ref[:, 3 * c:4 * c] + bin_ref[:, 3 * c:4 * c])
        du_ref[:, 2 * c:3 * c] = d_gb.astype(BF16)
        du_ref[:, 3 * c:4 * c] = d_gc.astype(BF16)
        du_ref[:, 4 * c:5 * c] = d_v.astype(BF16)
        dbin_ref[:, 2 * c:3 * c] += colsum(d_gb)
        dbin_ref[:, 3 * c:4 * c] += colsum(d_gc)
        dbin_ref[:, 4 * c:5 * c] += colsum(d_v)

    rev = lambda i: (nt - 1 - i, 0)
    tile = lambda w: pl.BlockSpec((ts, w), rev)
    halo = lambda h: pl.BlockSpec((h, c), lambda i: (jnp.maximum((nt - 1 - i) * (ts // h) - 1, 0), 0))
    small = [(1, 5 * c), (K_A, c), (1, c), (1, c), (1, c), (K_B, c)]
    return _call(
        body, name=name, grid=(nt,),
        in_specs=[tile(2 * c), tile(5 * c), tile(c), halo(HALO_A), tile(c), tile(c), halo(HALO_S),
                  _full((1, 5 * c)), _full((K_A, c)), _full((1, c)), _full((1, c)), _full((K_B, c))],
        out_specs=[tile(5 * c)] + [_full(sh) for sh in small],
        out_shape=[jax.ShapeDtypeStruct((s, 5 * c), BF16)] + [jax.ShapeDtypeStruct(sh, F32) for sh in small],
        scratch_shapes=[pltpu.VMEM((ts + HALO_A, c), F32), pltpu.VMEM((ts + HALO_S, c), F32),
                        pltpu.VMEM((ts + HALO_A, c), F32), pltpu.VMEM((ts + HALO_S, c), F32),
                        pltpu.VMEM((ts, c), F32), pltpu.VMEM((len(ph_p), rows_p, c), F32),
                        pltpu.VMEM((len(ph_q), rows_q, c), F32)],
        args=(dy, u, p, p, q, e, e, _row(b_in), wa, _row(lg), _row(lb), wb), hook=hook)


def _adamw(g, w, m, v, *, name):
    r, c = g.shape
    tr = _pick(r, (256, 128, 64, 8)) if r > 256 else r

    def body(g_ref, w_ref, m_ref, v_ref, d_ref, nm_ref, nv_ref):
        gv = g_ref[...]
        nm = ADAM_B1 * m_ref[...] + (1.0 - ADAM_B1) * gv
        nv = ADAM_B2 * v_ref[...] + (1.0 - ADAM_B2) * (gv * gv)
        m_hat = nm / (1.0 - ADAM_B1 ** ADAM_STEP)
        v_hat = nv / (1.0 - ADAM_B2 ** ADAM_STEP)
        d_ref[...] = -ADAM_LR * (m_hat / (jnp.sqrt(v_hat) + ADAM_EPS) + ADAM_WD * w_ref[...])
        nm_ref[...] = nm
        nv_ref[...] = nv

    tile = pl.BlockSpec((tr, c), lambda i: (i, 0))
    return pl.pallas_call(body, name=name, grid=(r // tr,), in_specs=[tile] * 4, out_specs=[tile] * 3,
                          out_shape=[jax.ShapeDtypeStruct((r, c), F32)] * 3, compiler_params=_params())(g, w, m, v)


def _cast_place(wshard, chip, *, name):
    l, r2, c = wshard.shape
    tr = r2 // 2

    def body(chip_ref, w_ref, *o_refs):
        for li, o_ref in enumerate(o_refs):
            o_ref[...] = w_ref[li].astype(BF16)

    grid_spec = pltpu.PrefetchScalarGridSpec(
        num_scalar_prefetch=1, grid=(2,),
        in_specs=[pl.BlockSpec((l, tr, c), lambda i, chip_ref: (0, i, 0))],
        out_specs=[pl.BlockSpec((None, tr, c), lambda i, chip_ref: (chip_ref[0], i, 0))] * l)
    return pl.pallas_call(body, name=name, grid_spec=grid_spec,
                          out_shape=[jax.ShapeDtypeStruct((N_CHIPS, r2, c), BF16)] * l, compiler_params=_params())(chip, wshard)


def _pair_sum(g, recv, core, *, name):
    _, _, r, c = g.shape

    def body(core_ref, g_ref, r_ref, o_ref):
        o_ref[...] = (g_ref[...] + r_ref[...].astype(F32)).astype(BF16)

    grid_spec = pltpu.PrefetchScalarGridSpec(
        num_scalar_prefetch=1, grid=(N_CHIPS,),
        in_specs=[pl.BlockSpec((None, None, r, c), lambda i, core_ref: (i, core_ref[0], 0, 0)),
                  pl.BlockSpec((None, r, c), lambda i, core_ref: (i, 0, 0))],
        out_specs=pl.BlockSpec((None, r, c), lambda i, core_ref: (i, 0, 0)))
    return pl.pallas_call(body, name=name, grid_spec=grid_spec,
                          out_shape=jax.ShapeDtypeStruct((N_CHIPS, r, c), BF16), compiler_params=_params())(core, g, recv)


def _chip_sum(own, recv, core_chip, stack, *, name):
    _, r, c = own.shape
    tr = r // 2
    l, layers, buf = stack

    def body(cc_ref, own_ref, r1_ref, r2_ref, r3_ref, *rest):
        o_ref = rest[-1]
        o_ref[...] = ((own_ref[...].astype(F32) + r1_ref[...].astype(F32)) + r2_ref[...].astype(F32)) + r3_ref[...].astype(F32)

    def slot(s):
        return pl.BlockSpec((None, tr, c), lambda i, cc_ref: ((cc_ref[1] + s) % N_CHIPS, i, 0))

    in_specs = [slot(0), slot(1), slot(2), slot(3)]
    args = [own, recv, recv, recv]
    aliases = {}
    if buf is not None:
        aliases = {1 + len(args): 0}
        in_specs.append(ANY)
        args.append(buf)
    grid_spec = pltpu.PrefetchScalarGridSpec(
        num_scalar_prefetch=1, grid=(2,), in_specs=in_specs,
        out_specs=pl.BlockSpec((None, None, tr, c), lambda i, cc_ref: (l, cc_ref[0], i, 0)))
    return pl.pallas_call(body, name=name, grid_spec=grid_spec, input_output_aliases=aliases,
                          out_shape=jax.ShapeDtypeStruct((layers, 2, r, c), F32), compiler_params=_params())(core_chip, *args)


def _place():
    x, y, c = lax.axis_index("x"), lax.axis_index("y"), lax.axis_index("c")
    chips = [(1 - x, y), (x, 1 - y), (1 - x, 1 - y)]
    return x, y, c, 2 * x + y, chips


def _remote(src, dst, send_sems, recv_sems, k, dev):
    return pltpu.make_async_remote_copy(src_ref=src, dst_ref=dst, send_sem=send_sems.at[k], recv_sem=recv_sems.at[k],
                                        device_id=dev, device_id_type=MESH)


def _sibling_halves(grads, *, name):
    n = len(grads)
    out_shape = [jax.ShapeDtypeStruct(g.shape[:1] + g.shape[2:], g.dtype) for g in grads]

    def body(*refs):
        ins, outs = refs[:n], refs[n:2 * n]
        send_sems, recv_sems = refs[2 * n:]
        x, y, c, _, _ = _place()
        cps = [_remote(ins[a].at[:, 1 - c], outs[a], send_sems, recv_sems, a, (x, y, 1 - c)) for a in range(n)]
        for cp in cps:
            cp.start()
        for cp in cps:
            cp.wait()

    return pl.pallas_call(body, name=name, in_specs=[ANY] * n, out_specs=[ANY] * n, out_shape=out_shape,
                          scratch_shapes=[pltpu.SemaphoreType.DMA((n,)), pltpu.SemaphoreType.DMA((n,))])(*grads)


class _Gather:
    def __init__(self, bufs):
        self.operands = list(bufs)
        self.out_shape = [jax.ShapeDtypeStruct(b.shape, b.dtype) for b in bufs]
        self.aliases = {a: a for a in range(len(bufs))}

    def sems(self):
        n = 6 * len(self.operands)
        return [pltpu.SemaphoreType.DMA((n,)), pltpu.SemaphoreType.DMA((n,))]

    def _sends(self, ins, outs, sems):
        x, y, c, j, chips = _place()
        return [_remote(ins[a].at[j, c], outs[a].at[j, c], sems[0], sems[1], 6 * a + k, (cx, cy, c))
                for a in range(len(ins)) for k, (cx, cy) in enumerate(chips)]

    def start(self, ins, outs, sems):
        for cp in self._sends(ins, outs, sems):
            cp.start()

    def finish(self, ins, outs, sems):
        x, y, c, j, chips = _place()
        sib = (x, y, 1 - c)
        passed = []
        for a in range(len(ins)):
            for k, (cx, cy) in enumerate(chips):
                blk = outs[a].at[2 * cx + cy, c]
                _remote(blk, blk, sems[0], sems[1], 6 * a + k, (cx, cy, c)).wait_recv()
                cp = _remote(blk, blk, sems[0], sems[1], 6 * a + 3 + k, sib)
                cp.start()
                passed.append(cp)
        for a in range(len(ins)):
            for k, (cx, cy) in enumerate(chips):
                blk = outs[a].at[2 * cx + cy, 1 - c]
                _remote(blk, blk, sems[0], sems[1], 6 * a + 3 + k, sib).wait_recv()
        for cp in self._sends(ins, outs, sems) + passed:
            cp.wait_send()


class _Scatter:
    def __init__(self, srcs):
        self.operands = list(srcs)
        self.out_shape = [jax.ShapeDtypeStruct(s.shape, s.dtype) for s in srcs]
        self.aliases = {}

    def sems(self):
        n = 3 * len(self.operands)
        return [pltpu.SemaphoreType.DMA((n,)), pltpu.SemaphoreType.DMA((n,))]

    def _sends(self, ins, outs, sems):
        x, y, c, j, chips = _place()
        return [_remote(ins[a].at[2 * cx + cy], outs[a].at[j], sems[0], sems[1], 3 * a + k, (cx, cy, c))
                for a in range(len(ins)) for k, (cx, cy) in enumerate(chips)]

    def start(self, ins, outs, sems):
        for cp in self._sends(ins, outs, sems):
            cp.start()

    def finish(self, ins, outs, sems):
        x, y, c, j, chips = _place()
        for a in range(len(ins)):
            for k, (cx, cy) in enumerate(chips):
                blk = outs[a].at[2 * cx + cy]
                _remote(blk, blk, sems[0], sems[1], 3 * a + k, (cx, cy, c)).wait_recv()
        for cp in self._sends(ins, outs, sems):
            cp.wait_send()


def _standalone(hook, *, name):
    n_in, n_out = len(hook.operands), len(hook.out_shape)

    def body(*refs):
        ins, outs, sems = refs[:n_in], refs[n_in:n_in + n_out], refs[n_in + n_out:]
        hook.start(ins, outs, sems)
        hook.finish(ins, outs, sems)

    return list(pl.pallas_call(body, name=name, in_specs=[ANY] * n_in, out_specs=[ANY] * n_out,
                               out_shape=hook.out_shape, input_output_aliases=hook.aliases,
                               scratch_shapes=hook.sems())(*hook.operands))


def _share_halves(bufs, *, name):
    n = len(bufs)

    def body(*refs):
        ins, outs = refs[:n], refs[n:2 * n]
        send_sems, recv_sems = refs[2 * n:]
        x, y, c, _, _ = _place()
        sends = [_remote(ins[a].at[:, c], outs[a].at[:, c], send_sems, recv_sems, a, (x, y, 1 - c)) for a in range(n)]
        for cp in sends:
            cp.start()
        for a in range(n):
            blk = outs[a].at[:, 1 - c]
            _remote(blk, blk, send_sems, recv_sems, a, (x, y, 1 - c)).wait_recv()
        for cp in sends:
            cp.wait_send()

    return pl.pallas_call(
        body, name=name, in_specs=[ANY] * n, out_specs=[ANY] * n,
        out_shape=[jax.ShapeDtypeStruct(b.shape, b.dtype) for b in bufs],
        input_output_aliases={a: a for a in range(n)},
        scratch_shapes=[pltpu.SemaphoreType.DMA((n,)), pltpu.SemaphoreType.DMA((n,))])(*bufs)


def _allreduce_small(v, *, name):
    r, c = v.shape
    n_dev = 8

    def body(v_ref, o_ref, buf, send_sems, recv_sems):
        x, y, cc = lax.axis_index("x"), lax.axis_index("y"), lax.axis_index("c")
        me = 4 * x + 2 * y + cc
        cps = []
        for d in range(1, n_dev):
            peer = (1 - x if d & 4 else x, 1 - y if d & 2 else y, 1 - cc if d & 1 else cc)
            cps.append(_remote(v_ref, buf.at[me], send_sems, recv_sems, d - 1, peer))
        for cp in cps:
            cp.start()
        buf[me] = v_ref[...]
        for cp in cps:
            cp.wait()
        acc = buf[0]
        for d in range(1, n_dev):
            acc = acc + buf[d]
        o_ref[...] = acc

    return pl.pallas_call(
        body, name=name, in_specs=[VMEM], out_specs=VMEM, out_shape=jax.ShapeDtypeStruct((r, c), F32),
        scratch_shapes=[pltpu.VMEM((n_dev, r, c), F32), pltpu.SemaphoreType.DMA((n_dev - 1,)),
                        pltpu.SemaphoreType.DMA((n_dev - 1,))])(v)


def _pack(arrays, row_multiple=8):
    flat = jnp.concatenate([a.reshape(-1) for a in arrays])
    rows = -(-flat.shape[0] // LANES)
    rows = -(-rows // row_multiple) * row_multiple
    return jnp.pad(flat, (0, rows * LANES - flat.shape[0])).reshape(rows, LANES)


def _unpack(slab, shapes):
    flat = slab.reshape(-1)
    out, pos = [], 0
    for sh in shapes:
        size = 1
        for dim in sh:
            size *= dim
        out.append(flat[pos:pos + size].reshape(sh))
        pos += size
    return out


def kernel(x, mix_norm_g, w_in, b_in, conv_a_w, conv_a_b, ln_a_g, ln_a_b, conv_b_w, w_out, ffn_norm_g, w_up, conv_f_w, w_down, final_norm_g, loss_target, m_mix_norm_g, m_w_in, m_b_in, m_conv_a_w, m_conv_a_b, m_ln_a_g, m_ln_a_b, m_conv_b_w, m_w_out, m_ffn_norm_g, m_w_up, m_conv_f_w, m_w_down, m_final_norm_g, v_mix_norm_g, v_w_in, v_b_in, v_conv_a_w, v_conv_a_b, v_ln_a_g, v_ln_a_b, v_conv_b_w, v_w_out, v_ffn_norm_g, v_w_up, v_conv_f_w, v_w_down, v_final_norm_g):
    w = dict(mix_norm_g=mix_norm_g, w_in=w_in, b_in=b_in, conv_a_w=conv_a_w, conv_a_b=conv_a_b, ln_a_g=ln_a_g,
             ln_a_b=ln_a_b, conv_b_w=conv_b_w, w_out=w_out, ffn_norm_g=ffn_norm_g, w_up=w_up, conv_f_w=conv_f_w,
             w_down=w_down, final_norm_g=final_norm_g)
    m = dict(mix_norm_g=m_mix_norm_g, w_in=m_w_in, b_in=m_b_in, conv_a_w=m_conv_a_w, conv_a_b=m_conv_a_b,
             ln_a_g=m_ln_a_g, ln_a_b=m_ln_a_b, conv_b_w=m_conv_b_w, w_out=m_w_out, ffn_norm_g=m_ffn_norm_g,
             w_up=m_w_up, conv_f_w=m_conv_f_w, w_down=m_w_down, final_norm_g=m_final_norm_g)
    v = dict(mix_norm_g=v_mix_norm_g, w_in=v_w_in, b_in=v_b_in, conv_a_w=v_conv_a_w, conv_a_b=v_conv_a_b,
             ln_a_g=v_ln_a_g, ln_a_b=v_ln_a_b, conv_b_w=v_conv_b_w, w_out=v_w_out, ffn_norm_g=v_ffn_norm_g,
             w_up=v_w_up, conv_f_w=v_conv_f_w, w_down=v_w_down, final_norm_g=v_final_norm_g)
    depth = w_in.shape[0]
    xs = x[0]
    tgt = loss_target[0]
    chip = (2 * lax.axis_index("x") + lax.axis_index("y")).astype(jnp.int32)
    chip_arr = chip.reshape(1)
    core = lax.axis_index("c").astype(jnp.int32).reshape(1)
    core_chip = jnp.concatenate([core, chip_arr])

    def halves(b):
        return b.reshape(N_CHIPS, 2, b.shape[1] // 2, b.shape[2])

    conv_shapes = [w[nm].shape for nm in CHANNEL_SHARDED]
    conv_slab = _pack([w[nm] for nm in CHANNEL_SHARDED], row_multiple=16)
    zero = jnp.int32(0)
    slab_buf = lax.dynamic_update_slice(jnp.zeros((N_CHIPS,) + conv_slab.shape, F32), conv_slab[None], (chip, zero, zero))
    wbuf = {}
    for nm in BIG:
        for l, b in enumerate(_cast_place(w[nm], chip_arr, name=f"cast_{nm}")):
            wbuf[nm, l] = halves(b)

    def gather_into(keys, done):
        for key, g in zip(keys, done):
            wbuf[key] = g

    def wmat(nm, l):
        g = wbuf[nm, l]
        if nm in ("w_in", "w_up"):
            return g.reshape(N_CHIPS, 2 * g.shape[2], g.shape[3])
        return g.reshape(N_CHIPS * 2 * g.shape[2], g.shape[3])

    first = [("w_in", 0), ("w_out", 0)]
    done = _standalone(_Gather([wbuf[k] for k in first] + [halves(slab_buf)]), name="gather_first")
    gather_into(first, done)
    conv_full = done[-1].reshape(N_CHIPS, -1, LANES)
    taps = {}
    for nm, parts in zip(CHANNEL_SHARDED, zip(*[_unpack(conv_full[jj], conv_shapes) for jj in range(N_CHIPS)])):
        taps[nm] = jnp.concatenate(parts, axis=-1)

    saved = []
    cur = xs
    h1 = _norm_fwd(cur, w["mix_norm_g"][0], name="norm_mix_0")
    for l in range(depth):
        u = _mm(h1, wmat("w_in", l), mode="nn", b_sharded=True, name=f"mm_in_{l}")
        keys = [("w_up", l), ("w_down", l)] if l == 0 else [("w_up", l)]
        (y, p, q, e), done = _mix_pw_fwd(u, w["b_in"][l], taps["conv_a_w"][l], w["conv_a_b"][l], w["ln_a_g"][l],
                                         w["ln_a_b"][l], taps["conv_b_w"][l], name=f"mix_fwd_{l}",
                                         hook=_Gather([wbuf[k] for k in keys]))
        gather_into(keys, done)
        x1, h2 = _mm_res_norm(y, wmat("w_out", l), cur, w["ffn_norm_g"][l], name=f"mm_out_{l}")
        u2 = _mm(h2, wmat("w_up", l), mode="nn", b_sharded=True, name=f"mm_up_{l}")
        keys = [("w_in", l + 1), ("w_out", l + 1), ("w_down", l + 1)] if l + 1 < depth else []
        (z,), done = _ffn_pw_fwd(u2, taps["conv_f_w"][l], name=f"ffn_fwd_{l}",
                                 hook=_Gather([wbuf[k] for k in keys]) if keys else None)
        gather_into(keys, done)
        saved.append((cur, h1, u, y, p, q, e, x1, h2, u2, z))
        if l + 1 < depth:
            cur, h1 = _mm_res_norm(z, wmat("w_down", l), x1, w["mix_norm_g"][l + 1], name=f"mm_down_{l}")
        else:
            cur = _mm(z, wmat("w_down", l), mode="nn", res=x1, name=f"mm_down_{l}")

    loss_blk, dx, dxb, dgf = _loss_bwd(cur, tgt, w["final_norm_g"], name="loss_bwd")
    loss = lax.psum(loss_blk[0, 0], ("x", "y", "c"))

    gfull, chip_sums, from_chips = {}, {}, {}

    def four_d(g):
        if g.ndim == 2:
            g = g.reshape(N_CHIPS, g.shape[0] // N_CHIPS, g.shape[1])
        return halves(g)

    def scatter_hook(keys, tag):
        from_sibling = _sibling_halves([four_d(gfull[k][1]) for k in keys], name=f"reduce_sibling_{tag}")
        for k, r in zip(keys, from_sibling):
            chip_sums[k] = _pair_sum(four_d(gfull[k][0]), r, core, name=f"pair_sum_{k[0]}_{k[1]}")
        return _Scatter([chip_sums[k] for k in keys])

    sg = {nm: [None] * depth for nm in SMALL if nm != "final_norm_g"}
    pending = []
    for l in reversed(range(depth)):
        x0, h1, u, y, p, q, e, x1, h2, u2, z = saved[l]
        dz = _mm(dxb, wmat("w_down", l), mode="nt", name=f"mm_ddown_{l}")
        gfull["w_down", l] = _mm(z, dxb, mode="tn", also_bf16=True, name=f"mm_gdown_{l}")
        keys = pending + [("w_down", l)]
        (du2, dwf), done = _ffn_pw_bwd(u2, dz, taps["conv_f_w"][l], name=f"ffn_bwd_{l}", hook=scatter_hook(keys, f"a{l}"))
        from_chips.update(zip(keys, done))
        dx1, dx1b, dg2 = _mm_norm_bwd(du2, wmat("w_up", l), x1, dx, w["ffn_norm_g"][l], name=f"mm_dup_{l}")
        gfull["w_up", l] = _mm(h2, du2, mode="tn", out_sharded=True, also_bf16=True, name=f"mm_gup_{l}")
        dy = _mm(dx1b, wmat("w_out", l), mode="nt", name=f"mm_dout_{l}")
        gfull["w_out", l] = _mm(y, dx1b, mode="tn", also_bf16=True, name=f"mm_gout_{l}")
        keys = [("w_up", l), ("w_out", l)]
        (du, dbin, dwa, dba, dlg, dlb, dwb), done = _mix_pw_bwd(
            dy, u, p, q, e, w["b_in"][l], taps["conv_a_w"][l], w["ln_a_g"][l], w["ln_a_b"][l],
            taps["conv_b_w"][l], name=f"mix_bwd_{l}", hook=scatter_hook(keys, f"b{l}"))
        from_chips.update(zip(keys, done))
        dx, dxb, dg1 = _mm_norm_bwd(du, wmat("w_in", l), x0, dx1, w["mix_norm_g"][l], name=f"mm_din_{l}")
        gfull["w_in", l] = _mm(h1, du, mode="tn", out_sharded=True, also_bf16=True, name=f"mm_gin_{l}")
        pending = [("w_in", l)]
        for nm, g in (("mix_norm_g", dg1), ("b_in", dbin), ("conv_a_w", dwa), ("conv_a_b", dba), ("ln_a_g", dlg),
                      ("ln_a_b", dlb), ("conv_b_w", dwb), ("ffn_norm_g", dg2), ("conv_f_w", dwf)):
            sg[nm][l] = g.reshape(g.shape[-1]) if g.shape[0] == 1 else g
    grad_x = dx[None]
    from_chips.update(zip(pending, _standalone(scatter_hook(pending, "last"), name="reduce_chips_last")))

    mine = []
    for nm in BIG:
        buf = None
        for l in range(depth):
            buf = _chip_sum(chip_sums[nm, l], from_chips[nm, l], core_chip, (l, depth, buf), name=f"chip_sum_{nm}_{l}")
        mine.append(buf)
    shared = _share_halves(mine, name="share_halves")
    grads = {}
    for nm, g in zip(BIG, shared):
        grads[nm] = g.reshape(w[nm].shape)

    small_full_shapes = []
    small_parts = []
    for nm in SMALL:
        if nm == "final_norm_g":
            g = dgf.reshape(-1)
        else:
            g = jnp.stack(sg[nm])
        small_parts.append(g)
        small_full_shapes.append(g.shape)
    summed = _unpack(_allreduce_small(_pack(small_parts), name="reduce_small"), small_full_shapes)
    for nm, g in zip(SMALL, summed):
        if nm in CHANNEL_SHARDED:
            width = w[nm].shape[-1]
            g = lax.dynamic_slice_in_dim(g, chip * width, width, axis=2)
        grads[nm] = g

    delta, new_m, new_v = {}, {}, {}
    for nm in BIG:
        sh = w[nm].shape
        two_d = lambda a: a.reshape(sh[0] * sh[1], sh[2])
        d_, m_, v_ = _adamw(two_d(grads[nm]), two_d(w[nm]), two_d(m[nm]), two_d(v[nm]), name=f"adamw_{nm}")
        delta[nm], new_m[nm], new_v[nm] = d_.reshape(sh), m_.reshape(sh), v_.reshape(sh)
    small_shapes = [w[nm].shape for nm in SMALL]
    outs = _adamw(_pack([grads[nm] for nm in SMALL]), _pack([w[nm] for nm in SMALL]), _pack([m[nm] for nm in SMALL]),
                  _pack([v[nm] for nm in SMALL]), name="adamw_small")
    for store, slab in zip((delta, new_m, new_v), outs):
        for nm, a in zip(SMALL, _unpack(slab, small_shapes)):
            store[nm] = a

    return (loss, grad_x, *[grads[nm] for nm in TWIN_WEIGHTS], *[delta[nm] for nm in TWIN_WEIGHTS],
            *[new_m[nm] for nm in TWIN_WEIGHTS], *[new_v[nm] for nm in TWIN_WEIGHTS])
```

```python
import functools

import jax
import jax.numpy as jnp
from jax import lax
from jax.experimental import pallas as pl
from jax.experimental.pallas import tpu as pltpu

F32 = jnp.float32
BF16 = jnp.bfloat16
MESH = pl.DeviceIdType.MESH
ANY = pl.BlockSpec(memory_space=pl.ANY)
VMEM = pl.BlockSpec(memory_space=pltpu.VMEM)

EPS = 1e-6
ADAM_LR, ADAM_B1, ADAM_B2, ADAM_EPS, ADAM_WD, ADAM_STEP = 0.001, 0.9, 0.999, 1e-08, 0.01, 10

D_CONF = 512
K_A, K_B, K_F = 31, 3, 3
HALO_A, HALO_S = 32, 8
N_CHIPS = 4
LANES = 128
SUBLANES = 8
CONV_ROWS = 128
TS_MIX, TS_FFN, TS_NORM = 256, 128, 512
VMEM_LIMIT = 48 * 1024 * 1024
VMEM_LIMIT_WIDE = 56 * 1024 * 1024

TWIN_WEIGHTS = ['mix_norm_g', 'w_in', 'b_in', 'conv_a_w', 'conv_a_b', 'ln_a_g', 'ln_a_b', 'conv_b_w', 'w_out',
                'ffn_norm_g', 'w_up', 'conv_f_w', 'w_down', 'final_norm_g']
BIG = ['w_in', 'w_out', 'w_up', 'w_down']
SMALL = [n for n in TWIN_WEIGHTS if n not in BIG]
CHANNEL_SHARDED = ['conv_a_w', 'conv_b_w', 'conv_f_w']


def _params(n_grid=1):
    return pltpu.CompilerParams(dimension_semantics=("arbitrary",) * n_grid, vmem_limit_bytes=VMEM_LIMIT)


def _pick(n, cands):
    for c in cands:
        if n % c == 0:
            return c
    raise ValueError(f"no tile for {n}")


_DN = {"nn": (((1,), (0,)), ((), ())), "nt": (((1,), (1,)), ((), ())), "tn": (((0,), (0,)), ((), ()))}


def _mm(a, b, *, mode, name, layer=None, b_sharded=False, out_sharded=False, res=None, stack=None, out_dtype=F32,
        also_bf16=False, hook=None):
    dn = _DN[mode]
    lead = () if layer is None else (None,)
    pre = () if layer is None else (layer,)
    wide = (1408, 1280, 1024, 512)
    if mode == "tn":
        k, m = a.shape
        n = b.shape[1]
        tm = _pick(m, (512, 256))
        tn = n // N_CHIPS if out_sharded else _pick(n, wide)
    else:
        m, k = a.shape
        tm = _pick(m, (1024, 512, 256, 128)) if k <= 1024 else _pick(m, (512, 256, 128))
        if mode == "nn":
            n = b.shape[-1] * N_CHIPS if b_sharded else b.shape[-1]
            tn = b.shape[-1] if b_sharded else _pick(n, wide)
        else:
            n = b.shape[-2]
            tn = _pick(n, (1024, 512, 256)) if b_sharded else _pick(n, wide)
    n_outer = (n // tn - 1) * a.size < (m // tm - 1) * (b.size if layer is None else b.size // b.shape[0])

    def im(f):
        return (lambda g0, g1: f(g1, g0)) if n_outer else f

    if mode == "tn":
        a_spec = pl.BlockSpec((k, tm), im(lambda i, j: (0, i)))
        b_spec = pl.BlockSpec((k, tn), im(lambda i, j: (0, j)))
    else:
        a_spec = pl.BlockSpec((tm, k), im(lambda i, j: (i, 0)))
        if mode == "nn" and b_sharded:
            b_spec = pl.BlockSpec(lead + (None, k, tn), im(lambda i, j: pre + (j, 0, 0)))
        elif mode == "nn":
            b_spec = pl.BlockSpec(lead + (k, tn), im(lambda i, j: pre + (0, j)))
        elif b_sharded:
            b_spec = pl.BlockSpec(lead + (N_CHIPS, tn, b.shape[-1]), im(lambda i, j: pre + (0, j, 0)))
        else:
            b_spec = pl.BlockSpec(lead + (tn, k), im(lambda i, j: pre + (j, 0)))
    slot_lead, slot_pre, slot_shape = (), (), ()
    if stack is not None:
        slot_lead, slot_pre, slot_shape = (None,), (stack[0],), (stack[1],)
    if out_sharded:
        out_shape = jax.ShapeDtypeStruct(slot_shape + (N_CHIPS, m, tn), out_dtype)
        o_spec = pl.BlockSpec(slot_lead + (None, tm, tn), im(lambda i, j: slot_pre + (j, i, 0)))
    else:
        out_shape = jax.ShapeDtypeStruct(slot_shape + (m, n), out_dtype)
        o_spec = pl.BlockSpec(slot_lead + (tm, tn), im(lambda i, j: slot_pre + (i, j)))
    in_specs = [a_spec, b_spec]
    args = [a, b]
    aliases = {}
    if res is not None:
        in_specs.append(pl.BlockSpec((tm, tn), im(lambda i, j: (i, j))))
        args.append(res)
    if stack is not None and stack[2] is not None:
        aliases = {len(args): 0}
        in_specs.append(ANY)
        args.append(stack[2])
    n_in = len(args)

    def body(*refs):
        a_ref, b_ref, o_ref = refs[0], refs[1], refs[n_in]
        if mode == "nt" and b_sharded:
            ks_ = b_ref.shape[2]
            acc = None
            for s in range(N_CHIPS):
                part = lax.dot_general(a_ref[:, s * ks_:(s + 1) * ks_], b_ref[s], dn, preferred_element_type=F32)
                acc = part if acc is None else acc + part
        else:
            acc = lax.dot_general(a_ref[...], b_ref[...], dn, preferred_element_type=F32)
        if res is not None:
            acc = acc + refs[2][...]
        o_ref[...] = acc.astype(o_ref.dtype)
        if also_bf16:
            refs[n_in + 1][...] = acc.astype(BF16)

    grid = (n // tn, m // tm) if n_outer else (m // tm, n // tn)
    if hook is not None:
        assert not aliases and not also_bf16
        outs, hook_outs = _call(body, name=name, grid=grid, in_specs=in_specs, out_specs=[o_spec], out_shape=[out_shape],
                                args=args, hook=hook)
        return outs[0], hook_outs
    if also_bf16:
        o_spec = [o_spec, o_spec]
        out_shape = [out_shape, jax.ShapeDtypeStruct(out_shape.shape, BF16)]
    return pl.pallas_call(body, name=name, grid=grid, in_specs=in_specs, out_specs=o_spec,
                          out_shape=out_shape, input_output_aliases=aliases, compiler_params=_params(2))(*args)


def _mm_res_norm(a, b, res, g, *, name):
    m, k = a.shape
    n = b.shape[1]
    tm = _pick(m, (1024, 512)) if k <= 1024 else _pick(m, (512, 256))

    def body(a_ref, b_ref, r_ref, g_ref, x_ref, h_ref):
        xv = r_ref[...] + lax.dot_general(a_ref[...], b_ref[...], _DN["nn"], preferred_element_type=F32)
        x_ref[...] = xv
        r = lax.rsqrt(jnp.mean(xv * xv, axis=-1, keepdims=True) + EPS)
        h_ref[...] = (xv * r * g_ref[...]).astype(BF16)

    rows = lambda w: pl.BlockSpec((tm, w), lambda i: (i, 0))
    return pl.pallas_call(
        body, name=name, grid=(m // tm,), in_specs=[rows(k), _full((k, n)), rows(n), _full((1, n))],
        out_specs=[rows(n), rows(n)],
        out_shape=[jax.ShapeDtypeStruct((m, n), F32), jax.ShapeDtypeStruct((m, n), BF16)],
        compiler_params=_params())(a, b, res, _row(g))


def _mm_norm_bwd(a, b4, x, dx_in, g, *, name):
    m, k = a.shape
    n, ks = b4.shape[1], b4.shape[2]
    tm = _pick(m, (512, 256))

    def body(a_ref, b_ref, x_ref, dxin_ref, g_ref, dx_ref, dxb_ref, dg_ref):
        @pl.when(pl.program_id(0) == 0)
        def _():
            dg_ref[...] = jnp.zeros_like(dg_ref)

        dh = None
        for s in range(N_CHIPS):
            part = lax.dot_general(a_ref[:, s * ks:(s + 1) * ks], b_ref[s], _DN["nt"], preferred_element_type=F32)
            dh = part if dh is None else dh + part
        xv = x_ref[...]
        r = lax.rsqrt(jnp.mean(xv * xv, axis=-1, keepdims=True) + EPS)
        xh = xv * r
        dg_ref[...] += jnp.sum(dh * xh, axis=0, keepdims=True)
        dxh = dh * g_ref[...]
        dx = dxin_ref[...] + r * (dxh - xh * jnp.mean(dxh * xh, axis=-1, keepdims=True))
        dx_ref[...] = dx
        dxb_ref[...] = dx.astype(BF16)

    rows = lambda w: pl.BlockSpec((tm, w), lambda i: (i, 0))
    weights = pl.BlockSpec((N_CHIPS, n, ks), lambda i: (0, 0, 0), pipeline_mode=pl.Buffered(1))
    return pl.pallas_call(
        body, name=name, grid=(m // tm,), in_specs=[rows(k), weights, rows(n), rows(n), _full((1, n))],
        out_specs=[rows(n), rows(n), _full((1, n))],
        out_shape=[jax.ShapeDtypeStruct((m, n), F32), jax.ShapeDtypeStruct((m, n), BF16),
                   jax.ShapeDtypeStruct((1, n), F32)],
        compiler_params=pltpu.CompilerParams(dimension_semantics=("arbitrary",), vmem_limit_bytes=VMEM_LIMIT_WIDE))(
            a, b4, x, dx_in, _row(g))


def _phases(offsets):
    return sorted({off % SUBLANES for off in offsets} - {0})


def _phase_rows(offsets, rows):
    return rows + max(off - off % SUBLANES for off in offsets if off % SUBLANES)


def _build_phase_copies(win_ref, copies, n_rows, cols):
    for r, ref in copies.items():
        for c0 in range(0, cols, LANES):
            for r0 in range(0, n_rows, CONV_ROWS):
                n = min(CONV_ROWS, n_rows - r0)
                ref[r0:r0 + n, c0:c0 + LANES] = win_ref[r + r0:r + r0 + n, c0:c0 + LANES]


def _tap(win_ref, copies, off):
    r = off % SUBLANES
    return (win_ref, off) if r == 0 or copies is None else (copies[r], off - r)


def _conv_rows(win_ref, w_ref, offsets, rows, cols, emit, bias_ref=None, copies=None):
    taps = [_tap(win_ref, copies, off) for off in offsets]
    for c0 in range(0, cols, LANES):
        cs = slice(c0, c0 + LANES)
        wk = [w_ref[k:k + 1, cs] for k in range(len(offsets))]
        for r0 in range(0, rows, CONV_ROWS):
            acc = None
            for k, (src, off) in enumerate(taps):
                term = wk[k] * src[off + r0:off + r0 + CONV_ROWS, cs]
                acc = term if acc is None else acc + term
            if bias_ref is not None:
                acc = acc + bias_ref[:, cs]
            emit(r0, c0, acc)


def _conv_wgrad(g_ref, x_ref, offsets, rows, cols, dw_ref, copies=None):
    taps = [_tap(x_ref, copies, off) for off in offsets]
    for c0 in range(0, cols, LANES):
        cs = slice(c0, c0 + LANES)
        for k, (src, off) in enumerate(taps):
            acc = None
            for r0 in range(0, rows, CONV_ROWS):
                term = g_ref[r0:r0 + CONV_ROWS, cs] * src[off + r0:off + r0 + CONV_ROWS, cs]
                acc = term if acc is None else acc + term
            dw_ref[k:k + 1, cs] += jnp.sum(acc, axis=0, keepdims=True)


def _store_into(ref):
    def emit(r0, c0, v):
        ref[r0:r0 + CONV_ROWS, c0:c0 + LANES] = v
    return emit


def _causal_offsets(halo, k):
    return [halo - (k - 1) + j for j in range(k)]


def _anticausal_offsets(k):
    return [(k - 1) - j for j in range(k)]


def _row(v):
    return v.reshape(1, -1)


def _full(shape):
    return pl.BlockSpec(shape, lambda i: (0,) * len(shape))


def _sigmoid(v):
    return jax.nn.sigmoid(v)


def _dsilu(v, sg):
    return sg * (1.0 + v * (1.0 - sg))


def _norm_fwd(x, g, *, name):
    s, d = x.shape
    ts = _pick(s, (TS_NORM, 256, 128))

    def body(x_ref, g_ref, h_ref):
        xv = x_ref[...]
        r = lax.rsqrt(jnp.mean(xv * xv, axis=-1, keepdims=True) + EPS)
        h_ref[...] = (xv * r * g_ref[...]).astype(BF16)

    return pl.pallas_call(body, name=name, grid=(s // ts,),
                          in_specs=[pl.BlockSpec((ts, d), lambda i: (i, 0)), _full((1, d))],
                          out_specs=pl.BlockSpec((ts, d), lambda i: (i, 0)),
                          out_shape=jax.ShapeDtypeStruct((s, d), BF16), compiler_params=_params())(x, _row(g))


def _call(body, *, name, grid, in_specs, out_specs, out_shape, args, scratch_shapes=(), hook=None):
    n_in, n_out, n_scr = len(in_specs), len(out_specs), len(scratch_shapes)
    if hook is None:
        outs = pl.pallas_call(body, name=name, grid=grid, in_specs=in_specs, out_specs=out_specs, out_shape=out_shape,
                              scratch_shapes=list(scratch_shapes), compiler_params=_params(len(grid)))(*args)
        return list(outs), []
    h_in, h_out = len(hook.operands), len(hook.out_shape)

    def hosted(*refs):
        ins, refs = refs[:n_in], refs[n_in:]
        h_ins, refs = refs[:h_in], refs[h_in:]
        outs, refs = refs[:n_out], refs[n_out:]
        h_outs, refs = refs[:h_out], refs[h_out:]
        scr, sems = refs[:n_scr], refs[n_scr:]
        first = last = None
        for d, size in enumerate(grid):
            at_first, at_last = pl.program_id(d) == 0, pl.program_id(d) == size - 1
            first = at_first if first is None else jnp.logical_and(first, at_first)
            last = at_last if last is None else jnp.logical_and(last, at_last)

        @pl.when(first)
        def _():
            hook.start(h_ins, h_outs, sems)

        body(*ins, *outs, *scr)

        @pl.when(last)
        def _():
            hook.finish(h_ins, h_outs, sems)

    outs = pl.pallas_call(
        hosted, name=name, grid=grid, in_specs=list(in_specs) + [ANY] * h_in, out_specs=list(out_specs) + [ANY] * h_out,
        out_shape=list(out_shape) + list(hook.out_shape), scratch_shapes=list(scratch_shapes) + hook.sems(),
        input_output_aliases={n_in + k: n_out + v for k, v in hook.aliases.items()},
        compiler_params=_params(len(grid)))(*args, *hook.operands)
    return list(outs[:n_out]), list(outs[n_out:])


def _mix_pw_fwd(u, b_in, wa, ba, lg, lb, wb, *, name, hook=None):
    s = u.shape[0]
    c = D_CONF
    ts = _pick(s, (TS_MIX, 128))
    offs_a, offs_b = _causal_offsets(HALO_A, K_A), _causal_offsets(HALO_S, K_B)
    ph_a, ph_rows = _phases(offs_a), _phase_rows(offs_a, ts)

    def body(u_ref, bin_ref, wa_ref, ba_ref, lg_ref, lb_ref, wb_ref, y_ref, p_ref, q_ref, e_ref, pwin, ewin, fbuf, pcop):
        i = pl.program_id(0)
        p_copies = {r: pcop.at[n] for n, r in enumerate(ph_a)}

        @pl.when(i == 0)
        def _():
            pwin[0:HALO_A, :] = jnp.zeros((HALO_A, c), F32)
            ewin[0:HALO_S, :] = jnp.zeros((HALO_S, c), F32)

        @pl.when(i > 0)
        def _():
            pwin[0:HALO_A, :] = pwin[ts:ts + HALO_A, :]
            ewin[0:HALO_S, :] = ewin[ts:ts + HALO_S, :]

        a_val = u_ref[:, 0:c] + bin_ref[:, 0:c]
        a_gate = u_ref[:, c:2 * c] + bin_ref[:, c:2 * c]
        p = a_val * _sigmoid(a_gate)
        pwin[HALO_A:HALO_A + ts, :] = p
        p_ref[...] = p
        e = (u_ref[:, 3 * c:4 * c] + bin_ref[:, 3 * c:4 * c]) * (u_ref[:, 4 * c:5 * c] + bin_ref[:, 4 * c:5 * c])
        ewin[HALO_S:HALO_S + ts, :] = e
        e_ref[...] = e
        _build_phase_copies(pwin, p_copies, ph_rows, c)
        _conv_rows(pwin, wa_ref, offs_a, ts, c, _store_into(q_ref), bias_ref=ba_ref, copies=p_copies)
        q = q_ref[...]
        mu = jnp.mean(q, axis=-1, keepdims=True)
        xc = q - mu
        var = jnp.mean(xc * xc, axis=-1, keepdims=True)
        ln = xc * lax.rsqrt(var + EPS) * lg_ref[...] + lb_ref[...]
        y_ref[:, 0:c] = (ln * _sigmoid(ln)).astype(BF16)
        _conv_rows(ewin, wb_ref, offs_b, ts, c, _store_into(fbuf))
        g_b = u_ref[:, 2 * c:3 * c] + bin_ref[:, 2 * c:3 * c]
        y_ref[:, c:2 * c] = (g_b * fbuf[...]).astype(BF16)

    tile = lambda w: pl.BlockSpec((ts, w), lambda i: (i, 0))
    return _call(
        body, name=name, grid=(s // ts,),
        in_specs=[tile(5 * c), _full((1, 5 * c)), _full((K_A, c)), _full((1, c)), _full((1, c)), _full((1, c)),
                  _full((K_B, c))],
        out_specs=[tile(2 * c), tile(c), tile(c), tile(c)],
        out_shape=[jax.ShapeDtypeStruct((s, 2 * c), BF16)] + [jax.ShapeDtypeStruct((s, c), F32)] * 3,
        scratch_shapes=[pltpu.VMEM((ts + HALO_A, c), F32), pltpu.VMEM((ts + HALO_S, c), F32), pltpu.VMEM((ts, c), F32),
                        pltpu.VMEM((len(ph_a), ph_rows, c), F32)],
        args=(u, _row(b_in), wa, _row(ba), _row(lg), _row(lb), wb), hook=hook)


def _ffn_pw_fwd(u2, wf, *, name, hook=None):
    s, f2 = u2.shape
    f = f2 // 2
    ts = _pick(s, (TS_FFN,))
    offs = _causal_offsets(HALO_S, K_F)

    def body(u_ref, wf_ref, z_ref, uwin, cbuf):
        i = pl.program_id(0)

        @pl.when(i == 0)
        def _():
            uwin[0:HALO_S, :] = jnp.zeros((HALO_S, f2), F32)

        @pl.when(i > 0)
        def _():
            uwin[0:HALO_S, :] = uwin[ts:ts + HALO_S, :]

        uwin[HALO_S:HALO_S + ts, :] = u_ref[...]
        _conv_rows(uwin, wf_ref, offs, ts, f2, _store_into(cbuf))
        gate = cbuf[:, 0:f]
        z_ref[...] = (gate * _sigmoid(gate) * cbuf[:, f:f2]).astype(BF16)

    return _call(
        body, name=name, grid=(s // ts,),
        in_specs=[pl.BlockSpec((ts, f2), lambda i: (i, 0)), _full((K_F, f2))],
        out_specs=[pl.BlockSpec((ts, f), lambda i: (i, 0))],
        out_shape=[jax.ShapeDtypeStruct((s, f), BF16)],
        scratch_shapes=[pltpu.VMEM((ts + HALO_S, f2), F32), pltpu.VMEM((ts, f2), F32)],
        args=(u2, wf), hook=hook)


def _loss_bwd(x, tgt, g, *, name):
    s, d = x.shape
    ts = _pick(s, (TS_NORM, 256, 128))

    def body(x_ref, t_ref, g_ref, loss_ref, dx_ref, dxb_ref, dg_ref):
        @pl.when(pl.program_id(0) == 0)
        def _():
            loss_ref[...] = jnp.zeros_like(loss_ref)
            dg_ref[...] = jnp.zeros_like(dg_ref)

        xv = x_ref[...]
        r = lax.rsqrt(jnp.mean(xv * xv, axis=-1, keepdims=True) + EPS)
        xh = xv * r
        err = xh * g_ref[...] - t_ref[...]
        loss_ref[...] += 0.5 * jnp.sum(jnp.mean(err * err, axis=-1, keepdims=True))
        dy = err * (1.0 / d)
        dg_ref[...] += jnp.sum(dy * xh, axis=0, keepdims=True)
        dxh = dy * g_ref[...]
        dx = r * (dxh - xh * jnp.mean(dxh * xh, axis=-1, keepdims=True))
        dx_ref[...] = dx
        dxb_ref[...] = dx.astype(BF16)

    tile = pl.BlockSpec((ts, d), lambda i: (i, 0))
    return pl.pallas_call(
        body, name=name, grid=(s // ts,), in_specs=[tile, tile, _full((1, d))],
        out_specs=[_full((8, LANES)), tile, tile, _full((1, d))],
        out_shape=[jax.ShapeDtypeStruct((8, LANES), F32), jax.ShapeDtypeStruct((s, d), F32),
                   jax.ShapeDtypeStruct((s, d), BF16), jax.ShapeDtypeStruct((1, d), F32)],
        compiler_params=_params())(x, tgt, _row(g))


def _norm_bwd(x, dh, dx_in, g, *, name):
    s, d = x.shape
    ts = _pick(s, (TS_NORM, 256, 128))

    def body(x_ref, dh_ref, dxin_ref, g_ref, dx_ref, dxb_ref, dg_ref):
        @pl.when(pl.program_id(0) == 0)
        def _():
            dg_ref[...] = jnp.zeros_like(dg_ref)

        xv = x_ref[...]
        r = lax.rsqrt(jnp.mean(xv * xv, axis=-1, keepdims=True) + EPS)
        xh = xv * r
        dh_v = dh_ref[...]
        dg_ref[...] += jnp.sum(dh_v * xh, axis=0, keepdims=True)
        dxh = dh_v * g_ref[...]
        dx = dxin_ref[...] + r * (dxh - xh * jnp.mean(dxh * xh, axis=-1, keepdims=True))
        dx_ref[...] = dx
        dxb_ref[...] = dx.astype(BF16)

    tile = pl.BlockSpec((ts, d), lambda i: (i, 0))
    return pl.pallas_call(
        body, name=name, grid=(s // ts,), in_specs=[tile, tile, tile, _full((1, d))],
        out_specs=[tile, tile, _full((1, d))],
        out_shape=[jax.ShapeDtypeStruct((s, d), F32), jax.ShapeDtypeStruct((s, d), BF16),
                   jax.ShapeDtypeStruct((1, d), F32)],
        compiler_params=_params())(x, dh, dx_in, _row(g))


def _ffn_pw_bwd(u2, dz, wf, *, name, hook=None):
    s, f2 = u2.shape
    f = f2 // 2
    ts = _pick(s, (TS_FFN,))
    nt = s // ts
    hb = ts // HALO_S
    offs_c, offs_t = _causal_offsets(HALO_S, K_F), _anticausal_offsets(K_F)
    ph_u, rows_u = _phases(offs_c), _phase_rows(offs_c, ts)

    def body(u_ref, uh_ref, dz_ref, wf_ref, du_ref, dwf_ref, uwin, cbuf, dcwin, ucop):
        i = pl.program_id(0)
        u_copies = {r: ucop.at[n] for n, r in enumerate(ph_u)}

        @pl.when(i == 0)
        def _():
            dwf_ref[...] = jnp.zeros_like(dwf_ref)
            dcwin[ts:ts + HALO_S, :] = jnp.zeros((HALO_S, f2), F32)

        @pl.when(i > 0)
        def _():
            dcwin[ts:ts + HALO_S, :] = dcwin[0:HALO_S, :]

        @pl.when(i == nt - 1)
        def _():
            uwin[0:HALO_S, :] = jnp.zeros((HALO_S, f2), F32)

        @pl.when(i < nt - 1)
        def _():
            uwin[0:HALO_S, :] = uh_ref[...]

        uwin[HALO_S:HALO_S + ts, :] = u_ref[...]
        _build_phase_copies(uwin, u_copies, rows_u, f2)
        _conv_rows(uwin, wf_ref, offs_c, ts, f2, _store_into(cbuf), copies=u_copies)
        gate = cbuf[:, 0:f]
        sg = _sigmoid(gate)
        dz_v = dz_ref[...]
        dcwin[0:ts, 0:f] = dz_v * cbuf[:, f:f2] * _dsilu(gate, sg)
        dcwin[0:ts, f:f2] = dz_v * gate * sg

        def emit(r0, c0, v):
            du_ref[r0:r0 + CONV_ROWS, c0:c0 + LANES] = v.astype(BF16)
        _conv_rows(dcwin, wf_ref, offs_t, ts, f2, emit)
        _conv_wgrad(dcwin, uwin, offs_c, ts, f2, dwf_ref, copies=u_copies)

    rev = lambda i: (nt - 1 - i, 0)
    return _call(
        body, name=name, grid=(nt,),
        in_specs=[pl.BlockSpec((ts, f2), rev),
                  pl.BlockSpec((HALO_S, f2), lambda i: (jnp.maximum((nt - 1 - i) * hb - 1, 0), 0)),
                  pl.BlockSpec((ts, f), rev), _full((K_F, f2))],
        out_specs=[pl.BlockSpec((ts, f2), rev), _full((K_F, f2))],
        out_shape=[jax.ShapeDtypeStruct((s, f2), BF16), jax.ShapeDtypeStruct((K_F, f2), F32)],
        scratch_shapes=[pltpu.VMEM((ts + HALO_S, f2), F32), pltpu.VMEM((ts, f2), F32),
                        pltpu.VMEM((ts + HALO_S, f2), F32), pltpu.VMEM((len(ph_u), rows_u, f2), F32)],
        args=(u2, u2, dz, wf), hook=hook)


def _mix_pw_bwd(dy, u, p, q, e, b_in, wa, lg, lb, wb, *, name, hook=None):
    s = u.shape[0]
    c = D_CONF
    ts = _pick(s, (TS_MIX, 128))
    nt = s // ts
    offs_ac, offs_at = _causal_offsets(HALO_A, K_A), _anticausal_offsets(K_A)
    offs_bc, offs_bt = _causal_offsets(HALO_S, K_B), _anticausal_offsets(K_B)
    ph_p, rows_p = _phases(offs_ac), _phase_rows(offs_ac, ts)
    ph_q, rows_q = _phases(offs_at), _phase_rows(offs_at, ts)

    def body(dy_ref, u_ref, p_ref, ph_ref, q_ref, e_ref, eh_ref, bin_ref, wa_ref, lg_ref, lb_ref, wb_ref,
             du_ref, dbin_ref, dwa_ref, dba_ref, dlg_ref, dlb_ref, dwb_ref, pwin, ewin, dqwin, dfwin, buf, pcop, qcop):
        i = pl.program_id(0)
        p_copies = {r: pcop.at[n] for n, r in enumerate(ph_p)}
        q_copies = {r: qcop.at[n] for n, r in enumerate(ph_q)}

        @pl.when(i == 0)
        def _():
            for r in (dbin_ref, dwa_ref, dba_ref, dlg_ref, dlb_ref, dwb_ref):
                r[...] = jnp.zeros_like(r)
            dqwin[ts:ts + HALO_A, :] = jnp.zeros((HALO_A, c), F32)
            dfwin[ts:ts + HALO_S, :] = jnp.zeros((HALO_S, c), F32)

        @pl.when(i > 0)
        def _():
            dqwin[ts:ts + HALO_A, :] = dqwin[0:HALO_A, :]
            dfwin[ts:ts + HALO_S, :] = dfwin[0:HALO_S, :]

        @pl.when(i == nt - 1)
        def _():
            pwin[0:HALO_A, :] = jnp.zeros((HALO_A, c), F32)
            ewin[0:HALO_S, :] = jnp.zeros((HALO_S, c), F32)

        @pl.when(i < nt - 1)
        def _():
            pwin[0:HALO_A, :] = ph_ref[...]
            ewin[0:HALO_S, :] = eh_ref[...]

        pwin[HALO_A:HALO_A + ts, :] = p_ref[...]
        ewin[HALO_S:HALO_S + ts, :] = e_ref[...]

        def colsum(v):
            return jnp.sum(v, axis=0, keepdims=True)

        q_v = q_ref[...]
        mu = jnp.mean(q_v, axis=-1, keepdims=True)
        xc = q_v - mu
        rstd = lax.rsqrt(jnp.mean(xc * xc, axis=-1, keepdims=True) + EPS)
        nrm = xc * rstd
        ln = nrm * lg_ref[...] + lb_ref[...]
        dln = dy_ref[:, 0:c] * _dsilu(ln, _sigmoid(ln))
        dlg_ref[...] += colsum(dln * nrm)
        dlb_ref[...] += colsum(dln)
        dn = dln * lg_ref[...]
        dq = rstd * (dn - jnp.mean(dn, axis=-1, keepdims=True) - nrm * jnp.mean(dn * nrm, axis=-1, keepdims=True))
        dba_ref[...] += colsum(dq)
        dqwin[0:ts, :] = dq
        _build_phase_copies(dqwin, q_copies, rows_q, c)
        _build_phase_copies(pwin, p_copies, rows_p, c)
        _conv_rows(dqwin, wa_ref, offs_at, ts, c, _store_into(buf), copies=q_copies)
        _conv_wgrad(dqwin, pwin, offs_ac, ts, c, dwa_ref, copies=p_copies)
        dp = buf[...]
        a_val = u_ref[:, 0:c] + bin_ref[:, 0:c]
        sg = _sigmoid(u_ref[:, c:2 * c] + bin_ref[:, c:2 * c])
        d_aval = dp * sg
        d_agate = dp * a_val * sg * (1.0 - sg)
        du_ref[:, 0:c] = d_aval.astype(BF16)
        du_ref[:, c:2 * c] = d_agate.astype(BF16)
        dbin_ref[:, 0:c] += colsum(d_aval)
        dbin_ref[:, c:2 * c] += colsum(d_agate)

        ds = dy_ref[:, c:2 * c]
        _conv_rows(ewin, wb_ref, offs_bc, ts, c, _store_into(buf))
        d_gb = ds * buf[...]
        dfwin[0:ts, :] = ds * (u_ref[:, 2 * c:3 * c] + bin_ref[:, 2 * c:3 * c])
        _conv_rows(dfwin, wb_ref, offs_bt, ts, c, _store_into(buf))
        _conv_wgrad(dfwin, ewin, offs_bc, ts, c, dwb_ref)
        de = buf[...]
        d_gc = de * (u_ref[:, 4 * c:5 * c] + bin_ref[:, 4 * c:5 * c])
        d_v = de * (u_ref[:, 3 * c:4 * c] + bin_ref[:, 3 * c:4 * c])
        du_ref[:, 2 * c:3 * c] = d_gb.astype(BF16)
        du_ref[:, 3 * c:4 * c] = d_gc.astype(BF16)
        du_ref[:, 4 * c:5 * c] = d_v.astype(BF16)
        dbin_ref[:, 2 * c:3 * c] += colsum(d_gb)
        dbin_ref[:, 3 * c:4 * c] += colsum(d_gc)
        dbin_ref[:, 4 * c:5 * c] += colsum(d_v)

    rev = lambda i: (nt - 1 - i, 0)
    tile = lambda w: pl.BlockSpec((ts, w), rev)
    halo = lambda h: pl.BlockSpec((h, c), lambda i: (jnp.maximum((nt - 1 - i) * (ts // h) - 1, 0), 0))
    small = [(1, 5 * c), (K_A, c), (1, c), (1, c), (1, c), (K_B, c)]
    return _call(
        body, name=name, grid=(nt,),
        in_specs=[tile(2 * c), tile(5 * c), tile(c), halo(HALO_A), tile(c), tile(c), halo(HALO_S),
                  _full((1, 5 * c)), _full((K_A, c)), _full((1, c)), _full((1, c)), _full((K_B, c))],
        out_specs=[tile(5 * c)] + [_full(sh) for sh in small],
        out_shape=[jax.ShapeDtypeStruct((s, 5 * c), BF16)] + [jax.ShapeDtypeStruct(sh, F32) for sh in small],
        scratch_shapes=[pltpu.VMEM((ts + HALO_A, c), F32), pltpu.VMEM((ts + HALO_S, c), F32),
                        pltpu.VMEM((ts + HALO_A, c), F32), pltpu.VMEM((ts + HALO_S, c), F32),
                        pltpu.VMEM((ts, c), F32), pltpu.VMEM((len(ph_p), rows_p, c), F32),
                        pltpu.VMEM((len(ph_q), rows_q, c), F32)],
        args=(dy, u, p, p, q, e, e, _row(b_in), wa, _row(lg), _row(lb), wb), hook=hook)


def _adamw(g, w, m, v, *, name):
    r, c = g.shape
    tr = _pick(r, (256, 128, 64, 8)) if r > 256 else r

    def body(g_ref, w_ref, m_ref, v_ref, d_ref, nm_ref, nv_ref):
        gv = g_ref[...]
        nm = ADAM_B1 * m_ref[...] + (1.0 - ADAM_B1) * gv
        nv = ADAM_B2 * v_ref[...] + (1.0 - ADAM_B2) * (gv * gv)
        m_hat = nm / (1.0 - ADAM_B1 ** ADAM_STEP)
        v_hat = nv / (1.0 - ADAM_B2 ** ADAM_STEP)
        d_ref[...] = -ADAM_LR * (m_hat / (jnp.sqrt(v_hat) + ADAM_EPS) + ADAM_WD * w_ref[...])
        nm_ref[...] = nm
        nv_ref[...] = nv

    tile = pl.BlockSpec((tr, c), lambda i: (i, 0))
    return pl.pallas_call(body, name=name, grid=(r // tr,), in_specs=[tile] * 4, out_specs=[tile] * 3,
                          out_shape=[jax.ShapeDtypeStruct((r, c), F32)] * 3, compiler_params=_params())(g, w, m, v)


def _cast_place(wshard, chip, *, name):
    l, r2, c = wshard.shape
    tr = r2 // 2

    def body(chip_ref, w_ref, *o_refs):
        for li, o_ref in enumerate(o_refs):
            o_ref[...] = w_ref[li].astype(BF16)

    grid_spec = pltpu.PrefetchScalarGridSpec(
        num_scalar_prefetch=1, grid=(2,),
        in_specs=[pl.BlockSpec((l, tr, c), lambda i, chip_ref: (0, i, 0))],
        out_specs=[pl.BlockSpec((None, tr, c), lambda i, chip_ref: (chip_ref[0], i, 0))] * l)
    return pl.pallas_call(body, name=name, grid_spec=grid_spec,
                          out_shape=[jax.ShapeDtypeStruct((N_CHIPS, r2, c), BF16)] * l, compiler_params=_params())(chip, wshard)


def _pair_sum(g, recv, core, *, name):
    _, _, r, c = g.shape

    def body(core_ref, g_ref, r_ref, o_ref):
        o_ref[...] = (g_ref[...] + r_ref[...].astype(F32)).astype(BF16)

    grid_spec = pltpu.PrefetchScalarGridSpec(
        num_scalar_prefetch=1, grid=(N_CHIPS,),
        in_specs=[pl.BlockSpec((None, None, r, c), lambda i, core_ref: (i, core_ref[0], 0, 0)),
                  pl.BlockSpec((None, r, c), lambda i, core_ref: (i, 0, 0))],
        out_specs=pl.BlockSpec((None, r, c), lambda i, core_ref: (i, 0, 0)))
    return pl.pallas_call(body, name=name, grid_spec=grid_spec,
                          out_shape=jax.ShapeDtypeStruct((N_CHIPS, r, c), BF16), compiler_params=_params())(core, g, recv)


def _chip_sum(own, recv, core_chip, stack, *, name):
    _, r, c = own.shape
    tr = r // 2
    l, layers, buf = stack

    def body(cc_ref, own_ref, r1_ref, r2_ref, r3_ref, *rest):
        o_ref = rest[-1]
        o_ref[...] = ((own_ref[...].astype(F32) + r1_ref[...].astype(F32)) + r2_ref[...].astype(F32)) + r3_ref[...].astype(F32)

    def slot(s):
        return pl.BlockSpec((None, tr, c), lambda i, cc_ref: ((cc_ref[1] + s) % N_CHIPS, i, 0))

    in_specs = [slot(0), slot(1), slot(2), slot(3)]
    args = [own, recv, recv, recv]
    aliases = {}
    if buf is not None:
        aliases = {1 + len(args): 0}
        in_specs.append(ANY)
        args.append(buf)
    grid_spec = pltpu.PrefetchScalarGridSpec(
        num_scalar_prefetch=1, grid=(2,), in_specs=in_specs,
        out_specs=pl.BlockSpec((None, None, tr, c), lambda i, cc_ref: (l, cc_ref[0], i, 0)))
    return pl.pallas_call(body, name=name, grid_spec=grid_spec, input_output_aliases=aliases,
                          out_shape=jax.ShapeDtypeStruct((layers, 2, r, c), F32), compiler_params=_params())(core_chip, *args)


def _place():
    x, y, c = lax.axis_index("x"), lax.axis_index("y"), lax.axis_index("c")
    chips = [(1 - x, y), (x, 1 - y), (1 - x, 1 - y)]
    return x, y, c, 2 * x + y, chips


def _remote(src, dst, send_sems, recv_sems, k, dev):
    return pltpu.make_async_remote_copy(src_ref=src, dst_ref=dst, send_sem=send_sems.at[k], recv_sem=recv_sems.at[k],
                                        device_id=dev, device_id_type=MESH)


def _sibling_halves(grads, *, name):
    n = len(grads)
    out_shape = [jax.ShapeDtypeStruct(g.shape[:1] + g.shape[2:], g.dtype) for g in grads]

    def body(*refs):
        ins, outs = refs[:n], refs[n:2 * n]
        send_sems, recv_sems = refs[2 * n:]
        x, y, c, _, _ = _place()
        cps = [_remote(ins[a].at[:, 1 - c], outs[a], send_sems, recv_sems, a, (x, y, 1 - c)) for a in range(n)]
        for cp in cps:
            cp.start()
        for cp in cps:
            cp.wait()

    return pl.pallas_call(body, name=name, in_specs=[ANY] * n, out_specs=[ANY] * n, out_shape=out_shape,
                          scratch_shapes=[pltpu.SemaphoreType.DMA((n,)), pltpu.SemaphoreType.DMA((n,))])(*grads)


class _Gather:
    def __init__(self, bufs):
        self.operands = list(bufs)
        self.out_shape = [jax.ShapeDtypeStruct(b.shape, b.dtype) for b in bufs]
        self.aliases = {a: a for a in range(len(bufs))}

    def sems(self):
        n = 6 * len(self.operands)
        return [pltpu.SemaphoreType.DMA((n,)), pltpu.SemaphoreType.DMA((n,))]

    def _sends(self, ins, outs, sems):
        x, y, c, j, chips = _place()
        return [_remote(ins[a].at[j, c], outs[a].at[j, c], sems[0], sems[1], 6 * a + k, (cx, cy, c))
                for a in range(len(ins)) for k, (cx, cy) in enumerate(chips)]

    def start(self, ins, outs, sems):
        for cp in self._sends(ins, outs, sems):
            cp.start()

    def finish(self, ins, outs, sems):
        x, y, c, j, chips = _place()
        sib = (x, y, 1 - c)
        passed = []
        for a in range(len(ins)):
            for k, (cx, cy) in enumerate(chips):
                blk = outs[a].at[2 * cx + cy, c]
                _remote(blk, blk, sems[0], sems[1], 6 * a + k, (cx, cy, c)).wait_recv()
                cp = _remote(blk, blk, sems[0], sems[1], 6 * a + 3 + k, sib)
                cp.start()
                passed.append(cp)
        for a in range(len(ins)):
            for k, (cx, cy) in enumerate(chips):
                blk = outs[a].at[2 * cx + cy, 1 - c]
                _remote(blk, blk, sems[0], sems[1], 6 * a + 3 + k, sib).wait_recv()
        for cp in self._sends(ins, outs, sems) + passed:
            cp.wait_send()


class _Scatter:
    def __init__(self, srcs):
        self.operands = list(srcs)
        self.out_shape = [jax.ShapeDtypeStruct(s.shape, s.dtype) for s in srcs]
        self.aliases = {}

    def sems(self):
        n = 3 * len(self.operands)
        return [pltpu.SemaphoreType.DMA((n,)), pltpu.SemaphoreType.DMA((n,))]

    def _sends(self, ins, outs, sems):
        x, y, c, j, chips = _place()
        return [_remote(ins[a].at[2 * cx + cy], outs[a].at[j], sems[0], sems[1], 3 * a + k, (cx, cy, c))
                for a in range(len(ins)) for k, (cx, cy) in enumerate(chips)]

    def start(self, ins, outs, sems):
        for cp in self._sends(ins, outs, sems):
            cp.start()

    def finish(self, ins, outs, sems):
        x, y, c, j, chips = _place()
        for a in range(len(ins)):
            for k, (cx, cy) in enumerate(chips):
                blk = outs[a].at[2 * cx + cy]
                _remote(blk, blk, sems[0], sems[1], 3 * a + k, (cx, cy, c)).wait_recv()
        for cp in self._sends(ins, outs, sems):
            cp.wait_send()


def _standalone(hook, *, name):
    n_in, n_out = len(hook.operands), len(hook.out_shape)

    def body(*refs):
        ins, outs, sems = refs[:n_in], refs[n_in:n_in + n_out], refs[n_in + n_out:]
        hook.start(ins, outs, sems)
        hook.finish(ins, outs, sems)

    return list(pl.pallas_call(body, name=name, in_specs=[ANY] * n_in, out_specs=[ANY] * n_out,
                               out_shape=hook.out_shape, input_output_aliases=hook.aliases,
                               scratch_shapes=hook.sems())(*hook.operands))


def _share_halves(bufs, *, name):
    n = len(bufs)

    def body(*refs):
        ins, outs = refs[:n], refs[n:2 * n]
        send_sems, recv_sems = refs[2 * n:]
        x, y, c, _, _ = _place()
        sends = [_remote(ins[a].at[:, c], outs[a].at[:, c], send_sems, recv_sems, a, (x, y, 1 - c)) for a in range(n)]
        for cp in sends:
            cp.start()
        for a in range(n):
            blk = outs[a].at[:, 1 - c]
            _remote(blk, blk, send_sems, recv_sems, a, (x, y, 1 - c)).wait_recv()
        for cp in sends:
            cp.wait_send()

    return pl.pallas_call(
        body, name=name, in_specs=[ANY] * n, out_specs=[ANY] * n,
        out_shape=[jax.ShapeDtypeStruct(b.shape, b.dtype) for b in bufs],
        input_output_aliases={a: a for a in range(n)},
        scratch_shapes=[pltpu.SemaphoreType.DMA((n,)), pltpu.SemaphoreType.DMA((n,))])(*bufs)


def _allreduce_small(v, *, name):
    r, c = v.shape
    n_dev = 8

    def body(v_ref, o_ref, buf, send_sems, recv_sems):
        x, y, cc = lax.axis_index("x"), lax.axis_index("y"), lax.axis_index("c")
        me = 4 * x + 2 * y + cc
        cps = []
        for d in range(1, n_dev):
            peer = (1 - x if d & 4 else x, 1 - y if d & 2 else y, 1 - cc if d & 1 else cc)
            cps.append(_remote(v_ref, buf.at[me], send_sems, recv_sems, d - 1, peer))
        for cp in cps:
            cp.start()
        buf[me] = v_ref[...]
        for cp in cps:
            cp.wait()
        acc = buf[0]
        for d in range(1, n_dev):
            acc = acc + buf[d]
        o_ref[...] = acc

    return pl.pallas_call(
        body, name=name, in_specs=[VMEM], out_specs=VMEM, out_shape=jax.ShapeDtypeStruct((r, c), F32),
        scratch_shapes=[pltpu.VMEM((n_dev, r, c), F32), pltpu.SemaphoreType.DMA((n_dev - 1,)),
                        pltpu.SemaphoreType.DMA((n_dev - 1,))])(v)


def _pack(arrays, row_multiple=8):
    flat = jnp.concatenate([a.reshape(-1) for a in arrays])
    rows = -(-flat.shape[0] // LANES)
    rows = -(-rows // row_multiple) * row_multiple
    return jnp.pad(flat, (0, rows * LANES - flat.shape[0])).reshape(rows, LANES)


def _unpack(slab, shapes):
    flat = slab.reshape(-1)
    out, pos = [], 0
    for sh in shapes:
        size = 1
        for dim in sh:
            size *= dim
        out.append(flat[pos:pos + size].reshape(sh))
        pos += size
    return out


def kernel(x, mix_norm_g, w_in, b_in, conv_a_w, conv_a_b, ln_a_g, ln_a_b, conv_b_w, w_out, ffn_norm_g, w_up, conv_f_w, w_down, final_norm_g, loss_target, m_mix_norm_g, m_w_in, m_b_in, m_conv_a_w, m_conv_a_b, m_ln_a_g, m_ln_a_b, m_conv_b_w, m_w_out, m_ffn_norm_g, m_w_up, m_conv_f_w, m_w_down, m_final_norm_g, v_mix_norm_g, v_w_in, v_b_in, v_conv_a_w, v_conv_a_b, v_ln_a_g, v_ln_a_b, v_conv_b_w, v_w_out, v_ffn_norm_g, v_w_up, v_conv_f_w, v_w_down, v_final_norm_g):
    w = dict(mix_norm_g=mix_norm_g, w_in=w_in, b_in=b_in, conv_a_w=conv_a_w, conv_a_b=conv_a_b, ln_a_g=ln_a_g,
             ln_a_b=ln_a_b, conv_b_w=conv_b_w, w_out=w_out, ffn_norm_g=ffn_norm_g, w_up=w_up, conv_f_w=conv_f_w,
             w_down=w_down, final_norm_g=final_norm_g)
    m = dict(mix_norm_g=m_mix_norm_g, w_in=m_w_in, b_in=m_b_in, conv_a_w=m_conv_a_w, conv_a_b=m_conv_a_b,
             ln_a_g=m_ln_a_g, ln_a_b=m_ln_a_b, conv_b_w=m_conv_b_w, w_out=m_w_out, ffn_norm_g=m_ffn_norm_g,
             w_up=m_w_up, conv_f_w=m_conv_f_w, w_down=m_w_down, final_norm_g=m_final_norm_g)
    v = dict(mix_norm_g=v_mix_norm_g, w_in=v_w_in, b_in=v_b_in, conv_a_w=v_conv_a_w, conv_a_b=v_conv_a_b,
             ln_a_g=v_ln_a_g, ln_a_b=v_ln_a_b, conv_b_w=v_conv_b_w, w_out=v_w_out, ffn_norm_g=v_ffn_norm_g,
             w_up=v_w_up, conv_f_w=v_conv_f_w, w_down=v_w_down, final_norm_g=v_final_norm_g)
    depth = w_in.shape[0]
    xs = x[0]
    tgt = loss_target[0]
    chip = (2 * lax.axis_index("x") + lax.axis_index("y")).astype(jnp.int32)
    chip_arr = chip.reshape(1)
    core = lax.axis_index("c").astype(jnp.int32).reshape(1)
    core_chip = jnp.concatenate([core, chip_arr])

    def halves(b):
        return b.reshape(N_CHIPS, 2, b.shape[1] // 2, b.shape[2])

    conv_shapes = [w[nm].shape for nm in CHANNEL_SHARDED]
    conv_slab = _pack([w[nm] for nm in CHANNEL_SHARDED], row_multiple=16)
    zero = jnp.int32(0)
    slab_buf = lax.dynamic_update_slice(jnp.zeros((N_CHIPS,) + conv_slab.shape, F32), conv_slab[None], (chip, zero, zero))
    wbuf = {}
    for nm in BIG:
        for l, b in enumerate(_cast_place(w[nm], chip_arr, name=f"cast_{nm}")):
            wbuf[nm, l] = halves(b)

    def gather_into(keys, done):
        for key, g in zip(keys, done):
            wbuf[key] = g

    def wmat(nm, l):
        g = wbuf[nm, l]
        if nm in ("w_in", "w_up"):
            return g.reshape(N_CHIPS, 2 * g.shape[2], g.shape[3])
        return g.reshape(N_CHIPS * 2 * g.shape[2], g.shape[3])

    first = [("w_in", 0)]
    done = _standalone(_Gather([wbuf[k] for k in first] + [halves(slab_buf)]), name="gather_first")
    gather_into(first, done)
    conv_full = done[-1].reshape(N_CHIPS, -1, LANES)
    taps = {}
    for nm, parts in zip(CHANNEL_SHARDED, zip(*[_unpack(conv_full[jj], conv_shapes) for jj in range(N_CHIPS)])):
        taps[nm] = jnp.concatenate(parts, axis=-1)

    assert depth == 2
    plan = {("mm_in", 0): [("w_out", 0)], ("mix_fwd", 0): [("w_up", 0)], ("mm_up", 0): [("w_down", 0), ("w_in", 1)],
            ("ffn_fwd", 0): [("w_out", 1), ("w_up", 1)], ("mix_fwd", 1): [("w_down", 1)]}

    def host(kind, l, run):
        keys = plan.get((kind, l), [])
        results, done = run(_Gather([wbuf[k] for k in keys]) if keys else None)
        gather_into(keys, done)
        return results

    saved = []
    cur = xs
    h1 = _norm_fwd(cur, w["mix_norm_g"][0], name="norm_mix_0")
    for l in range(depth):
        if ("mm_in", l) in plan:
            u = host("mm_in", l, lambda hook: _mm(h1, wmat("w_in", l), mode="nn", b_sharded=True, name=f"mm_in_{l}",
                                                  hook=hook))
        else:
            u = _mm(h1, wmat("w_in", l), mode="nn", b_sharded=True, name=f"mm_in_{l}")
        y, p, q, e = host("mix_fwd", l, lambda hook: _mix_pw_fwd(
            u, w["b_in"][l], taps["conv_a_w"][l], w["conv_a_b"][l], w["ln_a_g"][l], w["ln_a_b"][l],
            taps["conv_b_w"][l], name=f"mix_fwd_{l}", hook=hook))
        x1, h2 = _mm_res_norm(y, wmat("w_out", l), cur, w["ffn_norm_g"][l], name=f"mm_out_{l}")
        if ("mm_up", l) in plan:
            u2 = host("mm_up", l, lambda hook: _mm(h2, wmat("w_up", l), mode="nn", b_sharded=True, name=f"mm_up_{l}",
                                                   hook=hook))
        else:
            u2 = _mm(h2, wmat("w_up", l), mode="nn", b_sharded=True, name=f"mm_up_{l}")
        (z,) = host("ffn_fwd", l, lambda hook: _ffn_pw_fwd(u2, taps["conv_f_w"][l], name=f"ffn_fwd_{l}", hook=hook))
        saved.append((cur, h1, u, y, p, q, e, x1, h2, u2, z))
        if l + 1 < depth:
            cur, h1 = _mm_res_norm(z, wmat("w_down", l), x1, w["mix_norm_g"][l + 1], name=f"mm_down_{l}")
        else:
            cur = _mm(z, wmat("w_down", l), mode="nn", res=x1, name=f"mm_down_{l}")

    loss_blk, dx, dxb, dgf = _loss_bwd(cur, tgt, w["final_norm_g"], name="loss_bwd")
    loss = lax.psum(loss_blk[0, 0], ("x", "y", "c"))

    gfull, chip_sums, from_chips = {}, {}, {}

    def four_d(g):
        if g.ndim == 2:
            g = g.reshape(N_CHIPS, g.shape[0] // N_CHIPS, g.shape[1])
        return halves(g)

    def scatter_hook(keys, tag):
        from_sibling = _sibling_halves([four_d(gfull[k][1]) for k in keys], name=f"reduce_sibling_{tag}")
        for k, r in zip(keys, from_sibling):
            chip_sums[k] = _pair_sum(four_d(gfull[k][0]), r, core, name=f"pair_sum_{k[0]}_{k[1]}")
        return _Scatter([chip_sums[k] for k in keys])

    sg = {nm: [None] * depth for nm in SMALL if nm != "final_norm_g"}
    pending = []
    for l in reversed(range(depth)):
        x0, h1, u, y, p, q, e, x1, h2, u2, z = saved[l]
        dz = _mm(dxb, wmat("w_down", l), mode="nt", name=f"mm_ddown_{l}")
        gfull["w_down", l] = _mm(z, dxb, mode="tn", also_bf16=True, name=f"mm_gdown_{l}")
        keys = pending + [("w_down", l)]
        (du2, dwf), done = _ffn_pw_bwd(u2, dz, taps["conv_f_w"][l], name=f"ffn_bwd_{l}", hook=scatter_hook(keys, f"a{l}"))
        from_chips.update(zip(keys, done))
        dx1, dx1b, dg2 = _mm_norm_bwd(du2, wmat("w_up", l), x1, dx, w["ffn_norm_g"][l], name=f"mm_dup_{l}")
        gfull["w_up", l] = _mm(h2, du2, mode="tn", out_sharded=True, also_bf16=True, name=f"mm_gup_{l}")
        dy = _mm(dx1b, wmat("w_out", l), mode="nt", name=f"mm_dout_{l}")
        gfull["w_out", l] = _mm(y, dx1b, mode="tn", also_bf16=True, name=f"mm_gout_{l}")
        keys = [("w_up", l), ("w_out", l)]
        (du, dbin, dwa, dba, dlg, dlb, dwb), done = _mix_pw_bwd(
            dy, u, p, q, e, w["b_in"][l], taps["conv_a_w"][l], w["ln_a_g"][l], w["ln_a_b"][l],
            taps["conv_b_w"][l], name=f"mix_bwd_{l}", hook=scatter_hook(keys, f"b{l}"))
        from_chips.update(zip(keys, done))
        dx, dxb, dg1 = _mm_norm_bwd(du, wmat("w_in", l), x0, dx1, w["mix_norm_g"][l], name=f"mm_din_{l}")
        gfull["w_in", l] = _mm(h1, du, mode="tn", out_sharded=True, also_bf16=True, name=f"mm_gin_{l}")
        pending = [("w_in", l)]
        for nm, g in (("mix_norm_g", dg1), ("b_in", dbin), ("conv_a_w", dwa), ("conv_a_b", dba), ("ln_a_g", dlg),
                      ("ln_a_b", dlb), ("conv_b_w", dwb), ("ffn_norm_g", dg2), ("conv_f_w", dwf)):
            sg[nm][l] = g.reshape(g.shape[-1]) if g.shape[0] == 1 else g
    grad_x = dx[None]
    from_chips.update(zip(pending, _standalone(scatter_hook(pending, "last"), name="reduce_chips_last")))

    mine = []
    for nm in BIG:
        buf = None
        for l in range(depth):
            buf = _chip_sum(chip_sums[nm, l], from_chips[nm, l], core_chip, (l, depth, buf), name=f"chip_sum_{nm}_{l}")
        mine.append(buf)
    shared = _share_halves(mine, name="share_halves")
    grads = {}
    for nm, g in zip(BIG, shared):
        grads[nm] = g.reshape(w[nm].shape)

    small_full_shapes = []
    small_parts = []
    for nm in SMALL:
        if nm == "final_norm_g":
            g = dgf.reshape(-1)
        else:
            g = jnp.stack(sg[nm])
        small_parts.append(g)
        small_full_shapes.append(g.shape)
    summed = _unpack(_allreduce_small(_pack(small_parts), name="reduce_small"), small_full_shapes)
    for nm, g in zip(SMALL, summed):
        if nm in CHANNEL_SHARDED:
            width = w[nm].shape[-1]
            g = lax.dynamic_slice_in_dim(g, chip * width, width, axis=2)
        grads[nm] = g

    delta, new_m, new_v = {}, {}, {}
    for nm in BIG:
        sh = w[nm].shape
        two_d = lambda a: a.reshape(sh[0] * sh[1], sh[2])
        d_, m_, v_ = _adamw(two_d(grads[nm]), two_d(w[nm]), two_d(m[nm]), two_d(v[nm]), name=f"adamw_{nm}")
        delta[nm], new_m[nm], new_v[nm] = d_.reshape(sh), m_.reshape(sh), v_.reshape(sh)
    small_shapes = [w[nm].shape for nm in SMALL]
    outs = _adamw(_pack([grads[nm] for nm in SMALL]), _pack([w[nm] for nm in SMALL]), _pack([m[nm] for nm in SMALL]),
                  _pack([v[nm] for nm in SMALL]), name="adamw_small")
    for store, slab in zip((delta, new_m, new_v), outs):
        for nm, a in zip(SMALL, _unpack(slab, small_shapes)):
            store[nm] = a

    return (loss, grad_x, *[grads[nm] for nm in TWIN_WEIGHTS], *[delta[nm] for nm in TWIN_WEIGHTS],
            *[new_m[nm] for nm in TWIN_WEIGHTS], *[new_v[nm] for nm in TWIN_WEIGHTS])
```

```python
import functools

import jax
import jax.numpy as jnp
from jax import lax
from jax.experimental import pallas as pl
from jax.experimental.pallas import tpu as pltpu

F32 = jnp.float32
BF16 = jnp.bfloat16
MESH = pl.DeviceIdType.MESH
ANY = pl.BlockSpec(memory_space=pl.ANY)
VMEM = pl.BlockSpec(memory_space=pltpu.VMEM)

EPS = 1e-6
ADAM_LR, ADAM_B1, ADAM_B2, ADAM_EPS, ADAM_WD, ADAM_STEP = 0.001, 0.9, 0.999, 1e-08, 0.01, 10

D_CONF = 512
K_A, K_B, K_F = 31, 3, 3
HALO_A, HALO_S = 32, 8
N_CHIPS = 4
LANES = 128
SUBLANES = 8
CONV_ROWS = 128
TS_MIX, TS_FFN, TS_NORM = 256, 128, 512
VMEM_LIMIT = 48 * 1024 * 1024
VMEM_LIMIT_WIDE = 56 * 1024 * 1024

TWIN_WEIGHTS = ['mix_norm_g', 'w_in', 'b_in', 'conv_a_w', 'conv_a_b', 'ln_a_g', 'ln_a_b', 'conv_b_w', 'w_out',
                'ffn_norm_g', 'w_up', 'conv_f_w', 'w_down', 'final_norm_g']
BIG = ['w_in', 'w_out', 'w_up', 'w_down']
SMALL = [n for n in TWIN_WEIGHTS if n not in BIG]
CHANNEL_SHARDED = ['conv_a_w', 'conv_b_w', 'conv_f_w']


def _params(n_grid=1, vmem_limit=VMEM_LIMIT):
    return pltpu.CompilerParams(dimension_semantics=("arbitrary",) * n_grid, vmem_limit_bytes=vmem_limit)


def _pick(n, cands):
    for c in cands:
        if n % c == 0:
            return c
    raise ValueError(f"no tile for {n}")


_DN = {"nn": (((1,), (0,)), ((), ())), "nt": (((1,), (1,)), ((), ())), "tn": (((0,), (0,)), ((), ()))}


def _mm(a, b, *, mode, name, layer=None, b_sharded=False, out_sharded=False, res=None, stack=None, out_dtype=F32,
        also_bf16=False, hook=None):
    dn = _DN[mode]
    lead = () if layer is None else (None,)
    pre = () if layer is None else (layer,)
    wide = (1408, 1280, 1024, 512)
    if mode == "tn":
        k, m = a.shape
        n = b.shape[1]
        tm = _pick(m, (512, 256))
        tn = n // N_CHIPS if out_sharded else _pick(n, wide)
    else:
        m, k = a.shape
        tm = _pick(m, (1024, 512, 256, 128)) if k <= 1024 else _pick(m, (512, 256, 128))
        if mode == "nn":
            n = b.shape[-1] * N_CHIPS if b_sharded else b.shape[-1]
            tn = b.shape[-1] if b_sharded else _pick(n, wide)
        else:
            n = b.shape[-2]
            tn = _pick(n, (1024, 512, 256)) if b_sharded else _pick(n, wide)
    n_outer = (n // tn - 1) * a.size < (m // tm - 1) * (b.size if layer is None else b.size // b.shape[0])

    def im(f):
        return (lambda g0, g1: f(g1, g0)) if n_outer else f

    if mode == "tn":
        a_spec = pl.BlockSpec((k, tm), im(lambda i, j: (0, i)))
        b_spec = pl.BlockSpec((k, tn), im(lambda i, j: (0, j)))
    else:
        a_spec = pl.BlockSpec((tm, k), im(lambda i, j: (i, 0)))
        if mode == "nn" and b_sharded:
            b_spec = pl.BlockSpec(lead + (None, k, tn), im(lambda i, j: pre + (j, 0, 0)))
        elif mode == "nn":
            b_spec = pl.BlockSpec(lead + (k, tn), im(lambda i, j: pre + (0, j)))
        elif b_sharded:
            b_spec = pl.BlockSpec(lead + (N_CHIPS, tn, b.shape[-1]), im(lambda i, j: pre + (0, j, 0)))
        else:
            b_spec = pl.BlockSpec(lead + (tn, k), im(lambda i, j: pre + (j, 0)))
    slot_lead, slot_pre, slot_shape = (), (), ()
    if stack is not None:
        slot_lead, slot_pre, slot_shape = (None,), (stack[0],), (stack[1],)
    if out_sharded:
        out_shape = jax.ShapeDtypeStruct(slot_shape + (N_CHIPS, m, tn), out_dtype)
        o_spec = pl.BlockSpec(slot_lead + (None, tm, tn), im(lambda i, j: slot_pre + (j, i, 0)))
    else:
        out_shape = jax.ShapeDtypeStruct(slot_shape + (m, n), out_dtype)
        o_spec = pl.BlockSpec(slot_lead + (tm, tn), im(lambda i, j: slot_pre + (i, j)))
    in_specs = [a_spec, b_spec]
    args = [a, b]
    aliases = {}
    if res is not None:
        in_specs.append(pl.BlockSpec((tm, tn), im(lambda i, j: (i, j))))
        args.append(res)
    if stack is not None and stack[2] is not None:
        aliases = {len(args): 0}
        in_specs.append(ANY)
        args.append(stack[2])
    n_in = len(args)

    def body(*refs):
        a_ref, b_ref, o_ref = refs[0], refs[1], refs[n_in]
        if mode == "nt" and b_sharded:
            ks_ = b_ref.shape[2]
            acc = None
            for s in range(N_CHIPS):
                part = lax.dot_general(a_ref[:, s * ks_:(s + 1) * ks_], b_ref[s], dn, preferred_element_type=F32)
                acc = part if acc is None else acc + part
        else:
            acc = lax.dot_general(a_ref[...], b_ref[...], dn, preferred_element_type=F32)
        if res is not None:
            acc = acc + refs[2][...]
        o_ref[...] = acc.astype(o_ref.dtype)
        if also_bf16:
            refs[n_in + 1][...] = acc.astype(BF16)

    grid = (n // tn, m // tm) if n_outer else (m // tm, n // tn)
    if hook is not None:
        assert not aliases and not also_bf16
        outs, hook_outs = _call(body, name=name, grid=grid, in_specs=in_specs, out_specs=[o_spec], out_shape=[out_shape],
                                args=args, hook=hook)
        return outs[0], hook_outs
    if also_bf16:
        o_spec = [o_spec, o_spec]
        out_shape = [out_shape, jax.ShapeDtypeStruct(out_shape.shape, BF16)]
    return pl.pallas_call(body, name=name, grid=grid, in_specs=in_specs, out_specs=o_spec,
                          out_shape=out_shape, input_output_aliases=aliases, compiler_params=_params(2))(*args)


def _mm_res_norm(a, b, res, g, *, name):
    m, k = a.shape
    n = b.shape[1]
    tm = _pick(m, (1024, 512)) if k <= 1024 else _pick(m, (512, 256))

    def body(a_ref, b_ref, r_ref, g_ref, x_ref, h_ref):
        xv = r_ref[...] + lax.dot_general(a_ref[...], b_ref[...], _DN["nn"], preferred_element_type=F32)
        x_ref[...] = xv
        r = lax.rsqrt(jnp.mean(xv * xv, axis=-1, keepdims=True) + EPS)
        h_ref[...] = (xv * r * g_ref[...]).astype(BF16)

    rows = lambda w: pl.BlockSpec((tm, w), lambda i: (i, 0))
    return pl.pallas_call(
        body, name=name, grid=(m // tm,), in_specs=[rows(k), _full((k, n)), rows(n), _full((1, n))],
        out_specs=[rows(n), rows(n)],
        out_shape=[jax.ShapeDtypeStruct((m, n), F32), jax.ShapeDtypeStruct((m, n), BF16)],
        compiler_params=_params())(a, b, res, _row(g))


def _mm_norm_bwd(a, b4, x, dx_in, g, *, name, hook=None):
    m, k = a.shape
    n, ks = b4.shape[1], b4.shape[2]
    tm = _pick(m, (512, 256))

    def body(a_ref, b_ref, x_ref, dxin_ref, g_ref, dx_ref, dxb_ref, dg_ref):
        @pl.when(pl.program_id(0) == 0)
        def _():
            dg_ref[...] = jnp.zeros_like(dg_ref)

        dh = None
        for s in range(N_CHIPS):
            part = lax.dot_general(a_ref[:, s * ks:(s + 1) * ks], b_ref[s], _DN["nt"], preferred_element_type=F32)
            dh = part if dh is None else dh + part
        xv = x_ref[...]
        r = lax.rsqrt(jnp.mean(xv * xv, axis=-1, keepdims=True) + EPS)
        xh = xv * r
        dg_ref[...] += jnp.sum(dh * xh, axis=0, keepdims=True)
        dxh = dh * g_ref[...]
        dx = dxin_ref[...] + r * (dxh - xh * jnp.mean(dxh * xh, axis=-1, keepdims=True))
        dx_ref[...] = dx
        dxb_ref[...] = dx.astype(BF16)

    rows = lambda w: pl.BlockSpec((tm, w), lambda i: (i, 0))
    weights = pl.BlockSpec((N_CHIPS, n, ks), lambda i: (0, 0, 0), pipeline_mode=pl.Buffered(1))
    return _call(
        body, name=name, grid=(m // tm,), in_specs=[rows(k), weights, rows(n), rows(n), _full((1, n))],
        out_specs=[rows(n), rows(n), _full((1, n))],
        out_shape=[jax.ShapeDtypeStruct((m, n), F32), jax.ShapeDtypeStruct((m, n), BF16),
                   jax.ShapeDtypeStruct((1, n), F32)],
        args=(a, b4, x, dx_in, _row(g)), hook=hook, vmem_limit=VMEM_LIMIT_WIDE)


def _phases(offsets):
    return sorted({off % SUBLANES for off in offsets} - {0})


def _phase_rows(offsets, rows):
    return rows + max(off - off % SUBLANES for off in offsets if off % SUBLANES)


def _build_phase_copies(win_ref, copies, n_rows, cols):
    for r, ref in copies.items():
        for c0 in range(0, cols, LANES):
            for r0 in range(0, n_rows, CONV_ROWS):
                n = min(CONV_ROWS, n_rows - r0)
                ref[r0:r0 + n, c0:c0 + LANES] = win_ref[r + r0:r + r0 + n, c0:c0 + LANES]


def _tap(win_ref, copies, off):
    r = off % SUBLANES
    return (win_ref, off) if r == 0 or copies is None else (copies[r], off - r)


def _conv_rows(win_ref, w_ref, offsets, rows, cols, emit, bias_ref=None, copies=None):
    taps = [_tap(win_ref, copies, off) for off in offsets]
    for c0 in range(0, cols, LANES):
        cs = slice(c0, c0 + LANES)
        wk = [w_ref[k:k + 1, cs] for k in range(len(offsets))]
        for r0 in range(0, rows, CONV_ROWS):
            acc = None
            for k, (src, off) in enumerate(taps):
                term = wk[k] * src[off + r0:off + r0 + CONV_ROWS, cs]
                acc = term if acc is None else acc + term
            if bias_ref is not None:
                acc = acc + bias_ref[:, cs]
            emit(r0, c0, acc)


def _conv_wgrad(g_ref, x_ref, offsets, rows, cols, dw_ref, copies=None):
    taps = [_tap(x_ref, copies, off) for off in offsets]
    for c0 in range(0, cols, LANES):
        cs = slice(c0, c0 + LANES)
        for k, (src, off) in enumerate(taps):
            acc = None
            for r0 in range(0, rows, CONV_ROWS):
                term = g_ref[r0:r0 + CONV_ROWS, cs] * src[off + r0:off + r0 + CONV_ROWS, cs]
                acc = term if acc is None else acc + term
            dw_ref[k:k + 1, cs] += jnp.sum(acc, axis=0, keepdims=True)


def _store_into(ref):
    def emit(r0, c0, v):
        ref[r0:r0 + CONV_ROWS, c0:c0 + LANES] = v
    return emit


def _causal_offsets(halo, k):
    return [halo - (k - 1) + j for j in range(k)]


def _anticausal_offsets(k):
    return [(k - 1) - j for j in range(k)]


def _row(v):
    return v.reshape(1, -1)


def _full(shape):
    return pl.BlockSpec(shape, lambda i: (0,) * len(shape))


def _sigmoid(v):
    return jax.nn.sigmoid(v)


def _dsilu(v, sg):
    return sg * (1.0 + v * (1.0 - sg))


def _norm_fwd(x, g, *, name):
    s, d = x.shape
    ts = _pick(s, (TS_NORM, 256, 128))

    def body(x_ref, g_ref, h_ref):
        xv = x_ref[...]
        r = lax.rsqrt(jnp.mean(xv * xv, axis=-1, keepdims=True) + EPS)
        h_ref[...] = (xv * r * g_ref[...]).astype(BF16)

    return pl.pallas_call(body, name=name, grid=(s // ts,),
                          in_specs=[pl.BlockSpec((ts, d), lambda i: (i, 0)), _full((1, d))],
                          out_specs=pl.BlockSpec((ts, d), lambda i: (i, 0)),
                          out_shape=jax.ShapeDtypeStruct((s, d), BF16), compiler_params=_params())(x, _row(g))


def _call(body, *, name, grid, in_specs, out_specs, out_shape, args, scratch_shapes=(), hook=None, vmem_limit=VMEM_LIMIT):
    n_in, n_out, n_scr = len(in_specs), len(out_specs), len(scratch_shapes)
    if hook is None:
        outs = pl.pallas_call(body, name=name, grid=grid, in_specs=in_specs, out_specs=out_specs, out_shape=out_shape,
                              scratch_shapes=list(scratch_shapes), compiler_params=_params(len(grid), vmem_limit))(*args)
        return list(outs), []
    h_in, h_out = len(hook.operands), len(hook.out_shape)

    def hosted(*refs):
        ins, refs = refs[:n_in], refs[n_in:]
        h_ins, refs = refs[:h_in], refs[h_in:]
        outs, refs = refs[:n_out], refs[n_out:]
        h_outs, refs = refs[:h_out], refs[h_out:]
        scr, sems = refs[:n_scr], refs[n_scr:]
        first = last = None
        for d, size in enumerate(grid):
            at_first, at_last = pl.program_id(d) == 0, pl.program_id(d) == size - 1
            first = at_first if first is None else jnp.logical_and(first, at_first)
            last = at_last if last is None else jnp.logical_and(last, at_last)

        @pl.when(first)
        def _():
            hook.start(h_ins, h_outs, sems)

        body(*ins, *outs, *scr)

        @pl.when(last)
        def _():
            hook.finish(h_ins, h_outs, sems)

    outs = pl.pallas_call(
        hosted, name=name, grid=grid, in_specs=list(in_specs) + [ANY] * h_in, out_specs=list(out_specs) + [ANY] * h_out,
        out_shape=list(out_shape) + list(hook.out_shape), scratch_shapes=list(scratch_shapes) + hook.sems(),
        input_output_aliases={n_in + k: n_out + v for k, v in hook.aliases.items()},
        compiler_params=_params(len(grid), vmem_limit))(*args, *hook.operands)
    return list(outs[:n_out]), list(outs[n_out:])


def _mix_pw_fwd(u, b_in, wa, ba, lg, lb, wb, *, name, hook=None):
    s = u.shape[0]
    c = D_CONF
    ts = _pick(s, (TS_MIX, 128))
    offs_a, offs_b = _causal_offsets(HALO_A, K_A), _causal_offsets(HALO_S, K_B)
    ph_a, ph_rows = _phases(offs_a), _phase_rows(offs_a, ts)

    def body(u_ref, bin_ref, wa_ref, ba_ref, lg_ref, lb_ref, wb_ref, y_ref, p_ref, q_ref, e_ref, pwin, ewin, fbuf, pcop):
        i = pl.program_id(0)
        p_copies = {r: pcop.at[n] for n, r in enumerate(ph_a)}

        @pl.when(i == 0)
        def _():
            pwin[0:HALO_A, :] = jnp.zeros((HALO_A, c), F32)
            ewin[0:HALO_S, :] = jnp.zeros((HALO_S, c), F32)

        @pl.when(i > 0)
        def _():
            pwin[0:HALO_A, :] = pwin[ts:ts + HALO_A, :]
            ewin[0:HALO_S, :] = ewin[ts:ts + HALO_S, :]

        a_val = u_ref[:, 0:c] + bin_ref[:, 0:c]
        a_gate = u_ref[:, c:2 * c] + bin_ref[:, c:2 * c]
        p = a_val * _sigmoid(a_gate)
        pwin[HALO_A:HALO_A + ts, :] = p
        p_ref[...] = p
        e = (u_ref[:, 3 * c:4 * c] + bin_ref[:, 3 * c:4 * c]) * (u_ref[:, 4 * c:5 * c] + bin_ref[:, 4 * c:5 * c])
        ewin[HALO_S:HALO_S + ts, :] = e
        e_ref[...] = e
        _build_phase_copies(pwin, p_copies, ph_rows, c)
        _conv_rows(pwin, wa_ref, offs_a, ts, c, _store_into(q_ref), bias_ref=ba_ref, copies=p_copies)
        q = q_ref[...]
        mu = jnp.mean(q, axis=-1, keepdims=True)
        xc = q - mu
        var = jnp.mean(xc * xc, axis=-1, keepdims=True)
        ln = xc * lax.rsqrt(var + EPS) * lg_ref[...] + lb_ref[...]
        y_ref[:, 0:c] = (ln * _sigmoid(ln)).astype(BF16)
        _conv_rows(ewin, wb_ref, offs_b, ts, c, _store_into(fbuf))
        g_b = u_ref[:, 2 * c:3 * c] + bin_ref[:, 2 * c:3 * c]
        y_ref[:, c:2 * c] = (g_b * fbuf[...]).astype(BF16)

    tile = lambda w: pl.BlockSpec((ts, w), lambda i: (i, 0))
    return _call(
        body, name=name, grid=(s // ts,),
        in_specs=[tile(5 * c), _full((1, 5 * c)), _full((K_A, c)), _full((1, c)), _full((1, c)), _full((1, c)),
                  _full((K_B, c))],
        out_specs=[tile(2 * c), tile(c), tile(c), tile(c)],
        out_shape=[jax.ShapeDtypeStruct((s, 2 * c), BF16)] + [jax.ShapeDtypeStruct((s, c), F32)] * 3,
        scratch_shapes=[pltpu.VMEM((ts + HALO_A, c), F32), pltpu.VMEM((ts + HALO_S, c), F32), pltpu.VMEM((ts, c), F32),
                        pltpu.VMEM((len(ph_a), ph_rows, c), F32)],
        args=(u, _row(b_in), wa, _row(ba), _row(lg), _row(lb), wb), hook=hook)


def _ffn_pw_fwd(u2, wf, *, name, hook=None):
    s, f2 = u2.shape
    f = f2 // 2
    ts = _pick(s, (TS_FFN,))
    offs = _causal_offsets(HALO_S, K_F)

    def body(u_ref, wf_ref, z_ref, uwin, cbuf):
        i = pl.program_id(0)

        @pl.when(i == 0)
        def _():
            uwin[0:HALO_S, :] = jnp.zeros((HALO_S, f2), F32)

        @pl.when(i > 0)
        def _():
            uwin[0:HALO_S, :] = uwin[ts:ts + HALO_S, :]

        uwin[HALO_S:HALO_S + ts, :] = u_ref[...]
        _conv_rows(uwin, wf_ref, offs, ts, f2, _store_into(cbuf))
        gate = cbuf[:, 0:f]
        z_ref[...] = (gate * _sigmoid(gate) * cbuf[:, f:f2]).astype(BF16)

    return _call(
        body, name=name, grid=(s // ts,),
        in_specs=[pl.BlockSpec((ts, f2), lambda i: (i, 0)), _full((K_F, f2))],
        out_specs=[pl.BlockSpec((ts, f), lambda i: (i, 0))],
        out_shape=[jax.ShapeDtypeStruct((s, f), BF16)],
        scratch_shapes=[pltpu.VMEM((ts + HALO_S, f2), F32), pltpu.VMEM((ts, f2), F32)],
        args=(u2, wf), hook=hook)


def _loss_bwd(x, tgt, g, *, name):
    s, d = x.shape
    ts = _pick(s, (TS_NORM, 256, 128))

    def body(x_ref, t_ref, g_ref, loss_ref, dx_ref, dxb_ref, dg_ref):
        @pl.when(pl.program_id(0) == 0)
        def _():
            loss_ref[...] = jnp.zeros_like(loss_ref)
            dg_ref[...] = jnp.zeros_like(dg_ref)

        xv = x_ref[...]
        r = lax.rsqrt(jnp.mean(xv * xv, axis=-1, keepdims=True) + EPS)
        xh = xv * r
        err = xh * g_ref[...] - t_ref[...]
        loss_ref[...] += 0.5 * jnp.sum(jnp.mean(err * err, axis=-1, keepdims=True))
        dy = err * (1.0 / d)
        dg_ref[...] += jnp.sum(dy * xh, axis=0, keepdims=True)
        dxh = dy * g_ref[...]
        dx = r * (dxh - xh * jnp.mean(dxh * xh, axis=-1, keepdims=True))
        dx_ref[...] = dx
        dxb_ref[...] = dx.astype(BF16)

    tile = pl.BlockSpec((ts, d), lambda i: (i, 0))
    return pl.pallas_call(
        body, name=name, grid=(s // ts,), in_specs=[tile, tile, _full((1, d))],
        out_specs=[_full((8, LANES)), tile, tile, _full((1, d))],
        out_shape=[jax.ShapeDtypeStruct((8, LANES), F32), jax.ShapeDtypeStruct((s, d), F32),
                   jax.ShapeDtypeStruct((s, d), BF16), jax.ShapeDtypeStruct((1, d), F32)],
        compiler_params=_params())(x, tgt, _row(g))


def _ffn_pw_bwd(u2, dz, wf, *, name, hook=None):
    s, f2 = u2.shape
    f = f2 // 2
    ts = _pick(s, (TS_FFN,))
    nt = s // ts
    hb = ts // HALO_S
    offs_c, offs_t = _causal_offsets(HALO_S, K_F), _anticausal_offsets(K_F)
    ph_u, rows_u = _phases(offs_c), _phase_rows(offs_c, ts)

    def body(u_ref, uh_ref, dz_ref, wf_ref, du_ref, dwf_ref, uwin, cbuf, dcwin, ucop):
        i = pl.program_id(0)
        u_copies = {r: ucop.at[n] for n, r in enumerate(ph_u)}

        @pl.when(i == 0)
        def _():
            dwf_ref[...] = jnp.zeros_like(dwf_ref)
            dcwin[ts:ts + HALO_S, :] = jnp.zeros((HALO_S, f2), F32)

        @pl.when(i > 0)
        def _():
            dcwin[ts:ts + HALO_S, :] = dcwin[0:HALO_S, :]

        @pl.when(i == nt - 1)
        def _():
            uwin[0:HALO_S, :] = jnp.zeros((HALO_S, f2), F32)

        @pl.when(i < nt - 1)
        def _():
            uwin[0:HALO_S, :] = uh_ref[...]

        uwin[HALO_S:HALO_S + ts, :] = u_ref[...]
        _build_phase_copies(uwin, u_copies, rows_u, f2)
        _conv_rows(uwin, wf_ref, offs_c, ts, f2, _store_into(cbuf), copies=u_copies)
        gate = cbuf[:, 0:f]
        sg = _sigmoid(gate)
        dz_v = dz_ref[...]
        dcwin[0:ts, 0:f] = dz_v * cbuf[:, f:f2] * _dsilu(gate, sg)
        dcwin[0:ts, f:f2] = dz_v * gate * sg

        def emit(r0, c0, v):
            du_ref[r0:r0 + CONV_ROWS, c0:c0 + LANES] = v.astype(BF16)
        _conv_rows(dcwin, wf_ref, offs_t, ts, f2, emit)
        _conv_wgrad(dcwin, uwin, offs_c, ts, f2, dwf_ref, copies=u_copies)

    rev = lambda i: (nt - 1 - i, 0)
    return _call(
        body, name=name, grid=(nt,),
        in_specs=[pl.BlockSpec((ts, f2), rev),
                  pl.BlockSpec((HALO_S, f2), lambda i: (jnp.maximum((nt - 1 - i) * hb - 1, 0), 0)),
                  pl.BlockSpec((ts, f), rev), _full((K_F, f2))],
        out_specs=[pl.BlockSpec((ts, f2), rev), _full((K_F, f2))],
        out_shape=[jax.ShapeDtypeStruct((s, f2), BF16), jax.ShapeDtypeStruct((K_F, f2), F32)],
        scratch_shapes=[pltpu.VMEM((ts + HALO_S, f2), F32), pltpu.VMEM((ts, f2), F32),
                        pltpu.VMEM((ts + HALO_S, f2), F32), pltpu.VMEM((len(ph_u), rows_u, f2), F32)],
        args=(u2, u2, dz, wf), hook=hook)


def _mix_pw_bwd(dy, u, p, q, e, b_in, wa, lg, lb, wb, *, name, hook=None):
    s = u.shape[0]
    c = D_CONF
    ts = _pick(s, (TS_MIX, 128))
    nt = s // ts
    offs_ac, offs_at = _causal_offsets(HALO_A, K_A), _anticausal_offsets(K_A)
    offs_bc, offs_bt = _causal_offsets(HALO_S, K_B), _anticausal_offsets(K_B)
    ph_p, rows_p = _phases(offs_ac), _phase_rows(offs_ac, ts)
    ph_q, rows_q = _phases(offs_at), _phase_rows(offs_at, ts)

    def body(dy_ref, u_ref, p_ref, ph_ref, q_ref, e_ref, eh_ref, bin_ref, wa_ref, lg_ref, lb_ref, wb_ref,
             du_ref, dbin_ref, dwa_ref, dba_ref, dlg_ref, dlb_ref, dwb_ref, pwin, ewin, dqwin, dfwin, buf, pcop, qcop):
        i = pl.program_id(0)
        p_copies = {r: pcop.at[n] for n, r in enumerate(ph_p)}
        q_copies = {r: qcop.at[n] for n, r in enumerate(ph_q)}

        @pl.when(i == 0)
        def _():
            for r in (dbin_ref, dwa_ref, dba_ref, dlg_ref, dlb_ref, dwb_ref):
                r[...] = jnp.zeros_like(r)
            dqwin[ts:ts + HALO_A, :] = jnp.zeros((HALO_A, c), F32)
            dfwin[ts:ts + HALO_S, :] = jnp.zeros((HALO_S, c), F32)

        @pl.when(i > 0)
        def _():
            dqwin[ts:ts + HALO_A, :] = dqwin[0:HALO_A, :]
            dfwin[ts:ts + HALO_S, :] = dfwin[0:HALO_S, :]

        @pl.when(i == nt - 1)
        def _():
            pwin[0:HALO_A, :] = jnp.zeros((HALO_A, c), F32)
            ewin[0:HALO_S, :] = jnp.zeros((HALO_S, c), F32)

        @pl.when(i < nt - 1)
        def _():
            pwin[0:HALO_A, :] = ph_ref[...]
            ewin[0:HALO_S, :] = eh_ref[...]

        pwin[HALO_A:HALO_A + ts, :] = p_ref[...]
        ewin[HALO_S:HALO_S + ts, :] = e_ref[...]

        def colsum(v):
            return jnp.sum(v, axis=0, keepdims=True)

        q_v = q_ref[...]
        mu = jnp.mean(q_v, axis=-1, keepdims=True)
        xc = q_v - mu
        rstd = lax.rsqrt(jnp.mean(xc * xc, axis=-1, keepdims=True) + EPS)
        nrm = xc * rstd
        ln = nrm * lg_ref[...] + lb_ref[...]
        dln = dy_ref[:, 0:c] * _dsilu(ln, _sigmoid(ln))
        dlg_ref[...] += colsum(dln * nrm)
        dlb_ref[...] += colsum(dln)
        dn = dln * lg_ref[...]
        dq = rstd * (dn - jnp.mean(dn, axis=-1, keepdims=True) - nrm * jnp.mean(dn * nrm, axis=-1, keepdims=True))
        dba_ref[...] += colsum(dq)
        dqwin[0:ts, :] = dq
        _build_phase_copies(dqwin, q_copies, rows_q, c)
        _build_phase_copies(pwin, p_copies, rows_p, c)
        _conv_rows(dqwin, wa_ref, offs_at, ts, c, _store_into(buf), copies=q_copies)
        _conv_wgrad(dqwin, pwin, offs_ac, ts, c, dwa_ref, copies=p_copies)
        dp = buf[...]
        a_val = u_ref[:, 0:c] + bin_ref[:, 0:c]
        sg = _sigmoid(u_ref[:, c:2 * c] + bin_ref[:, c:2 * c])
        d_aval = dp * sg
        d_agate = dp * a_val * sg * (1.0 - sg)
        du_ref[:, 0:c] = d_aval.astype(BF16)
        du_ref[:, c:2 * c] = d_agate.astype(BF16)
        dbin_ref[:, 0:c] += colsum(d_aval)
        dbin_ref[:, c:2 * c] += colsum(d_agate)

        ds = dy_ref[:, c:2 * c]
        _conv_rows(ewin, wb_ref, offs_bc, ts, c, _store_into(buf))
        d_gb = ds * buf[...]
        dfwin[0:ts, :] = ds * (u_ref[:, 2 * c:3 * c] + bin_ref[:, 2 * c:3 * c])
        _conv_rows(dfwin, wb_ref, offs_bt, ts, c, _store_into(buf))
        _conv_wgrad(dfwin, ewin, offs_bc, ts, c, dwb_ref)
        de = buf[...]
        d_gc = de * (u_ref[:, 4 * c:5 * c] + bin_ref[:, 4 * c:5 * c])
        d_v = de * (u_ref[:, 3 * c:4 * c] + bin_ref[:, 3 * c:4 * c])
        du_ref[:, 2 * c:3 * c] = d_gb.astype(BF16)
        du_ref[:, 3 * c:4 * c] = d_gc.astype(BF16)
        du_ref[:, 4 * c:5 * c] = d_v.astype(BF16)
        dbin_ref[:, 2 * c:3 * c] += colsum(d_gb)
        dbin_ref[:, 3 * c:4 * c] += colsum(d_gc)
        dbin_ref[:, 4 * c:5 * c] += colsum(d_v)

    rev = lambda i: (nt - 1 - i, 0)
    tile = lambda w: pl.BlockSpec((ts, w), rev)
    halo = lambda h: pl.BlockSpec((h, c), lambda i: (jnp.maximum((nt - 1 - i) * (ts // h) - 1, 0), 0))
    small = [(1, 5 * c), (K_A, c), (1, c), (1, c), (1, c), (K_B, c)]
    return _call(
        body, name=name, grid=(nt,),
        in_specs=[tile(2 * c), tile(5 * c), tile(c), halo(HALO_A), tile(c), tile(c), halo(HALO_S),
                  _full((1, 5 * c)), _full((K_A, c)), _full((1, c)), _full((1, c)), _full((K_B, c))],
        out_specs=[tile(5 * c)] + [_full(sh) for sh in small],
        out_shape=[jax.ShapeDtypeStruct((s, 5 * c), BF16)] + [jax.ShapeDtypeStruct(sh, F32) for sh in small],
        scratch_shapes=[pltpu.VMEM((ts + HALO_A, c), F32), pltpu.VMEM((ts + HALO_S, c), F32),
                        pltpu.VMEM((ts + HALO_A, c), F32), pltpu.VMEM((ts + HALO_S, c), F32),
                        pltpu.VMEM((ts, c), F32), pltpu.VMEM((len(ph_p), rows_p, c), F32),
                        pltpu.VMEM((len(ph_q), rows_q, c), F32)],
        args=(dy, u, p, p, q, e, e, _row(b_in), wa, _row(lg), _row(lb), wb), hook=hook)


def _adamw(g, w, m, v, *, name):
    r, c = g.shape
    tr = _pick(r, (256, 128, 64, 8)) if r > 256 else r

    def body(g_ref, w_ref, m_ref, v_ref, d_ref, nm_ref, nv_ref):
        gv = g_ref[...]
        nm = ADAM_B1 * m_ref[...] + (1.0 - ADAM_B1) * gv
        nv = ADAM_B2 * v_ref[...] + (1.0 - ADAM_B2) * (gv * gv)
        m_hat = nm / (1.0 - ADAM_B1 ** ADAM_STEP)
        v_hat = nv / (1.0 - ADAM_B2 ** ADAM_STEP)
        d_ref[...] = -ADAM_LR * (m_hat / (jnp.sqrt(v_hat) + ADAM_EPS) + ADAM_WD * w_ref[...])
        nm_ref[...] = nm
        nv_ref[...] = nv

    tile = pl.BlockSpec((tr, c), lambda i: (i, 0))
    return pl.pallas_call(body, name=name, grid=(r // tr,), in_specs=[tile] * 4, out_specs=[tile] * 3,
                          out_shape=[jax.ShapeDtypeStruct((r, c), F32)] * 3, compiler_params=_params())(g, w, m, v)


def _cast_place(wshard, chip, *, name):
    l, r2, c = wshard.shape
    tr = r2 // 2

    def body(chip_ref, w_ref, *o_refs):
        for li, o_ref in enumerate(o_refs):
            o_ref[...] = w_ref[li].astype(BF16)

    grid_spec = pltpu.PrefetchScalarGridSpec(
        num_scalar_prefetch=1, grid=(2,),
        in_specs=[pl.BlockSpec((l, tr, c), lambda i, chip_ref: (0, i, 0))],
        out_specs=[pl.BlockSpec((None, tr, c), lambda i, chip_ref: (chip_ref[0], i, 0))] * l)
    return pl.pallas_call(body, name=name, grid_spec=grid_spec,
                          out_shape=[jax.ShapeDtypeStruct((N_CHIPS, r2, c), BF16)] * l, compiler_params=_params())(chip, wshard)


def _pair_sum(g, recv, core, *, name):
    _, _, r, c = g.shape

    def body(core_ref, g_ref, r_ref, o_ref):
        o_ref[...] = (g_ref[...] + r_ref[...].astype(F32)).astype(BF16)

    grid_spec = pltpu.PrefetchScalarGridSpec(
        num_scalar_prefetch=1, grid=(N_CHIPS,),
        in_specs=[pl.BlockSpec((None, None, r, c), lambda i, core_ref: (i, core_ref[0], 0, 0)),
                  pl.BlockSpec((None, r, c), lambda i, core_ref: (i, 0, 0))],
        out_specs=pl.BlockSpec((None, r, c), lambda i, core_ref: (i, 0, 0)))
    return pl.pallas_call(body, name=name, grid_spec=grid_spec,
                          out_shape=jax.ShapeDtypeStruct((N_CHIPS, r, c), BF16), compiler_params=_params())(core, g, recv)


def _chip_sum(own, recv, core_chip, stack, *, name):
    _, r, c = own.shape
    tr = r // 2
    l, layers, buf = stack

    def body(cc_ref, own_ref, r1_ref, r2_ref, r3_ref, *rest):
        o_ref = rest[-1]
        o_ref[...] = ((own_ref[...].astype(F32) + r1_ref[...].astype(F32)) + r2_ref[...].astype(F32)) + r3_ref[...].astype(F32)

    def slot(s):
        return pl.BlockSpec((None, tr, c), lambda i, cc_ref: ((cc_ref[1] + s) % N_CHIPS, i, 0))

    in_specs = [slot(0), slot(1), slot(2), slot(3)]
    args = [own, recv, recv, recv]
    aliases = {}
    if buf is not None:
        aliases = {1 + len(args): 0}
        in_specs.append(ANY)
        args.append(buf)
    grid_spec = pltpu.PrefetchScalarGridSpec(
        num_scalar_prefetch=1, grid=(2,), in_specs=in_specs,
        out_specs=pl.BlockSpec((None, None, tr, c), lambda i, cc_ref: (l, cc_ref[0], i, 0)))
    return pl.pallas_call(body, name=name, grid_spec=grid_spec, input_output_aliases=aliases,
                          out_shape=jax.ShapeDtypeStruct((layers, 2, r, c), F32), compiler_params=_params())(core_chip, *args)


def _place():
    x, y, c = lax.axis_index("x"), lax.axis_index("y"), lax.axis_index("c")
    chips = [(1 - x, y), (x, 1 - y), (1 - x, 1 - y)]
    return x, y, c, 2 * x + y, chips


def _remote(src, dst, send_sems, recv_sems, k, dev):
    return pltpu.make_async_remote_copy(src_ref=src, dst_ref=dst, send_sem=send_sems.at[k], recv_sem=recv_sems.at[k],
                                        device_id=dev, device_id_type=MESH)


def _sibling_halves(grads, *, name):
    n = len(grads)
    out_shape = [jax.ShapeDtypeStruct(g.shape[:1] + g.shape[2:], g.dtype) for g in grads]

    def body(*refs):
        ins, outs = refs[:n], refs[n:2 * n]
        send_sems, recv_sems = refs[2 * n:]
        x, y, c, _, _ = _place()
        cps = [_remote(ins[a].at[:, 1 - c], outs[a], send_sems, recv_sems, a, (x, y, 1 - c)) for a in range(n)]
        for cp in cps:
            cp.start()
        for cp in cps:
            cp.wait()

    return pl.pallas_call(body, name=name, in_specs=[ANY] * n, out_specs=[ANY] * n, out_shape=out_shape,
                          scratch_shapes=[pltpu.SemaphoreType.DMA((n,)), pltpu.SemaphoreType.DMA((n,))])(*grads)


class _Gather:
    def __init__(self, bufs):
        self.operands = list(bufs)
        self.out_shape = [jax.ShapeDtypeStruct(b.shape, b.dtype) for b in bufs]
        self.aliases = {a: a for a in range(len(bufs))}

    def sems(self):
        n = 6 * len(self.operands)
        return [pltpu.SemaphoreType.DMA((n,)), pltpu.SemaphoreType.DMA((n,))]

    def _sends(self, ins, outs, sems):
        x, y, c, j, chips = _place()
        return [_remote(ins[a].at[j, c], outs[a].at[j, c], sems[0], sems[1], 6 * a + k, (cx, cy, c))
                for a in range(len(ins)) for k, (cx, cy) in enumerate(chips)]

    def start(self, ins, outs, sems):
        for cp in self._sends(ins, outs, sems):
            cp.start()

    def finish(self, ins, outs, sems):
        x, y, c, j, chips = _place()
        sib = (x, y, 1 - c)
        passed = []
        for a in range(len(ins)):
            for k, (cx, cy) in enumerate(chips):
                blk = outs[a].at[2 * cx + cy, c]
                _remote(blk, blk, sems[0], sems[1], 6 * a + k, (cx, cy, c)).wait_recv()
                cp = _remote(blk, blk, sems[0], sems[1], 6 * a + 3 + k, sib)
                cp.start()
                passed.append(cp)
        for a in range(len(ins)):
            for k, (cx, cy) in enumerate(chips):
                blk = outs[a].at[2 * cx + cy, 1 - c]
                _remote(blk, blk, sems[0], sems[1], 6 * a + 3 + k, sib).wait_recv()
        for cp in self._sends(ins, outs, sems) + passed:
            cp.wait_send()


class _Scatter:
    def __init__(self, srcs):
        self.operands = list(srcs)
        self.out_shape = [jax.ShapeDtypeStruct(s.shape, s.dtype) for s in srcs]
        self.aliases = {}

    def sems(self):
        n = 3 * len(self.operands)
        return [pltpu.SemaphoreType.DMA((n,)), pltpu.SemaphoreType.DMA((n,))]

    def _sends(self, ins, outs, sems):
        x, y, c, j, chips = _place()
        return [_remote(ins[a].at[2 * cx + cy], outs[a].at[j], sems[0], sems[1], 3 * a + k, (cx, cy, c))
                for a in range(len(ins)) for k, (cx, cy) in enumerate(chips)]

    def start(self, ins, outs, sems):
        for cp in self._sends(ins, outs, sems):
            cp.start()

    def finish(self, ins, outs, sems):
        x, y, c, j, chips = _place()
        for a in range(len(ins)):
            for k, (cx, cy) in enumerate(chips):
                blk = outs[a].at[2 * cx + cy]
                _remote(blk, blk, sems[0], sems[1], 3 * a + k, (cx, cy, c)).wait_recv()
        for cp in self._sends(ins, outs, sems):
            cp.wait_send()


def _standalone(hook, *, name):
    n_in, n_out = len(hook.operands), len(hook.out_shape)

    def body(*refs):
        ins, outs, sems = refs[:n_in], refs[n_in:n_in + n_out], refs[n_in + n_out:]
        hook.start(ins, outs, sems)
        hook.finish(ins, outs, sems)

    return list(pl.pallas_call(body, name=name, in_specs=[ANY] * n_in, out_specs=[ANY] * n_out,
                               out_shape=hook.out_shape, input_output_aliases=hook.aliases,
                               scratch_shapes=hook.sems())(*hook.operands))


def _share_halves(bufs, *, name):
    n = len(bufs)

    def body(*refs):
        ins, outs = refs[:n], refs[n:2 * n]
        send_sems, recv_sems = refs[2 * n:]
        x, y, c, _, _ = _place()
        sends = [_remote(ins[a].at[:, c], outs[a].at[:, c], send_sems, recv_sems, a, (x, y, 1 - c)) for a in range(n)]
        for cp in sends:
            cp.start()
        for a in range(n):
            blk = outs[a].at[:, 1 - c]
            _remote(blk, blk, send_sems, recv_sems, a, (x, y, 1 - c)).wait_recv()
        for cp in sends:
            cp.wait_send()

    return pl.pallas_call(
        body, name=name, in_specs=[ANY] * n, out_specs=[ANY] * n,
        out_shape=[jax.ShapeDtypeStruct(b.shape, b.dtype) for b in bufs],
        input_output_aliases={a: a for a in range(n)},
        scratch_shapes=[pltpu.SemaphoreType.DMA((n,)), pltpu.SemaphoreType.DMA((n,))])(*bufs)


def _allreduce_small(v, *, name):
    r, c = v.shape
    n_dev = 8

    def body(v_ref, o_ref, buf, send_sems, recv_sems):
        x, y, cc = lax.axis_index("x"), lax.axis_index("y"), lax.axis_index("c")
        me = 4 * x + 2 * y + cc
        cps = []
        for d in range(1, n_dev):
            peer = (1 - x if d & 4 else x, 1 - y if d & 2 else y, 1 - cc if d & 1 else cc)
            cps.append(_remote(v_ref, buf.at[me], send_sems, recv_sems, d - 1, peer))
        for cp in cps:
            cp.start()
        buf[me] = v_ref[...]
        for cp in cps:
            cp.wait()
        acc = buf[0]
        for d in range(1, n_dev):
            acc = acc + buf[d]
        o_ref[...] = acc

    return pl.pallas_call(
        body, name=name, in_specs=[VMEM], out_specs=VMEM, out_shape=jax.ShapeDtypeStruct((r, c), F32),
        scratch_shapes=[pltpu.VMEM((n_dev, r, c), F32), pltpu.SemaphoreType.DMA((n_dev - 1,)),
                        pltpu.SemaphoreType.DMA((n_dev - 1,))])(v)


def _pack(arrays, row_multiple=8):
    flat = jnp.concatenate([a.reshape(-1) for a in arrays])
    rows = -(-flat.shape[0] // LANES)
    rows = -(-rows // row_multiple) * row_multiple
    return jnp.pad(flat, (0, rows * LANES - flat.shape[0])).reshape(rows, LANES)


def _unpack(slab, shapes):
    flat = slab.reshape(-1)
    out, pos = [], 0
    for sh in shapes:
        size = 1
        for dim in sh:
            size *= dim
        out.append(flat[pos:pos + size].reshape(sh))
        pos += size
    return out


def kernel(x, mix_norm_g, w_in, b_in, conv_a_w, conv_a_b, ln_a_g, ln_a_b, conv_b_w, w_out, ffn_norm_g, w_up, conv_f_w, w_down, final_norm_g, loss_target, m_mix_norm_g, m_w_in, m_b_in, m_conv_a_w, m_conv_a_b, m_ln_a_g, m_ln_a_b, m_conv_b_w, m_w_out, m_ffn_norm_g, m_w_up, m_conv_f_w, m_w_down, m_final_norm_g, v_mix_norm_g, v_w_in, v_b_in, v_conv_a_w, v_conv_a_b, v_ln_a_g, v_ln_a_b, v_conv_b_w, v_w_out, v_ffn_norm_g, v_w_up, v_conv_f_w, v_w_down, v_final_norm_g):
    w = dict(mix_norm_g=mix_norm_g, w_in=w_in, b_in=b_in, conv_a_w=conv_a_w, conv_a_b=conv_a_b, ln_a_g=ln_a_g,
             ln_a_b=ln_a_b, conv_b_w=conv_b_w, w_out=w_out, ffn_norm_g=ffn_norm_g, w_up=w_up, conv_f_w=conv_f_w,
             w_down=w_down, final_norm_g=final_norm_g)
    m = dict(mix_norm_g=m_mix_norm_g, w_in=m_w_in, b_in=m_b_in, conv_a_w=m_conv_a_w, conv_a_b=m_conv_a_b,
             ln_a_g=m_ln_a_g, ln_a_b=m_ln_a_b, conv_b_w=m_conv_b_w, w_out=m_w_out, ffn_norm_g=m_ffn_norm_g,
             w_up=m_w_up, conv_f_w=m_conv_f_w, w_down=m_w_down, final_norm_g=m_final_norm_g)
    v = dict(mix_norm_g=v_mix_norm_g, w_in=v_w_in, b_in=v_b_in, conv_a_w=v_conv_a_w, conv_a_b=v_conv_a_b,
             ln_a_g=v_ln_a_g, ln_a_b=v_ln_a_b, conv_b_w=v_conv_b_w, w_out=v_w_out, ffn_norm_g=v_ffn_norm_g,
             w_up=v_w_up, conv_f_w=v_conv_f_w, w_down=v_w_down, final_norm_g=v_final_norm_g)
    depth = w_in.shape[0]
    xs = x[0]
    tgt = loss_target[0]
    chip = (2 * lax.axis_index("x") + lax.axis_index("y")).astype(jnp.int32)
    chip_arr = chip.reshape(1)
    core = lax.axis_index("c").astype(jnp.int32).reshape(1)
    core_chip = jnp.concatenate([core, chip_arr])

    def halves(b):
        return b.reshape(N_CHIPS, 2, b.shape[1] // 2, b.shape[2])

    conv_shapes = [w[nm].shape for nm in CHANNEL_SHARDED]
    conv_slab = _pack([w[nm] for nm in CHANNEL_SHARDED], row_multiple=16)
    zero = jnp.int32(0)
    slab_buf = lax.dynamic_update_slice(jnp.zeros((N_CHIPS,) + conv_slab.shape, F32), conv_slab[None], (chip, zero, zero))
    wbuf = {}
    for nm in BIG:
        for l, b in enumerate(_cast_place(w[nm], chip_arr, name=f"cast_{nm}")):
            wbuf[nm, l] = halves(b)

    def gather_into(keys, done):
        for key, g in zip(keys, done):
            wbuf[key] = g

    def wmat(nm, l):
        g = wbuf[nm, l]
        if nm in ("w_in", "w_up"):
            return g.reshape(N_CHIPS, 2 * g.shape[2], g.shape[3])
        return g.reshape(N_CHIPS * 2 * g.shape[2], g.shape[3])

    first = [("w_in", 0)]
    done = _standalone(_Gather([wbuf[k] for k in first] + [halves(slab_buf)]), name="gather_first")
    gather_into(first, done)
    conv_full = done[-1].reshape(N_CHIPS, -1, LANES)
    taps = {}
    for nm, parts in zip(CHANNEL_SHARDED, zip(*[_unpack(conv_full[jj], conv_shapes) for jj in range(N_CHIPS)])):
        taps[nm] = jnp.concatenate(parts, axis=-1)

    assert depth == 2
    plan = {("mm_in", 0): [("w_out", 0)], ("mix_fwd", 0): [("w_up", 0)], ("mm_up", 0): [("w_down", 0), ("w_in", 1)],
            ("ffn_fwd", 0): [("w_up", 1)], ("mix_fwd", 1): [("w_out", 1), ("w_down", 1)]}

    def host(kind, l, run):
        keys = plan.get((kind, l), [])
        results, done = run(_Gather([wbuf[k] for k in keys]) if keys else None)
        gather_into(keys, done)
        return results

    saved = []
    cur = xs
    h1 = _norm_fwd(cur, w["mix_norm_g"][0], name="norm_mix_0")
    for l in range(depth):
        if ("mm_in", l) in plan:
            u = host("mm_in", l, lambda hook: _mm(h1, wmat("w_in", l), mode="nn", b_sharded=True, name=f"mm_in_{l}",
                                                  hook=hook))
        else:
            u = _mm(h1, wmat("w_in", l), mode="nn", b_sharded=True, name=f"mm_in_{l}")
        y, p, q, e = host("mix_fwd", l, lambda hook: _mix_pw_fwd(
            u, w["b_in"][l], taps["conv_a_w"][l], w["conv_a_b"][l], w["ln_a_g"][l], w["ln_a_b"][l],
            taps["conv_b_w"][l], name=f"mix_fwd_{l}", hook=hook))
        x1, h2 = _mm_res_norm(y, wmat("w_out", l), cur, w["ffn_norm_g"][l], name=f"mm_out_{l}")
        if ("mm_up", l) in plan:
            u2 = host("mm_up", l, lambda hook: _mm(h2, wmat("w_up", l), mode="nn", b_sharded=True, name=f"mm_up_{l}",
                                                   hook=hook))
        else:
            u2 = _mm(h2, wmat("w_up", l), mode="nn", b_sharded=True, name=f"mm_up_{l}")
        (z,) = host("ffn_fwd", l, lambda hook: _ffn_pw_fwd(u2, taps["conv_f_w"][l], name=f"ffn_fwd_{l}", hook=hook))
        saved.append((cur, h1, u, y, p, q, e, x1, h2, u2, z))
        if l + 1 < depth:
            cur, h1 = _mm_res_norm(z, wmat("w_down", l), x1, w["mix_norm_g"][l + 1], name=f"mm_down_{l}")
        else:
            cur = _mm(z, wmat("w_down", l), mode="nn", res=x1, name=f"mm_down_{l}")

    loss_blk, dx, dxb, dgf = _loss_bwd(cur, tgt, w["final_norm_g"], name="loss_bwd")
    loss = lax.psum(loss_blk[0, 0], ("x", "y", "c"))

    gfull, chip_sums, from_chips = {}, {}, {}

    def four_d(g):
        if g.ndim == 2:
            g = g.reshape(N_CHIPS, g.shape[0] // N_CHIPS, g.shape[1])
        return halves(g)

    def scatter_hook(keys, tag):
        from_sibling = _sibling_halves([four_d(gfull[k][1]) for k in keys], name=f"reduce_sibling_{tag}")
        for k, r in zip(keys, from_sibling):
            chip_sums[k] = _pair_sum(four_d(gfull[k][0]), r, core, name=f"pair_sum_{k[0]}_{k[1]}")
        return _Scatter([chip_sums[k] for k in keys])

    sg = {nm: [None] * depth for nm in SMALL if nm != "final_norm_g"}
    for l in reversed(range(depth)):
        x0, h1, u, y, p, q, e, x1, h2, u2, z = saved[l]
        dz = _mm(dxb, wmat("w_down", l), mode="nt", name=f"mm_ddown_{l}")
        gfull["w_down", l] = _mm(z, dxb, mode="tn", also_bf16=True, name=f"mm_gdown_{l}")
        keys = [("w_down", l)]
        (du2, dwf), done = _ffn_pw_bwd(u2, dz, taps["conv_f_w"][l], name=f"ffn_bwd_{l}", hook=scatter_hook(keys, f"a{l}"))
        from_chips.update(zip(keys, done))
        (dx1, dx1b, dg2), _ = _mm_norm_bwd(du2, wmat("w_up", l), x1, dx, w["ffn_norm_g"][l], name=f"mm_dup_{l}")
        gfull["w_up", l] = _mm(h2, du2, mode="tn", out_sharded=True, also_bf16=True, name=f"mm_gup_{l}")
        dy = _mm(dx1b, wmat("w_out", l), mode="nt", name=f"mm_dout_{l}")
        gfull["w_out", l] = _mm(y, dx1b, mode="tn", also_bf16=True, name=f"mm_gout_{l}")
        keys = [("w_up", l), ("w_out", l)]
        (du, dbin, dwa, dba, dlg, dlb, dwb), done = _mix_pw_bwd(
            dy, u, p, q, e, w["b_in"][l], taps["conv_a_w"][l], w["ln_a_g"][l], w["ln_a_b"][l],
            taps["conv_b_w"][l], name=f"mix_bwd_{l}", hook=scatter_hook(keys, f"b{l}"))
        from_chips.update(zip(keys, done))
        gfull["w_in", l] = _mm(h1, du, mode="tn", out_sharded=True, also_bf16=True, name=f"mm_gin_{l}")
        keys = [("w_in", l)]
        (dx, dxb, dg1), done = _mm_norm_bwd(du, wmat("w_in", l), x0, dx1, w["mix_norm_g"][l], name=f"mm_din_{l}",
                                            hook=scatter_hook(keys, f"c{l}"))
        from_chips.update(zip(keys, done))
        for nm, g in (("mix_norm_g", dg1), ("b_in", dbin), ("conv_a_w", dwa), ("conv_a_b", dba), ("ln_a_g", dlg),
                      ("ln_a_b", dlb), ("conv_b_w", dwb), ("ffn_norm_g", dg2), ("conv_f_w", dwf)):
            sg[nm][l] = g.reshape(g.shape[-1]) if g.shape[0] == 1 else g
    grad_x = dx[None]

    mine = []
    for nm in BIG:
        buf = None
        for l in range(depth):
            buf = _chip_sum(chip_sums[nm, l], from_chips[nm, l], core_chip, (l, depth, buf), name=f"chip_sum_{nm}_{l}")
        mine.append(buf)
    shared = _share_halves(mine, name="share_halves")
    grads = {}
    for nm, g in zip(BIG, shared):
        grads[nm] = g.reshape(w[nm].shape)

    small_full_shapes = []
    small_parts = []
    for nm in SMALL:
        if nm == "final_norm_g":
            g = dgf.reshape(-1)
        else:
            g = jnp.stack(sg[nm])
        small_parts.append(g)
        small_full_shapes.append(g.shape)
    summed = _unpack(_allreduce_small(_pack(small_parts), name="reduce_small"), small_full_shapes)
    for nm, g in zip(SMALL, summed):
        if nm in CHANNEL_SHARDED:
            width = w[nm].shape[-1]
            g = lax.dynamic_slice_in_dim(g, chip * width, width, axis=2)
        grads[nm] = g

    delta, new_m, new_v = {}, {}, {}
    for nm in BIG:
        sh = w[nm].shape
        two_d = lambda a: a.reshape(sh[0] * sh[1], sh[2])
        d_, m_, v_ = _adamw(two_d(grads[nm]), two_d(w[nm]), two_d(m[nm]), two_d(v[nm]), name=f"adamw_{nm}")
        delta[nm], new_m[nm], new_v[nm] = d_.reshape(sh), m_.reshape(sh), v_.reshape(sh)
    small_shapes = [w[nm].shape for nm in SMALL]
    outs = _adamw(_pack([grads[nm] for nm in SMALL]), _pack([w[nm] for nm in SMALL]), _pack([m[nm] for nm in SMALL]),
                  _pack([v[nm] for nm in SMALL]), name="adamw_small")
    for store, slab in zip((delta, new_m, new_v), outs):
        for nm, a in zip(SMALL, _unpack(slab, small_shapes)):
            store[nm] = a

    return (loss, grad_x, *[grads[nm] for nm in TWIN_WEIGHTS], *[delta[nm] for nm in TWIN_WEIGHTS],
            *[new_m[nm] for nm in TWIN_WEIGHTS], *[new_v[nm] for nm in TWIN_WEIGHTS])
```

```python
import functools

import jax
import jax.numpy as jnp
from jax import lax
from jax.experimental import pallas as pl
from jax.experimental.pallas import tpu as pltpu

F32 = jnp.float32
BF16 = jnp.bfloat16
MESH = pl.DeviceIdType.MESH
ANY = pl.BlockSpec(memory_space=pl.ANY)
VMEM = pl.BlockSpec(memory_space=pltpu.VMEM)

EPS = 1e-6
ADAM_LR, ADAM_B1, ADAM_B2, ADAM_EPS, ADAM_WD, ADAM_STEP = 0.001, 0.9, 0.999, 1e-08, 0.01, 10

D_CONF = 512
K_A, K_B, K_F = 31, 3, 3
HALO_A, HALO_S = 32, 8
N_CHIPS = 4
LANES = 128
SUBLANES = 8
CONV_ROWS = 128
TS_MIX, TS_FFN, TS_NORM = 256, 128, 512
VMEM_LIMIT = 48 * 1024 * 1024
VMEM_LIMIT_WIDE = 56 * 1024 * 1024

TWIN_WEIGHTS = ['mix_norm_g', 'w_in', 'b_in', 'conv_a_w', 'conv_a_b', 'ln_a_g', 'ln_a_b', 'conv_b_w', 'w_out',
                'ffn_norm_g', 'w_up', 'conv_f_w', 'w_down', 'final_norm_g']
BIG = ['w_in', 'w_out', 'w_up', 'w_down']
SMALL = [n for n in TWIN_WEIGHTS if n not in BIG]
CHANNEL_SHARDED = ['conv_a_w', 'conv_b_w', 'conv_f_w']


def _params(n_grid=1, vmem_limit=VMEM_LIMIT):
    return pltpu.CompilerParams(dimension_semantics=("arbitrary",) * n_grid, vmem_limit_bytes=vmem_limit)


def _pick(n, cands):
    for c in cands:
        if n % c == 0:
            return c
    raise ValueError(f"no tile for {n}")


_DN = {"nn": (((1,), (0,)), ((), ())), "nt": (((1,), (1,)), ((), ())), "tn": (((0,), (0,)), ((), ()))}


def _mm(a, b, *, mode, name, layer=None, b_sharded=False, out_sharded=False, res=None, stack=None, out_dtype=F32,
        also_bf16=False, hook=None):
    dn = _DN[mode]
    lead = () if layer is None else (None,)
    pre = () if layer is None else (layer,)
    wide = (1408, 1280, 1024, 512)
    if mode == "tn":
        k, m = a.shape
        n = b.shape[1]
        tm = _pick(m, (512, 256))
        tn = n // N_CHIPS if out_sharded else _pick(n, wide)
    else:
        m, k = a.shape
        tm = _pick(m, (1024, 512, 256, 128)) if k <= 1024 else _pick(m, (512, 256, 128))
        if mode == "nn":
            n = b.shape[-1] * N_CHIPS if b_sharded else b.shape[-1]
            tn = b.shape[-1] if b_sharded else _pick(n, wide)
        else:
            n = b.shape[-2]
            tn = _pick(n, (1024, 512, 256)) if b_sharded else _pick(n, wide)
    n_outer = (n // tn - 1) * a.size < (m // tm - 1) * (b.size if layer is None else b.size // b.shape[0])

    def im(f):
        return (lambda g0, g1: f(g1, g0)) if n_outer else f

    if mode == "tn":
        a_spec = pl.BlockSpec((k, tm), im(lambda i, j: (0, i)))
        b_spec = pl.BlockSpec((k, tn), im(lambda i, j: (0, j)))
    else:
        a_spec = pl.BlockSpec((tm, k), im(lambda i, j: (i, 0)))
        if mode == "nn" and b_sharded:
            b_spec = pl.BlockSpec(lead + (None, k, tn), im(lambda i, j: pre + (j, 0, 0)))
        elif mode == "nn":
            b_spec = pl.BlockSpec(lead + (k, tn), im(lambda i, j: pre + (0, j)))
        elif b_sharded:
            b_spec = pl.BlockSpec(lead + (N_CHIPS, tn, b.shape[-1]), im(lambda i, j: pre + (0, j, 0)))
        else:
            b_spec = pl.BlockSpec(lead + (tn, k), im(lambda i, j: pre + (j, 0)))
    slot_lead, slot_pre, slot_shape = (), (), ()
    if stack is not None:
        slot_lead, slot_pre, slot_shape = (None,), (stack[0],), (stack[1],)
    if out_sharded:
        out_shape = jax.ShapeDtypeStruct(slot_shape + (N_CHIPS, m, tn), out_dtype)
        o_spec = pl.BlockSpec(slot_lead + (None, tm, tn), im(lambda i, j: slot_pre + (j, i, 0)))
    else:
        out_shape = jax.ShapeDtypeStruct(slot_shape + (m, n), out_dtype)
        o_spec = pl.BlockSpec(slot_lead + (tm, tn), im(lambda i, j: slot_pre + (i, j)))
    in_specs = [a_spec, b_spec]
    args = [a, b]
    aliases = {}
    if res is not None:
        in_specs.append(pl.BlockSpec((tm, tn), im(lambda i, j: (i, j))))
        args.append(res)
    if stack is not None and stack[2] is not None:
        aliases = {len(args): 0}
        in_specs.append(ANY)
        args.append(stack[2])
    n_in = len(args)

    def body(*refs):
        a_ref, b_ref, o_ref = refs[0], refs[1], refs[n_in]
        if mode == "nt" and b_sharded:
            ks_ = b_ref.shape[2]
            acc = None
            for s in range(N_CHIPS):
                part = lax.dot_general(a_ref[:, s * ks_:(s + 1) * ks_], b_ref[s], dn, preferred_element_type=F32)
                acc = part if acc is None else acc + part
        else:
            acc = lax.dot_general(a_ref[...], b_ref[...], dn, preferred_element_type=F32)
        if res is not None:
            acc = acc + refs[2][...]
        o_ref[...] = acc.astype(o_ref.dtype)
        if also_bf16:
            refs[n_in + 1][...] = acc.astype(BF16)

    grid = (n // tn, m // tm) if n_outer else (m // tm, n // tn)
    if hook is not None:
        assert not aliases and not also_bf16
        outs, hook_outs = _call(body, name=name, grid=grid, in_specs=in_specs, out_specs=[o_spec], out_shape=[out_shape],
                                args=args, hook=hook)
        return outs[0], hook_outs
    if also_bf16:
        o_spec = [o_spec, o_spec]
        out_shape = [out_shape, jax.ShapeDtypeStruct(out_shape.shape, BF16)]
    return pl.pallas_call(body, name=name, grid=grid, in_specs=in_specs, out_specs=o_spec,
                          out_shape=out_shape, input_output_aliases=aliases, compiler_params=_params(2))(*args)


def _mm_res_norm(a, b, res, g, *, name):
    m, k = a.shape
    n = b.shape[1]
    tm = _pick(m, (1024, 512)) if k <= 1024 else _pick(m, (512, 256))

    def body(a_ref, b_ref, r_ref, g_ref, x_ref, h_ref):
        xv = r_ref[...] + lax.dot_general(a_ref[...], b_ref[...], _DN["nn"], preferred_element_type=F32)
        x_ref[...] = xv
        r = lax.rsqrt(jnp.mean(xv * xv, axis=-1, keepdims=True) + EPS)
        h_ref[...] = (xv * r * g_ref[...]).astype(BF16)

    rows = lambda w: pl.BlockSpec((tm, w), lambda i: (i, 0))
    return pl.pallas_call(
        body, name=name, grid=(m // tm,), in_specs=[rows(k), _full((k, n)), rows(n), _full((1, n))],
        out_specs=[rows(n), rows(n)],
        out_shape=[jax.ShapeDtypeStruct((m, n), F32), jax.ShapeDtypeStruct((m, n), BF16)],
        compiler_params=_params())(a, b, res, _row(g))


def _mm_norm_bwd(a, b4, x, dx_in, g, *, name, hook=None):
    m, k = a.shape
    n, ks = b4.shape[1], b4.shape[2]
    tm = _pick(m, (512, 256))

    def body(a_ref, b_ref, x_ref, dxin_ref, g_ref, dx_ref, dxb_ref, dg_ref):
        @pl.when(pl.program_id(0) == 0)
        def _():
            dg_ref[...] = jnp.zeros_like(dg_ref)

        dh = None
        for s in range(N_CHIPS):
            part = lax.dot_general(a_ref[:, s * ks:(s + 1) * ks], b_ref[s], _DN["nt"], preferred_element_type=F32)
            dh = part if dh is None else dh + part
        xv = x_ref[...]
        r = lax.rsqrt(jnp.mean(xv * xv, axis=-1, keepdims=True) + EPS)
        xh = xv * r
        dg_ref[...] += jnp.sum(dh * xh, axis=0, keepdims=True)
        dxh = dh * g_ref[...]
        dx = dxin_ref[...] + r * (dxh - xh * jnp.mean(dxh * xh, axis=-1, keepdims=True))
        dx_ref[...] = dx
        dxb_ref[...] = dx.astype(BF16)

    rows = lambda w: pl.BlockSpec((tm, w), lambda i: (i, 0))
    weights = pl.BlockSpec((N_CHIPS, n, ks), lambda i: (0, 0, 0), pipeline_mode=pl.Buffered(1))
    return _call(
        body, name=name, grid=(m // tm,), in_specs=[rows(k), weights, rows(n), rows(n), _full((1, n))],
        out_specs=[rows(n), rows(n), _full((1, n))],
        out_shape=[jax.ShapeDtypeStruct((m, n), F32), jax.ShapeDtypeStruct((m, n), BF16),
                   jax.ShapeDtypeStruct((1, n), F32)],
        args=(a, b4, x, dx_in, _row(g)), hook=hook, vmem_limit=VMEM_LIMIT_WIDE)


def _phases(offsets):
    return sorted({off % SUBLANES for off in offsets} - {0})


def _phase_rows(offsets, rows):
    return rows + max(off - off % SUBLANES for off in offsets if off % SUBLANES)


def _build_phase_copies(win_ref, copies, n_rows, cols):
    for r, ref in copies.items():
        for c0 in range(0, cols, LANES):
            for r0 in range(0, n_rows, CONV_ROWS):
                n = min(CONV_ROWS, n_rows - r0)
                ref[r0:r0 + n, c0:c0 + LANES] = win_ref[r + r0:r + r0 + n, c0:c0 + LANES]


def _tap(win_ref, copies, off):
    r = off % SUBLANES
    return (win_ref, off) if r == 0 or copies is None else (copies[r], off - r)


def _conv_rows(win_ref, w_ref, offsets, rows, cols, emit, bias_ref=None, copies=None):
    taps = [_tap(win_ref, copies, off) for off in offsets]
    for c0 in range(0, cols, LANES):
        cs = slice(c0, c0 + LANES)
        wk = [w_ref[k:k + 1, cs] for k in range(len(offsets))]
        for r0 in range(0, rows, CONV_ROWS):
            acc = None
            for k, (src, off) in enumerate(taps):
                term = wk[k] * src[off + r0:off + r0 + CONV_ROWS, cs]
                acc = term if acc is None else acc + term
            if bias_ref is not None:
                acc = acc + bias_ref[:, cs]
            emit(r0, c0, acc)


def _conv_wgrad(g_ref, x_ref, offsets, rows, cols, dw_ref, copies=None):
    taps = [_tap(x_ref, copies, off) for off in offsets]
    for c0 in range(0, cols, LANES):
        cs = slice(c0, c0 + LANES)
        for k, (src, off) in enumerate(taps):
            acc = None
            for r0 in range(0, rows, CONV_ROWS):
                term = g_ref[r0:r0 + CONV_ROWS, cs] * src[off + r0:off + r0 + CONV_ROWS, cs]
                acc = term if acc is None else acc + term
            dw_ref[k:k + 1, cs] += jnp.sum(acc, axis=0, keepdims=True)


def _store_into(ref):
    def emit(r0, c0, v):
        ref[r0:r0 + CONV_ROWS, c0:c0 + LANES] = v
    return emit


def _causal_offsets(halo, k):
    return [halo - (k - 1) + j for j in range(k)]


def _anticausal_offsets(k):
    return [(k - 1) - j for j in range(k)]


def _row(v):
    return v.reshape(1, -1)


def _full(shape):
    return pl.BlockSpec(shape, lambda i: (0,) * len(shape))


def _sigmoid(v):
    return jax.nn.sigmoid(v)


def _dsilu(v, sg):
    return sg * (1.0 + v * (1.0 - sg))


def _norm_fwd(x, g, *, name):
    s, d = x.shape
    ts = _pick(s, (TS_NORM, 256, 128))

    def body(x_ref, g_ref, h_ref):
        xv = x_ref[...]
        r = lax.rsqrt(jnp.mean(xv * xv, axis=-1, keepdims=True) + EPS)
        h_ref[...] = (xv * r * g_ref[...]).astype(BF16)

    return pl.pallas_call(body, name=name, grid=(s // ts,),
                          in_specs=[pl.BlockSpec((ts, d), lambda i: (i, 0)), _full((1, d))],
                          out_specs=pl.BlockSpec((ts, d), lambda i: (i, 0)),
                          out_shape=jax.ShapeDtypeStruct((s, d), BF16), compiler_params=_params())(x, _row(g))


def _call(body, *, name, grid, in_specs, out_specs, out_shape, args, scratch_shapes=(), hook=None, vmem_limit=VMEM_LIMIT):
    n_in, n_out, n_scr = len(in_specs), len(out_specs), len(scratch_shapes)
    if hook is None:
        outs = pl.pallas_call(body, name=name, grid=grid, in_specs=in_specs, out_specs=out_specs, out_shape=out_shape,
                              scratch_shapes=list(scratch_shapes), compiler_params=_params(len(grid), vmem_limit))(*args)
        return list(outs), []
    h_in, h_out = len(hook.operands), len(hook.out_shape)

    def hosted(*refs):
        ins, refs = refs[:n_in], refs[n_in:]
        h_ins, refs = refs[:h_in], refs[h_in:]
        outs, refs = refs[:n_out], refs[n_out:]
        h_outs, refs = refs[:h_out], refs[h_out:]
        scr, sems = refs[:n_scr], refs[n_scr:]
        first = last = None
        for d, size in enumerate(grid):
            at_first, at_last = pl.program_id(d) == 0, pl.program_id(d) == size - 1
            first = at_first if first is None else jnp.logical_and(first, at_first)
            last = at_last if last is None else jnp.logical_and(last, at_last)

        @pl.when(first)
        def _():
            hook.start(h_ins, h_outs, sems)

        body(*ins, *outs, *scr)

        @pl.when(last)
        def _():
            hook.finish(h_ins, h_outs, sems)

    outs = pl.pallas_call(
        hosted, name=name, grid=grid, in_specs=list(in_specs) + [ANY] * h_in, out_specs=list(out_specs) + [ANY] * h_out,
        out_shape=list(out_shape) + list(hook.out_shape), scratch_shapes=list(scratch_shapes) + hook.sems(),
        input_output_aliases={n_in + k: n_out + v for k, v in hook.aliases.items()},
        compiler_params=_params(len(grid), vmem_limit))(*args, *hook.operands)
    return list(outs[:n_out]), list(outs[n_out:])


def _mix_pw_fwd(u, b_in, wa, ba, lg, lb, wb, *, name, hook=None):
    s = u.shape[0]
    c = D_CONF
    ts = _pick(s, (TS_MIX, 128))
    offs_a, offs_b = _causal_offsets(HALO_A, K_A), _causal_offsets(HALO_S, K_B)
    ph_a, ph_rows = _phases(offs_a), _phase_rows(offs_a, ts)

    def body(u_ref, bin_ref, wa_ref, ba_ref, lg_ref, lb_ref, wb_ref, y_ref, p_ref, q_ref, e_ref, pwin, ewin, fbuf, pcop):
        i = pl.program_id(0)
        p_copies = {r: pcop.at[n] for n, r in enumerate(ph_a)}

        @pl.when(i == 0)
        def _():
            pwin[0:HALO_A, :] = jnp.zeros((HALO_A, c), F32)
            ewin[0:HALO_S, :] = jnp.zeros((HALO_S, c), F32)

        @pl.when(i > 0)
        def _():
            pwin[0:HALO_A, :] = pwin[ts:ts + HALO_A, :]
            ewin[0:HALO_S, :] = ewin[ts:ts + HALO_S, :]

        a_val = u_ref[:, 0:c] + bin_ref[:, 0:c]
        a_gate = u_ref[:, c:2 * c] + bin_ref[:, c:2 * c]
        p = a_val * _sigmoid(a_gate)
        pwin[HALO_A:HALO_A + ts, :] = p
        p_ref[...] = p
        e = (u_ref[:, 3 * c:4 * c] + bin_ref[:, 3 * c:4 * c]) * (u_ref[:, 4 * c:5 * c] + bin_ref[:, 4 * c:5 * c])
        ewin[HALO_S:HALO_S + ts, :] = e
        e_ref[...] = e
        _build_phase_copies(pwin, p_copies, ph_rows, c)
        _conv_rows(pwin, wa_ref, offs_a, ts, c, _store_into(q_ref), bias_ref=ba_ref, copies=p_copies)
        q = q_ref[...]
        mu = jnp.mean(q, axis=-1, keepdims=True)
        xc = q - mu
        var = jnp.mean(xc * xc, axis=-1, keepdims=True)
        ln = xc * lax.rsqrt(var + EPS) * lg_ref[...] + lb_ref[...]
        y_ref[:, 0:c] = (ln * _sigmoid(ln)).astype(BF16)
        _conv_rows(ewin, wb_ref, offs_b, ts, c, _store_into(fbuf))
        g_b = u_ref[:, 2 * c:3 * c] + bin_ref[:, 2 * c:3 * c]
        y_ref[:, c:2 * c] = (g_b * fbuf[...]).astype(BF16)

    tile = lambda w: pl.BlockSpec((ts, w), lambda i: (i, 0))
    return _call(
        body, name=name, grid=(s // ts,),
        in_specs=[tile(5 * c), _full((1, 5 * c)), _full((K_A, c)), _full((1, c)), _full((1, c)), _full((1, c)),
                  _full((K_B, c))],
        out_specs=[tile(2 * c), tile(c), tile(c), tile(c)],
        out_shape=[jax.ShapeDtypeStruct((s, 2 * c), BF16)] + [jax.ShapeDtypeStruct((s, c), F32)] * 3,
        scratch_shapes=[pltpu.VMEM((ts + HALO_A, c), F32), pltpu.VMEM((ts + HALO_S, c), F32), pltpu.VMEM((ts, c), F32),
                        pltpu.VMEM((len(ph_a), ph_rows, c), F32)],
        args=(u, _row(b_in), wa, _row(ba), _row(lg), _row(lb), wb), hook=hook)


def _ffn_pw_fwd(u2, wf, *, name, hook=None):
    s, f2 = u2.shape
    f = f2 // 2
    ts = _pick(s, (TS_FFN,))
    offs = _causal_offsets(HALO_S, K_F)

    def body(u_ref, wf_ref, z_ref, uwin, cbuf):
        i = pl.program_id(0)

        @pl.when(i == 0)
        def _():
            uwin[0:HALO_S, :] = jnp.zeros((HALO_S, f2), F32)

        @pl.when(i > 0)
        def _():
            uwin[0:HALO_S, :] = uwin[ts:ts + HALO_S, :]

        uwin[HALO_S:HALO_S + ts, :] = u_ref[...]
        _conv_rows(uwin, wf_ref, offs, ts, f2, _store_into(cbuf))
        gate = cbuf[:, 0:f]
        z_ref[...] = (gate * _sigmoid(gate) * cbuf[:, f:f2]).astype(BF16)

    return _call(
        body, name=name, grid=(s // ts,),
        in_specs=[pl.BlockSpec((ts, f2), lambda i: (i, 0)), _full((K_F, f2))],
        out_specs=[pl.BlockSpec((ts, f), lambda i: (i, 0))],
        out_shape=[jax.ShapeDtypeStruct((s, f), BF16)],
        scratch_shapes=[pltpu.VMEM((ts + HALO_S, f2), F32), pltpu.VMEM((ts, f2), F32)],
        args=(u2, wf), hook=hook)


def _loss_bwd(x, tgt, g, *, name):
    s, d = x.shape
    ts = _pick(s, (TS_NORM, 256, 128))

    def body(x_ref, t_ref, g_ref, loss_ref, dx_ref, dxb_ref, dg_ref):
        @pl.when(pl.program_id(0) == 0)
        def _():
            loss_ref[...] = jnp.zeros_like(loss_ref)
            dg_ref[...] = jnp.zeros_like(dg_ref)

        xv = x_ref[...]
        r = lax.rsqrt(jnp.mean(xv * xv, axis=-1, keepdims=True) + EPS)
        xh = xv * r
        err = xh * g_ref[...] - t_ref[...]
        loss_ref[...] += 0.5 * jnp.sum(jnp.mean(err * err, axis=-1, keepdims=True))
        dy = err * (1.0 / d)
        dg_ref[...] += jnp.sum(dy * xh, axis=0, keepdims=True)
        dxh = dy * g_ref[...]
        dx = r * (dxh - xh * jnp.mean(dxh * xh, axis=-1, keepdims=True))
        dx_ref[...] = dx
        dxb_ref[...] = dx.astype(BF16)

    tile = pl.BlockSpec((ts, d), lambda i: (i, 0))
    return pl.pallas_call(
        body, name=name, grid=(s // ts,), in_specs=[tile, tile, _full((1, d))],
        out_specs=[_full((8, LANES)), tile, tile, _full((1, d))],
        out_shape=[jax.ShapeDtypeStruct((8, LANES), F32), jax.ShapeDtypeStruct((s, d), F32),
                   jax.ShapeDtypeStruct((s, d), BF16), jax.ShapeDtypeStruct((1, d), F32)],
        compiler_params=_params())(x, tgt, _row(g))


def _ffn_pw_bwd(u2, dz, wf, *, name, hook=None):
    s, f2 = u2.shape
    f = f2 // 2
    ts = _pick(s, (TS_FFN,))
    nt = s // ts
    hb = ts // HALO_S
    offs_c, offs_t = _causal_offsets(HALO_S, K_F), _anticausal_offsets(K_F)
    ph_u, rows_u = _phases(offs_c), _phase_rows(offs_c, ts)

    def body(u_ref, uh_ref, dz_ref, wf_ref, du_ref, dwf_ref, uwin, cbuf, dcwin, ucop):
        i = pl.program_id(0)
        u_copies = {r: ucop.at[n] for n, r in enumerate(ph_u)}

        @pl.when(i == 0)
        def _():
            dwf_ref[...] = jnp.zeros_like(dwf_ref)
            dcwin[ts:ts + HALO_S, :] = jnp.zeros((HALO_S, f2), F32)

        @pl.when(i > 0)
        def _():
            dcwin[ts:ts + HALO_S, :] = dcwin[0:HALO_S, :]

        @pl.when(i == nt - 1)
        def _():
            uwin[0:HALO_S, :] = jnp.zeros((HALO_S, f2), F32)

        @pl.when(i < nt - 1)
        def _():
            uwin[0:HALO_S, :] = uh_ref[...]

        uwin[HALO_S:HALO_S + ts, :] = u_ref[...]
        _build_phase_copies(uwin, u_copies, rows_u, f2)
        _conv_rows(uwin, wf_ref, offs_c, ts, f2, _store_into(cbuf), copies=u_copies)
        gate = cbuf[:, 0:f]
        sg = _sigmoid(gate)
        dz_v = dz_ref[...]
        dcwin[0:ts, 0:f] = dz_v * cbuf[:, f:f2] * _dsilu(gate, sg)
        dcwin[0:ts, f:f2] = dz_v * gate * sg

        def emit(r0, c0, v):
            du_ref[r0:r0 + CONV_ROWS, c0:c0 + LANES] = v.astype(BF16)
        _conv_rows(dcwin, wf_ref, offs_t, ts, f2, emit)
        _conv_wgrad(dcwin, uwin, offs_c, ts, f2, dwf_ref, copies=u_copies)

    rev = lambda i: (nt - 1 - i, 0)
    return _call(
        body, name=name, grid=(nt,),
        in_specs=[pl.BlockSpec((ts, f2), rev),
                  pl.BlockSpec((HALO_S, f2), lambda i: (jnp.maximum((nt - 1 - i) * hb - 1, 0), 0)),
                  pl.BlockSpec((ts, f), rev), _full((K_F, f2))],
        out_specs=[pl.BlockSpec((ts, f2), rev), _full((K_F, f2))],
        out_shape=[jax.ShapeDtypeStruct((s, f2), BF16), jax.ShapeDtypeStruct((K_F, f2), F32)],
        scratch_shapes=[pltpu.VMEM((ts + HALO_S, f2), F32), pltpu.VMEM((ts, f2), F32),
                        pltpu.VMEM((ts + HALO_S, f2), F32), pltpu.VMEM((len(ph_u), rows_u, f2), F32)],
        args=(u2, u2, dz, wf), hook=hook)


def _mix_pw_bwd(dy, u, p, q, e, b_in, wa, lg, lb, wb, *, name, hook=None):
    s = u.shape[0]
    c = D_CONF
    ts = _pick(s, (TS_MIX, 128))
    nt = s // ts
    offs_ac, offs_at = _causal_offsets(HALO_A, K_A), _anticausal_offsets(K_A)
    offs_bc, offs_bt = _causal_offsets(HALO_S, K_B), _anticausal_offsets(K_B)
    ph_p, rows_p = _phases(offs_ac), _phase_rows(offs_ac, ts)
    ph_q, rows_q = _phases(offs_at), _phase_rows(offs_at, ts)

    def body(dy_ref, u_ref, p_ref, ph_ref, q_ref, e_ref, eh_ref, bin_ref, wa_ref, lg_ref, lb_ref, wb_ref,
             du_ref, dbin_ref, dwa_ref, dba_ref, dlg_ref, dlb_ref, dwb_ref, pwin, ewin, dqwin, dfwin, buf, pcop, qcop):
        i = pl.program_id(0)
        p_copies = {r: pcop.at[n] for n, r in enumerate(ph_p)}
        q_copies = {r: qcop.at[n] for n, r in enumerate(ph_q)}

        @pl.when(i == 0)
        def _():
            for r in (dbin_ref, dwa_ref, dba_ref, dlg_ref, dlb_ref, dwb_ref):
                r[...] = jnp.zeros_like(r)
            dqwin[ts:ts + HALO_A, :] = jnp.zeros((HALO_A, c), F32)
            dfwin[ts:ts + HALO_S, :] = jnp.zeros((HALO_S, c), F32)

        @pl.when(i > 0)
        def _():
            dqwin[ts:ts + HALO_A, :] = dqwin[0:HALO_A, :]
            dfwin[ts:ts + HALO_S, :] = dfwin[0:HALO_S, :]

        @pl.when(i == nt - 1)
        def _():
            pwin[0:HALO_A, :] = jnp.zeros((HALO_A, c), F32)
            ewin[0:HALO_S, :] = jnp.zeros((HALO_S, c), F32)

        @pl.when(i < nt - 1)
        def _():
            pwin[0:HALO_A, :] = ph_ref[...]
            ewin[0:HALO_S, :] = eh_ref[...]

        pwin[HALO_A:HALO_A + ts, :] = p_ref[...]
        ewin[HALO_S:HALO_S + ts, :] = e_ref[...]

        def colsum(v):
            return jnp.sum(v, axis=0, keepdims=True)

        q_v = q_ref[...]
        mu = jnp.mean(q_v, axis=-1, keepdims=True)
        xc = q_v - mu
        rstd = lax.rsqrt(jnp.mean(xc * xc, axis=-1, keepdims=True) + EPS)
        nrm = xc * rstd
        ln = nrm * lg_ref[...] + lb_ref[...]
        dln = dy_ref[:, 0:c] * _dsilu(ln, _sigmoid(ln))
        dlg_ref[...] += colsum(dln * nrm)
        dlb_ref[...] += colsum(dln)
        dn = dln * lg_ref[...]
        dq = rstd * (dn - jnp.mean(dn, axis=-1, keepdims=True) - nrm * jnp.mean(dn * nrm, axis=-1, keepdims=True))
        dba_ref[...] += colsum(dq)
        dqwin[0:ts, :] = dq
        _build_phase_copies(dqwin, q_copies, rows_q, c)
        _build_phase_copies(pwin, p_copies, rows_p, c)
        _conv_rows(dqwin, wa_ref, offs_at, ts, c, _store_into(buf), copies=q_copies)
        _conv_wgrad(dqwin, pwin, offs_ac, ts, c, dwa_ref, copies=p_copies)
        dp = buf[...]
        a_val = u_ref[:, 0:c] + bin_ref[:, 0:c]
        sg = _sigmoid(u_ref[:, c:2 * c] + bin_ref[:, c:2 * c])
        d_aval = dp * sg
        d_agate = dp * a_val * sg * (1.0 - sg)
        du_ref[:, 0:c] = d_aval.astype(BF16)
        du_ref[:, c:2 * c] = d_agate.astype(BF16)
        dbin_ref[:, 0:c] += colsum(d_aval)
        dbin_ref[:, c:2 * c] += colsum(d_agate)

        ds = dy_ref[:, c:2 * c]
        _conv_rows(ewin, wb_ref, offs_bc, ts, c, _store_into(buf))
        d_gb = ds * buf[...]
        dfwin[0:ts, :] = ds * (u_ref[:, 2 * c:3 * c] + bin_ref[:, 2 * c:3 * c])
        _conv_rows(dfwin, wb_ref, offs_bt, ts, c, _store_into(buf))
        _conv_wgrad(dfwin, ewin, offs_bc, ts, c, dwb_ref)
        de = buf[...]
        d_gc = de * (u_ref[:, 4 * c:5 * c] + bin_ref[:, 4 * c:5 * c])
        d_v = de * (u_ref[:, 3 * c:4 * c] + bin_ref[:, 3 * c:4 * c])
        du_ref[:, 2 * c:3 * c] = d_gb.astype(BF16)
        du_ref[:, 3 * c:4 * c] = d_gc.astype(BF16)
        du_ref[:, 4 * c:5 * c] = d_v.astype(BF16)
        dbin_ref[:, 2 * c:3 * c] += colsum(d_gb)
        dbin_ref[:, 3 * c:4 * c] += colsum(d_gc)
        dbin_ref[:, 4 * c:5 * c] += colsum(d_v)

    rev = lambda i: (nt - 1 - i, 0)
    tile = lambda w: pl.BlockSpec((ts, w), rev)
    halo = lambda h: pl.BlockSpec((h, c), lambda i: (jnp.maximum((nt - 1 - i) * (ts // h) - 1, 0), 0))
    small = [(1, 5 * c), (K_A, c), (1, c), (1, c), (1, c), (K_B, c)]
    return _call(
        body, name=name, grid=(nt,),
        in_specs=[tile(2 * c), tile(5 * c), tile(c), halo(HALO_A), tile(c), tile(c), halo(HALO_S),
                  _full((1, 5 * c)), _full((K_A, c)), _full((1, c)), _full((1, c)), _full((K_B, c))],
        out_specs=[tile(5 * c)] + [_full(sh) for sh in small],
        out_shape=[jax.ShapeDtypeStruct((s, 5 * c), BF16)] + [jax.ShapeDtypeStruct(sh, F32) for sh in small],
        scratch_shapes=[pltpu.VMEM((ts + HALO_A, c), F32), pltpu.VMEM((ts + HALO_S, c), F32),
                        pltpu.VMEM((ts + HALO_A, c), F32), pltpu.VMEM((ts + HALO_S, c), F32),
                        pltpu.VMEM((ts, c), F32), pltpu.VMEM((len(ph_p), rows_p, c), F32),
                        pltpu.VMEM((len(ph_q), rows_q, c), F32)],
        args=(dy, u, p, p, q, e, e, _row(b_in), wa, _row(lg), _row(lb), wb), hook=hook)


def _adamw(g, w, m, v, *, name):
    r, c = g.shape
    tr = _pick(r, (256, 128, 64, 8)) if r > 256 else r

    def body(g_ref, w_ref, m_ref, v_ref, d_ref, nm_ref, nv_ref):
        gv = g_ref[...]
        nm = ADAM_B1 * m_ref[...] + (1.0 - ADAM_B1) * gv
        nv = ADAM_B2 * v_ref[...] + (1.0 - ADAM_B2) * (gv * gv)
        m_hat = nm / (1.0 - ADAM_B1 ** ADAM_STEP)
        v_hat = nv / (1.0 - ADAM_B2 ** ADAM_STEP)
        d_ref[...] = -ADAM_LR * (m_hat / (jnp.sqrt(v_hat) + ADAM_EPS) + ADAM_WD * w_ref[...])
        nm_ref[...] = nm
        nv_ref[...] = nv

    tile = pl.BlockSpec((tr, c), lambda i: (i, 0))
    return pl.pallas_call(body, name=name, grid=(r // tr,), in_specs=[tile] * 4, out_specs=[tile] * 3,
                          out_shape=[jax.ShapeDtypeStruct((r, c), F32)] * 3, compiler_params=_params())(g, w, m, v)


def _cast_place(wshard, chip, *, name):
    l, r2, c = wshard.shape
    tr = r2 // 2

    def body(chip_ref, w_ref, *o_refs):
        for li, o_ref in enumerate(o_refs):
            o_ref[...] = w_ref[li].astype(BF16)

    grid_spec = pltpu.PrefetchScalarGridSpec(
        num_scalar_prefetch=1, grid=(2,),
        in_specs=[pl.BlockSpec((l, tr, c), lambda i, chip_ref: (0, i, 0))],
        out_specs=[pl.BlockSpec((None, tr, c), lambda i, chip_ref: (chip_ref[0], i, 0))] * l)
    return pl.pallas_call(body, name=name, grid_spec=grid_spec,
                          out_shape=[jax.ShapeDtypeStruct((N_CHIPS, r2, c), BF16)] * l, compiler_params=_params())(chip, wshard)


def _pair_sum(g, recv, core, *, name):
    _, _, r, c = g.shape

    def body(core_ref, g_ref, r_ref, o_ref):
        o_ref[...] = (g_ref[...] + r_ref[...].astype(F32)).astype(BF16)

    grid_spec = pltpu.PrefetchScalarGridSpec(
        num_scalar_prefetch=1, grid=(N_CHIPS,),
        in_specs=[pl.BlockSpec((None, None, r, c), lambda i, core_ref: (i, core_ref[0], 0, 0)),
                  pl.BlockSpec((None, r, c), lambda i, core_ref: (i, 0, 0))],
        out_specs=pl.BlockSpec((None, r, c), lambda i, core_ref: (i, 0, 0)))
    return pl.pallas_call(body, name=name, grid_spec=grid_spec,
                          out_shape=jax.ShapeDtypeStruct((N_CHIPS, r, c), BF16), compiler_params=_params())(core, g, recv)


def _chip_sum(own, recv, core_chip, stack, *, name):
    _, r, c = own.shape
    tr = r // 2
    l, layers, buf = stack

    def body(cc_ref, own_ref, r1_ref, r2_ref, r3_ref, *rest):
        o_ref = rest[-1]
        o_ref[...] = ((own_ref[...].astype(F32) + r1_ref[...].astype(F32)) + r2_ref[...].astype(F32)) + r3_ref[...].astype(F32)

    def slot(s):
        return pl.BlockSpec((None, tr, c), lambda i, cc_ref: ((cc_ref[1] + s) % N_CHIPS, i, 0))

    in_specs = [slot(0), slot(1), slot(2), slot(3)]
    args = [own, recv, recv, recv]
    aliases = {}
    if buf is not None:
        aliases = {1 + len(args): 0}
        in_specs.append(ANY)
        args.append(buf)
    grid_spec = pltpu.PrefetchScalarGridSpec(
        num_scalar_prefetch=1, grid=(2,), in_specs=in_specs,
        out_specs=pl.BlockSpec((None, None, tr, c), lambda i, cc_ref: (l, cc_ref[0], i, 0)))
    return pl.pallas_call(body, name=name, grid_spec=grid_spec, input_output_aliases=aliases,
                          out_shape=jax.ShapeDtypeStruct((layers, 2, r, c), F32), compiler_params=_params())(core_chip, *args)


def _place():
    x, y, c = lax.axis_index("x"), lax.axis_index("y"), lax.axis_index("c")
    chips = [(1 - x, y), (x, 1 - y), (1 - x, 1 - y)]
    return x, y, c, 2 * x + y, chips


def _remote(src, dst, send_sems, recv_sems, k, dev):
    return pltpu.make_async_remote_copy(src_ref=src, dst_ref=dst, send_sem=send_sems.at[k], recv_sem=recv_sems.at[k],
                                        device_id=dev, device_id_type=MESH)


def _sibling_halves(grads, *, name):
    n = len(grads)
    out_shape = [jax.ShapeDtypeStruct(g.shape[:1] + g.shape[2:], g.dtype) for g in grads]

    def body(*refs):
        ins, outs = refs[:n], refs[n:2 * n]
        send_sems, recv_sems = refs[2 * n:]
        x, y, c, _, _ = _place()
        cps = [_remote(ins[a].at[:, 1 - c], outs[a], send_sems, recv_sems, a, (x, y, 1 - c)) for a in range(n)]
        for cp in cps:
            cp.start()
        for cp in cps:
            cp.wait()

    return pl.pallas_call(body, name=name, in_specs=[ANY] * n, out_specs=[ANY] * n, out_shape=out_shape,
                          scratch_shapes=[pltpu.SemaphoreType.DMA((n,)), pltpu.SemaphoreType.DMA((n,))])(*grads)


class _Gather:
    def __init__(self, bufs):
        self.operands = list(bufs)
        self.out_shape = [jax.ShapeDtypeStruct(b.shape, b.dtype) for b in bufs]
        self.aliases = {a: a for a in range(len(bufs))}

    def sems(self):
        n = 6 * len(self.operands)
        return [pltpu.SemaphoreType.DMA((n,)), pltpu.SemaphoreType.DMA((n,))]

    def _sends(self, ins, outs, sems):
        x, y, c, j, chips = _place()
        return [_remote(ins[a].at[j, c], outs[a].at[j, c], sems[0], sems[1], 6 * a + k, (cx, cy, c))
                for a in range(len(ins)) for k, (cx, cy) in enumerate(chips)]

    def start(self, ins, outs, sems):
        for cp in self._sends(ins, outs, sems):
            cp.start()

    def finish(self, ins, outs, sems):
        x, y, c, j, chips = _place()
        sib = (x, y, 1 - c)
        passed = []
        for a in range(len(ins)):
            for k, (cx, cy) in enumerate(chips):
                blk = outs[a].at[2 * cx + cy, c]
                _remote(blk, blk, sems[0], sems[1], 6 * a + k, (cx, cy, c)).wait_recv()
                cp = _remote(blk, blk, sems[0], sems[1], 6 * a + 3 + k, sib)
                cp.start()
                passed.append(cp)
        for a in range(len(ins)):
            for k, (cx, cy) in enumerate(chips):
                blk = outs[a].at[2 * cx + cy, 1 - c]
                _remote(blk, blk, sems[0], sems[1], 6 * a + 3 + k, sib).wait_recv()
        for cp in self._sends(ins, outs, sems) + passed:
            cp.wait_send()


class _Scatter:
    def __init__(self, srcs):
        self.operands = list(srcs)
        self.out_shape = [jax.ShapeDtypeStruct(s.shape, s.dtype) for s in srcs]
        self.aliases = {}

    def sems(self):
        n = 3 * len(self.operands)
        return [pltpu.SemaphoreType.DMA((n,)), pltpu.SemaphoreType.DMA((n,))]

    def _sends(self, ins, outs, sems):
        x, y, c, j, chips = _place()
        return [_remote(ins[a].at[2 * cx + cy], outs[a].at[j], sems[0], sems[1], 3 * a + k, (cx, cy, c))
                for a in range(len(ins)) for k, (cx, cy) in enumerate(chips)]

    def start(self, ins, outs, sems):
        for cp in self._sends(ins, outs, sems):
            cp.start()

    def finish(self, ins, outs, sems):
        x, y, c, j, chips = _place()
        for a in range(len(ins)):
            for k, (cx, cy) in enumerate(chips):
                blk = outs[a].at[2 * cx + cy]
                _remote(blk, blk, sems[0], sems[1], 3 * a + k, (cx, cy, c)).wait_recv()
        for cp in self._sends(ins, outs, sems):
            cp.wait_send()


def _standalone(hook, *, name):
    n_in, n_out = len(hook.operands), len(hook.out_shape)

    def body(*refs):
        ins, outs, sems = refs[:n_in], refs[n_in:n_in + n_out], refs[n_in + n_out:]
        hook.start(ins, outs, sems)
        hook.finish(ins, outs, sems)

    return list(pl.pallas_call(body, name=name, in_specs=[ANY] * n_in, out_specs=[ANY] * n_out,
                               out_shape=hook.out_shape, input_output_aliases=hook.aliases,
                               scratch_shapes=hook.sems())(*hook.operands))


def _share_halves(bufs, *, name):
    n = len(bufs)

    def body(*refs):
        ins, outs = refs[:n], refs[n:2 * n]
        send_sems, recv_sems = refs[2 * n:]
        x, y, c, _, _ = _place()
        sends = [_remote(ins[a].at[:, c], outs[a].at[:, c], send_sems, recv_sems, a, (x, y, 1 - c)) for a in range(n)]
        for cp in sends:
            cp.start()
        for a in range(n):
            blk = outs[a].at[:, 1 - c]
            _remote(blk, blk, send_sems, recv_sems, a, (x, y, 1 - c)).wait_recv()
        for cp in sends:
            cp.wait_send()

    return pl.pallas_call(
        body, name=name, in_specs=[ANY] * n, out_specs=[ANY] * n,
        out_shape=[jax.ShapeDtypeStruct(b.shape, b.dtype) for b in bufs],
        input_output_aliases={a: a for a in range(n)},
        scratch_shapes=[pltpu.SemaphoreType.DMA((n,)), pltpu.SemaphoreType.DMA((n,))])(*bufs)


def _allreduce_small(v, *, name):
    r, c = v.shape
    hr = r // 2

    def body(v_ref, o_ref, sib_buf, chip_buf, send_sems, recv_sems):
        x, y, cc, j, chips = _place()
        sib = (x, y, 1 - cc)
        to_sib = _remote(v_ref.at[1 - cc], sib_buf, send_sems, recv_sems, 0, sib)
        to_sib.start()
        to_sib.wait()
        chip_buf[j] = v_ref[cc] + sib_buf[...]
        sends = [_remote(chip_buf.at[j], chip_buf.at[j], send_sems, recv_sems, 1 + k, (cx, cy, cc))
                 for k, (cx, cy) in enumerate(chips)]
        for cp in sends:
            cp.start()
        for k, (cx, cy) in enumerate(chips):
            blk = chip_buf.at[2 * cx + cy]
            _remote(blk, blk, send_sems, recv_sems, 1 + k, (cx, cy, cc)).wait_recv()
        for cp in sends:
            cp.wait_send()
        o_ref[cc] = ((chip_buf[0] + chip_buf[1]) + chip_buf[2]) + chip_buf[3]
        swap = _remote(o_ref.at[cc], o_ref.at[cc], send_sems, recv_sems, 4, sib)
        swap.start()
        _remote(o_ref.at[1 - cc], o_ref.at[1 - cc], send_sems, recv_sems, 4, sib).wait_recv()
        swap.wait_send()

    out = pl.pallas_call(
        body, name=name, in_specs=[VMEM], out_specs=VMEM, out_shape=jax.ShapeDtypeStruct((2, hr, c), F32),
        scratch_shapes=[pltpu.VMEM((hr, c), F32), pltpu.VMEM((N_CHIPS, hr, c), F32), pltpu.SemaphoreType.DMA((5,)),
                        pltpu.SemaphoreType.DMA((5,))])(v.reshape(2, hr, c))
    return out.reshape(r, c)


def _pack(arrays, row_multiple=8):
    flat = jnp.concatenate([a.reshape(-1) for a in arrays])
    rows = -(-flat.shape[0] // LANES)
    rows = -(-rows // row_multiple) * row_multiple
    return jnp.pad(flat, (0, rows * LANES - flat.shape[0])).reshape(rows, LANES)


def _unpack(slab, shapes):
    flat = slab.reshape(-1)
    out, pos = [], 0
    for sh in shapes:
        size = 1
        for dim in sh:
            size *= dim
        out.append(flat[pos:pos + size].reshape(sh))
        pos += size
    return out


def kernel(x, mix_norm_g, w_in, b_in, conv_a_w, conv_a_b, ln_a_g, ln_a_b, conv_b_w, w_out, ffn_norm_g, w_up, conv_f_w, w_down, final_norm_g, loss_target, m_mix_norm_g, m_w_in, m_b_in, m_conv_a_w, m_conv_a_b, m_ln_a_g, m_ln_a_b, m_conv_b_w, m_w_out, m_ffn_norm_g, m_w_up, m_conv_f_w, m_w_down, m_final_norm_g, v_mix_norm_g, v_w_in, v_b_in, v_conv_a_w, v_conv_a_b, v_ln_a_g, v_ln_a_b, v_conv_b_w, v_w_out, v_ffn_norm_g, v_w_up, v_conv_f_w, v_w_down, v_final_norm_g):
    w = dict(mix_norm_g=mix_norm_g, w_in=w_in, b_in=b_in, conv_a_w=conv_a_w, conv_a_b=conv_a_b, ln_a_g=ln_a_g,
             ln_a_b=ln_a_b, conv_b_w=conv_b_w, w_out=w_out, ffn_norm_g=ffn_norm_g, w_up=w_up, conv_f_w=conv_f_w,
             w_down=w_down, final_norm_g=final_norm_g)
    m = dict(mix_norm_g=m_mix_norm_g, w_in=m_w_in, b_in=m_b_in, conv_a_w=m_conv_a_w, conv_a_b=m_conv_a_b,
             ln_a_g=m_ln_a_g, ln_a_b=m_ln_a_b, conv_b_w=m_conv_b_w, w_out=m_w_out, ffn_norm_g=m_ffn_norm_g,
             w_up=m_w_up, conv_f_w=m_conv_f_w, w_down=m_w_down, final_norm_g=m_final_norm_g)
    v = dict(mix_norm_g=v_mix_norm_g, w_in=v_w_in, b_in=v_b_in, conv_a_w=v_conv_a_w, conv_a_b=v_conv_a_b,
             ln_a_g=v_ln_a_g, ln_a_b=v_ln_a_b, conv_b_w=v_conv_b_w, w_out=v_w_out, ffn_norm_g=v_ffn_norm_g,
             w_up=v_w_up, conv_f_w=v_conv_f_w, w_down=v_w_down, final_norm_g=v_final_norm_g)
    depth = w_in.shape[0]
    xs = x[0]
    tgt = loss_target[0]
    chip = (2 * lax.axis_index("x") + lax.axis_index("y")).astype(jnp.int32)
    chip_arr = chip.reshape(1)
    core = lax.axis_index("c").astype(jnp.int32).reshape(1)
    core_chip = jnp.concatenate([core, chip_arr])

    def halves(b):
        return b.reshape(N_CHIPS, 2, b.shape[1] // 2, b.shape[2])

    conv_shapes = [w[nm].shape for nm in CHANNEL_SHARDED]
    conv_slab = _pack([w[nm] for nm in CHANNEL_SHARDED], row_multiple=16)
    zero = jnp.int32(0)
    slab_buf = lax.dynamic_update_slice(jnp.zeros((N_CHIPS,) + conv_slab.shape, F32), conv_slab[None], (chip, zero, zero))
    wbuf = {}
    for nm in BIG:
        for l, b in enumerate(_cast_place(w[nm], chip_arr, name=f"cast_{nm}")):
            wbuf[nm, l] = halves(b)

    def gather_into(keys, done):
        for key, g in zip(keys, done):
            wbuf[key] = g

    def wmat(nm, l):
        g = wbuf[nm, l]
        if nm in ("w_in", "w_up"):
            return g.reshape(N_CHIPS, 2 * g.shape[2], g.shape[3])
        return g.reshape(N_CHIPS * 2 * g.shape[2], g.shape[3])

    first = [("w_in", 0)]
    done = _standalone(_Gather([wbuf[k] for k in first] + [halves(slab_buf)]), name="gather_first")
    gather_into(first, done)
    conv_full = done[-1].reshape(N_CHIPS, -1, LANES)
    taps = {}
    for nm, parts in zip(CHANNEL_SHARDED, zip(*[_unpack(conv_full[jj], conv_shapes) for jj in range(N_CHIPS)])):
        taps[nm] = jnp.concatenate(parts, axis=-1)

    assert depth == 2
    plan = {("mm_in", 0): [("w_out", 0)], ("mix_fwd", 0): [("w_up", 0)], ("mm_up", 0): [("w_down", 0), ("w_in", 1)],
            ("ffn_fwd", 0): [("w_up", 1)], ("mix_fwd", 1): [("w_out", 1), ("w_down", 1)]}

    def host(kind, l, run):
        keys = plan.get((kind, l), [])
        results, done = run(_Gather([wbuf[k] for k in keys]) if keys else None)
        gather_into(keys, done)
        return results

    saved = []
    cur = xs
    h1 = _norm_fwd(cur, w["mix_norm_g"][0], name="norm_mix_0")
    for l in range(depth):
        if ("mm_in", l) in plan:
            u = host("mm_in", l, lambda hook: _mm(h1, wmat("w_in", l), mode="nn", b_sharded=True, name=f"mm_in_{l}",
                                                  hook=hook))
        else:
            u = _mm(h1, wmat("w_in", l), mode="nn", b_sharded=True, name=f"mm_in_{l}")
        y, p, q, e = host("mix_fwd", l, lambda hook: _mix_pw_fwd(
            u, w["b_in"][l], taps["conv_a_w"][l], w["conv_a_b"][l], w["ln_a_g"][l], w["ln_a_b"][l],
            taps["conv_b_w"][l], name=f"mix_fwd_{l}", hook=hook))
        x1, h2 = _mm_res_norm(y, wmat("w_out", l), cur, w["ffn_norm_g"][l], name=f"mm_out_{l}")
        if ("mm_up", l) in plan:
            u2 = host("mm_up", l, lambda hook: _mm(h2, wmat("w_up", l), mode="nn", b_sharded=True, name=f"mm_up_{l}",
                                                   hook=hook))
        else:
            u2 = _mm(h2, wmat("w_up", l), mode="nn", b_sharded=True, name=f"mm_up_{l}")
        (z,) = host("ffn_fwd", l, lambda hook: _ffn_pw_fwd(u2, taps["conv_f_w"][l], name=f"ffn_fwd_{l}", hook=hook))
        saved.append((cur, h1, u, y, p, q, e, x1, h2, u2, z))
        if l + 1 < depth:
            cur, h1 = _mm_res_norm(z, wmat("w_down", l), x1, w["mix_norm_g"][l + 1], name=f"mm_down_{l}")
        else:
            cur = _mm(z, wmat("w_down", l), mode="nn", res=x1, name=f"mm_down_{l}")

    loss_blk, dx, dxb, dgf = _loss_bwd(cur, tgt, w["final_norm_g"], name="loss_bwd")
    loss = lax.psum(loss_blk[0, 0], ("x", "y", "c"))

    gfull, chip_sums, from_chips = {}, {}, {}

    def four_d(g):
        if g.ndim == 2:
            g = g.reshape(N_CHIPS, g.shape[0] // N_CHIPS, g.shape[1])
        return halves(g)

    def scatter_hook(keys, tag):
        from_sibling = _sibling_halves([four_d(gfull[k][1]) for k in keys], name=f"reduce_sibling_{tag}")
        for k, r in zip(keys, from_sibling):
            chip_sums[k] = _pair_sum(four_d(gfull[k][0]), r, core, name=f"pair_sum_{k[0]}_{k[1]}")
        return _Scatter([chip_sums[k] for k in keys])

    sg = {nm: [None] * depth for nm in SMALL if nm != "final_norm_g"}
    for l in reversed(range(depth)):
        x0, h1, u, y, p, q, e, x1, h2, u2, z = saved[l]
        dz = _mm(dxb, wmat("w_down", l), mode="nt", name=f"mm_ddown_{l}")
        gfull["w_down", l] = _mm(z, dxb, mode="tn", also_bf16=True, name=f"mm_gdown_{l}")
        keys = [("w_down", l)]
        (du2, dwf), done = _ffn_pw_bwd(u2, dz, taps["conv_f_w"][l], name=f"ffn_bwd_{l}", hook=scatter_hook(keys, f"a{l}"))
        from_chips.update(zip(keys, done))
        (dx1, dx1b, dg2), _ = _mm_norm_bwd(du2, wmat("w_up", l), x1, dx, w["ffn_norm_g"][l], name=f"mm_dup_{l}")
        gfull["w_up", l] = _mm(h2, du2, mode="tn", out_sharded=True, also_bf16=True, name=f"mm_gup_{l}")
        dy = _mm(dx1b, wmat("w_out", l), mode="nt", name=f"mm_dout_{l}")
        gfull["w_out", l] = _mm(y, dx1b, mode="tn", also_bf16=True, name=f"mm_gout_{l}")
        keys = [("w_up", l), ("w_out", l)]
        (du, dbin, dwa, dba, dlg, dlb, dwb), done = _mix_pw_bwd(
            dy, u, p, q, e, w["b_in"][l], taps["conv_a_w"][l], w["ln_a_g"][l], w["ln_a_b"][l],
            taps["conv_b_w"][l], name=f"mix_bwd_{l}", hook=scatter_hook(keys, f"b{l}"))
        from_chips.update(zip(keys, done))
        gfull["w_in", l] = _mm(h1, du, mode="tn", out_sharded=True, also_bf16=True, name=f"mm_gin_{l}")
        keys = [("w_in", l)]
        (dx, dxb, dg1), done = _mm_norm_bwd(du, wmat("w_in", l), x0, dx1, w["mix_norm_g"][l], name=f"mm_din_{l}",
                                            hook=scatter_hook(keys, f"c{l}"))
        from_chips.update(zip(keys, done))
        for nm, g in (("mix_norm_g", dg1), ("b_in", dbin), ("conv_a_w", dwa), ("conv_a_b", dba), ("ln_a_g", dlg),
                      ("ln_a_b", dlb), ("conv_b_w", dwb), ("ffn_norm_g", dg2), ("conv_f_w", dwf)):
            sg[nm][l] = g.reshape(g.shape[-1]) if g.shape[0] == 1 else g
    grad_x = dx[None]

    mine = []
    for nm in BIG:
        buf = None
        for l in range(depth):
            buf = _chip_sum(chip_sums[nm, l], from_chips[nm, l], core_chip, (l, depth, buf), name=f"chip_sum_{nm}_{l}")
        mine.append(buf)
    shared = _share_halves(mine, name="share_halves")
    grads = {}
    for nm, g in zip(BIG, shared):
        grads[nm] = g.reshape(w[nm].shape)

    small_full_shapes = []
    small_parts = []
    for nm in SMALL:
        if nm == "final_norm_g":
            g = dgf.reshape(-1)
        else:
            g = jnp.stack(sg[nm])
        small_parts.append(g)
        small_full_shapes.append(g.shape)
    summed = _unpack(_allreduce_small(_pack(small_parts), name="reduce_small"), small_full_shapes)
    for nm, g in zip(SMALL, summed):
        if nm in CHANNEL_SHARDED:
            width = w[nm].shape[-1]
            g = lax.dynamic_slice_in_dim(g, chip * width, width, axis=2)
        grads[nm] = g

    delta, new_m, new_v = {}, {}, {}
    for nm in BIG:
        sh = w[nm].shape
        two_d = lambda a: a.reshape(sh[0] * sh[1], sh[2])
        d_, m_, v_ = _adamw(two_d(grads[nm]), two_d(w[nm]), two_d(m[nm]), two_d(v[nm]), name=f"adamw_{nm}")
        delta[nm], new_m[nm], new_v[nm] = d_.reshape(sh), m_.reshape(sh), v_.reshape(sh)
    small_shapes = [w[nm].shape for nm in SMALL]
    outs = _adamw(_pack([grads[nm] for nm in SMALL]), _pack([w[nm] for nm in SMALL]), _pack([m[nm] for nm in SMALL]),
                  _pack([v[nm] for nm in SMALL]), name="adamw_small")
    for store, slab in zip((delta, new_m, new_v), outs):
        for nm, a in zip(SMALL, _unpack(slab, small_shapes)):
            store[nm] = a

    return (loss, grad_x, *[grads[nm] for nm in TWIN_WEIGHTS], *[delta[nm] for nm in TWIN_WEIGHTS],
            *[new_m[nm] for nm in TWIN_WEIGHTS], *[new_v[nm] for nm in TWIN_WEIGHTS])
```
